```python
import math
import jax, jax.numpy as jnp
from jax import lax
import numpy as np

D_MODEL = 1024
BATCH = 4
SEQ = 4096
DEPTH = 4
DEC_BATCH = 2
DEC_SEQ = 16384
PAST_LEN = 128

ATTN_HEADS = 8
HEAD_DIM = 64
V_DIM = 2 * HEAD_DIM
QK_COLS = ATTN_HEADS * 2 * HEAD_DIM
ATTN_WIDTH = ATTN_HEADS * V_DIM
Q_BLOCK = 128
SSD_HEADS = 16
SSD_HEAD_DIM = 64
D_INNER = SSD_HEADS * SSD_HEAD_DIM
SSD_GROUPS = 2
D_STATE = 128
SSD_CHUNK = 128
SSD_CONV = 3
CONV_DIM = D_INNER + 2 * SSD_GROUPS * D_STATE
D_FF = 2752
FFN_CONV = 3
EPS = 1e-6
IN_COLS = 2 * QK_COLS + ATTN_WIDTH + D_INNER + CONV_DIM + 2 * SSD_HEADS + 2 * D_MODEL

kernel_name = "hybrid_diffattn_ssd_encoder"


def rmsnorm(x, w):
    xf = x.astype(jnp.float32)
    y = xf * lax.rsqrt(jnp.mean(xf * xf, axis=-1, keepdims=True) + EPS)
    return (y * w.astype(jnp.float32)).astype(x.dtype)


def dwconv_centred(x, w, b):
    K = w.shape[0]
    pad = K // 2
    S = x.shape[1]
    xp = jnp.pad(x, ((0, 0), (pad, pad), (0, 0)))
    out = b
    for k in range(K):
        out = out + xp[:, k:k + S] * w[k]
    return out


def alibi_slopes():
    return jnp.asarray([2.0 ** (-8.0 * (i + 1) / ATTN_HEADS) for i in range(ATTN_HEADS)], jnp.float32)


def diff_attention(q, k, v, lam, lam_init, subln_w):
    b, S = q.shape[0], q.shape[1]
    nb = S // Q_BLOCK
    scale = 1.0 / math.sqrt(HEAD_DIM)
    qb = q.reshape(b, nb, Q_BLOCK, ATTN_HEADS, 2, HEAD_DIM).transpose(1, 0, 3, 4, 2, 5)
    kt = k.transpose(0, 2, 3, 1, 4)
    vt = v.transpose(0, 2, 1, 3)
    slopes = alibi_slopes()
    kpos = jnp.arange(S, dtype=jnp.int32)
    starts = jnp.arange(nb, dtype=jnp.int32) * Q_BLOCK

    def block(args):
        qi, start = args
        s = jnp.einsum('bhmqd,bhmkd->bhmqk', qi, kt).astype(jnp.float32) * scale
        qpos = start + jnp.arange(Q_BLOCK, dtype=jnp.int32)
        dist = jnp.abs(qpos[:, None] - kpos[None, :]).astype(jnp.float32)
        s = s - slopes[:, None, None, None] * dist
        p = jax.nn.softmax(s, axis=-1)
        wmap = (p[:, :, 0] - lam * p[:, :, 1]).astype(vt.dtype)
        return jnp.einsum('bhqk,bhkv->bhqv', wmap, vt)

    o = lax.map(block, (qb, starts))
    o = o.transpose(1, 0, 3, 2, 4).reshape(b, S, ATTN_HEADS, V_DIM)
    o = rmsnorm(o, subln_w) * (1.0 - lam_init)
    return o.reshape(b, S, ATTN_WIDTH)


def ssd_causal(x, dt, A, B, C):
    b, S, H, P = x.shape
    G, N = B.shape[2], B.shape[3]
    hg = H // G
    L = SSD_CHUNK
    c = S // L
    xd = (x * dt[..., None].astype(x.dtype)).reshape(b, c, L, G, hg, P)
    a = (dt * A).reshape(b, c, L, G, hg).transpose(0, 3, 4, 1, 2)
    acum = jnp.cumsum(a, axis=-1)
    Bc = B.reshape(b, c, L, G, N)
    Cc = C.reshape(b, c, L, G, N)
    causal = jnp.tril(jnp.ones((L, L), dtype=bool))
    seg = acum[..., :, None] - acum[..., None, :]
    decay_in = jnp.exp(jnp.where(causal, seg, -jnp.inf)).astype(x.dtype)
    cb = jnp.einsum('bclgn,bcsgn->bgcls', Cc, Bc)
    y_diag = jnp.einsum('bgcls,bghcls,bcsghp->bclghp', cb, decay_in, xd)
    decay_to_end = jnp.exp(acum[..., -1:] - acum).astype(x.dtype)
    chunk_states = jnp.einsum('bcsgn,bghcs,bcsghp->cbghpn', Bc, decay_to_end, xd)
    chunk_decay = jnp.exp(acum[..., -1]).transpose(3, 0, 1, 2).astype(x.dtype)

    def step(carry, inp):
        st, dec = inp
        return carry * dec[..., None, None] + st, carry

    init = jnp.zeros((b, G, hg, P, N), x.dtype)
    _, states_in = lax.scan(step, init, (chunk_states, chunk_decay))
    decay_from_start = jnp.exp(acum).astype(x.dtype)
    y_off = jnp.einsum('bclgn,bghcl,cbghpn->bclghp', Cc, decay_from_start, states_in)
    return (y_diag + y_off).reshape(b, S, H, P)


def ssd_branch(z, xbc, dt_raw, conv_w, conv_b, dt_bias, a_log, d_skip, norm_w):
    b, S = z.shape[0], z.shape[1]
    xbc = jax.nn.silu(dwconv_centred(xbc, conv_w, conv_b))
    xs = xbc[..., :D_INNER].reshape(b, S, SSD_HEADS, SSD_HEAD_DIM)
    Bm = xbc[..., D_INNER:D_INNER + SSD_GROUPS * D_STATE].reshape(b, S, SSD_GROUPS, D_STATE)
    Cm = xbc[..., D_INNER + SSD_GROUPS * D_STATE:].reshape(b, S, SSD_GROUPS, D_STATE)
    dt = jax.nn.softplus(dt_raw.astype(jnp.float32).reshape(b, S, 2, SSD_HEADS) + dt_bias.astype(jnp.float32))
    A = -jnp.exp(a_log.astype(jnp.float32))
    y_f = ssd_causal(xs, dt[:, :, 0], A[0], Bm, Cm)
    y_b = jnp.flip(ssd_causal(jnp.flip(xs, 1), jnp.flip(dt[:, :, 1], 1), A[1], jnp.flip(Bm, 1), jnp.flip(Cm, 1)), 1)
    y = y_f + y_b + xs * d_skip[:, None]
    y = y.reshape(b, S, D_INNER) * jax.nn.silu(z)
    y = rmsnorm(y.reshape(b, S, SSD_GROUPS, D_INNER // SSD_GROUPS),
                norm_w.reshape(SSD_GROUPS, D_INNER // SSD_GROUPS))
    return y.reshape(b, S, D_INNER)


def encoder_layer(x, lam_init, norm_mix_pre, norm_mix_post, norm_ffn_pre, norm_ffn_post, w_in,
                  lam_q1, lam_k1, lam_q2, lam_k2, attn_subln, conv_ssd_w, conv_ssd_b, dt_bias,
                  a_log, d_skip, ssd_norm, w_o_attn, w_o_ssd, w_out, w_up, conv_ffn_w, conv_ffn_b, w_down):
    b, S, _ = x.shape
    h = rmsnorm(x, norm_mix_pre)
    proj = h @ w_in
    cuts = np.cumsum([QK_COLS, QK_COLS, ATTN_WIDTH, D_INNER, CONV_DIM, 2 * SSD_HEADS]).tolist()
    q, k, v, z, xbc, dt_raw, gates = jnp.split(proj, cuts, axis=-1)
    q = q.reshape(b, S, ATTN_HEADS, 2, HEAD_DIM)
    k = k.reshape(b, S, ATTN_HEADS, 2, HEAD_DIM)
    v = v.reshape(b, S, ATTN_HEADS, V_DIM)
    lam = (jnp.exp(jnp.sum(lam_q1.astype(jnp.float32) * lam_k1.astype(jnp.float32)))
           - jnp.exp(jnp.sum(lam_q2.astype(jnp.float32) * lam_k2.astype(jnp.float32))) + lam_init)
    attn_out = diff_attention(q, k, v, lam, lam_init, attn_subln) @ w_o_attn
    ssd_out = ssd_branch(z, xbc, dt_raw, conv_ssd_w, conv_ssd_b, dt_bias, a_log, d_skip, ssd_norm) @ w_o_ssd
    g_a, g_s = jnp.split(gates, 2, axis=-1)
    merged = jax.nn.sigmoid(g_a) * attn_out + jax.nn.sigmoid(g_s) * ssd_out
    x = x + rmsnorm(merged @ w_out, norm_mix_post)
    h = rmsnorm(x, norm_ffn_pre)
    u = dwconv_centred(h @ w_up, conv_ffn_w, conv_ffn_b)
    a, g = jnp.split(u, 2, axis=-1)
    x = x + rmsnorm((jax.nn.silu(g) * a) @ w_down, norm_ffn_post)
    return x


def setup_inputs(seed: int = 0) -> dict:
    key = jax.random.key(seed)
    ks = jax.random.split(key, 32)
    f32 = jnp.float32

    def nrm(k, shape, scale):
        return jax.random.normal(k, shape, f32) * scale

    def gain(k, shape):
        return 1.0 + 0.02 * jax.random.normal(k, shape, f32)

    dt0 = jnp.exp(jax.random.uniform(ks[10], (DEPTH, 2, SSD_HEADS), f32, math.log(1e-3), math.log(0.1)))
    dt_bias = dt0 + jnp.log(-jnp.expm1(-dt0))
    a_log = jnp.log(jax.random.uniform(ks[11], (DEPTH, 2, SSD_HEADS), f32, 1.0, 16.0))
    return {
        "x_prompt": jax.random.normal(ks[0], (BATCH, SEQ, D_MODEL), f32),
        "x_sample": jax.random.normal(ks[1], (DEC_BATCH, DEC_SEQ, D_MODEL), f32),
        "norm_mix_pre": gain(ks[2], (DEPTH, D_MODEL)),
        "norm_mix_post": gain(ks[3], (DEPTH, D_MODEL)),
        "norm_ffn_pre": gain(ks[4], (DEPTH, D_MODEL)),
        "norm_ffn_post": gain(ks[5], (DEPTH, D_MODEL)),
        "w_in": nrm(ks[6], (DEPTH, D_MODEL, IN_COLS), D_MODEL ** -0.5),
        "lam_q1": nrm(ks[7], (DEPTH, HEAD_DIM), 0.1),
        "lam_k1": nrm(ks[8], (DEPTH, HEAD_DIM), 0.1),
        "lam_q2": nrm(ks[9], (DEPTH, HEAD_DIM), 0.1),
        "lam_k2": nrm(ks[12], (DEPTH, HEAD_DIM), 0.1),
        "attn_subln": gain(ks[13], (DEPTH, V_DIM)),
        "conv_ssd_w": nrm(ks[14], (DEPTH, SSD_CONV, CONV_DIM), SSD_CONV ** -0.5),
        "conv_ssd_b": nrm(ks[15], (DEPTH, CONV_DIM), 0.02),
        "dt_bias": dt_bias,
        "a_log": a_log,
        "d_skip": gain(ks[16], (DEPTH, SSD_HEADS)),
        "ssd_norm": gain(ks[17], (DEPTH, D_INNER)),
        "w_o_attn": nrm(ks[18], (DEPTH, ATTN_WIDTH, D_MODEL), ATTN_WIDTH ** -0.5),
        "w_o_ssd": nrm(ks[19], (DEPTH, D_INNER, D_MODEL), D_INNER ** -0.5),
        "w_out": nrm(ks[20], (DEPTH, D_MODEL, D_MODEL), D_MODEL ** -0.5),
        "w_up": nrm(ks[21], (DEPTH, D_MODEL, 2 * D_FF), D_MODEL ** -0.5),
        "conv_ffn_w": nrm(ks[22], (DEPTH, FFN_CONV, 2 * D_FF), FFN_CONV ** -0.5),
        "conv_ffn_b": nrm(ks[23], (DEPTH, 2 * D_FF), 0.02),
        "w_down": nrm(ks[24], (DEPTH, D_FF, D_MODEL), D_FF ** -0.5),
    }


def reference(x_prompt, x_sample, norm_mix_pre, norm_mix_post, norm_ffn_pre, norm_ffn_post, w_in,
              lam_q1, lam_k1, lam_q2, lam_k2, attn_subln, conv_ssd_w, conv_ssd_b, dt_bias, a_log,
              d_skip, ssd_norm, w_o_attn, w_o_ssd, w_out, w_up, conv_ffn_w, conv_ffn_b, w_down):
    y_prompt = x_prompt
    y_sample = x_sample
    for l in range(DEPTH):
        lam_init = 0.8 - 0.6 * math.exp(-0.3 * l)
        lw = (norm_mix_pre[l], norm_mix_post[l], norm_ffn_pre[l], norm_ffn_post[l], w_in[l],
              lam_q1[l], lam_k1[l], lam_q2[l], lam_k2[l], attn_subln[l], conv_ssd_w[l], conv_ssd_b[l],
              dt_bias[l], a_log[l], d_skip[l], ssd_norm[l], w_o_attn[l], w_o_ssd[l], w_out[l],
              w_up[l], conv_ffn_w[l], conv_ffn_b[l], w_down[l])
        y_prompt = encoder_layer(y_prompt, lam_init, *lw)
        y_sample = encoder_layer(y_sample, lam_init, *lw)
    return (y_prompt, y_sample)
```

```python
import functools
import math

import jax
import jax.numpy as jnp
from jax import lax
from jax.experimental import pallas as pl
from jax.experimental.pallas import tpu as pltpu

F32 = jnp.float32
BF16 = jnp.bfloat16

D_MODEL = 1024
ATTN_HEADS = 8
HEAD_DIM = 64
V_DIM = 2 * HEAD_DIM
SSD_HEADS = 16
SSD_HEAD_DIM = 64
D_INNER = SSD_HEADS * SSD_HEAD_DIM
SSD_GROUPS = 2
GROUP_COLS = D_INNER // SSD_GROUPS
D_STATE = 128
CHUNK = 128
CONV_DIM = D_INNER + 2 * SSD_GROUPS * D_STATE
D_FF = 2752
EPS = 1e-6

LANES = 128
D_FF_PAD = 2816
DT_PAD = LANES
HALO = 16
VMEM_LIMIT = 56 * 1024 * 1024


def _params(sem):
    return pltpu.CompilerParams(dimension_semantics=sem, vmem_limit_bytes=VMEM_LIMIT)


def _rms(x, g):
    ms = jnp.mean(x * x, axis=-1, keepdims=True)
    return x * lax.rsqrt(ms + EPS) * g


def _dot(a, b):
    return jnp.dot(a, b, preferred_element_type=F32)


def _dot_nt(a, b):
    return lax.dot_general(a, b, (((1,), (1,)), ((), ())), preferred_element_type=F32)


def _dot_f32_by_01(x, m01):
    hi = x.astype(BF16)
    r1 = x - hi.astype(F32)
    mid = r1.astype(BF16)
    lo = (r1 - mid.astype(F32)).astype(BF16)
    return _dot(hi, m01) + _dot(mid, m01) + _dot(lo, m01)


def _dot_01_by_f32(m01, x):
    hi = x.astype(BF16)
    r1 = x - hi.astype(F32)
    mid = r1.astype(BF16)
    lo = (r1 - mid.astype(F32)).astype(BF16)
    return _dot(m01, hi) + _dot(m01, mid) + _dot(m01, lo)


def _softplus(x):
    return jnp.maximum(x, 0.0) + jnp.log1p(jnp.exp(-jnp.abs(x)))


def _silu(x):
    return x * jax.nn.sigmoid(x)


def _norm_matmul_kernel(x_ref, g_ref, w_ref, o_ref, h_ref):
    @pl.when(pl.program_id(1) == 0)
    def _():
        h_ref[...] = _rms(x_ref[...], g_ref[...]).astype(BF16)

    o_ref[...] = _dot(h_ref[...], w_ref[...]).astype(o_ref.dtype)


def norm_matmul(x, g, w, out_dtype, tm, tn):
    t, d = x.shape
    n = w.shape[1]
    return pl.pallas_call(
        _norm_matmul_kernel,
        grid=(t // tm, n // tn),
        in_specs=[
            pl.BlockSpec((tm, d), lambda i, j: (i, 0)),
            pl.BlockSpec((1, d), lambda i, j: (0, 0)),
            pl.BlockSpec((d, tn), lambda i, j: (0, j)),
        ],
        out_specs=pl.BlockSpec((tm, tn), lambda i, j: (i, j)),
        out_shape=jax.ShapeDtypeStruct((t, n), out_dtype),
        scratch_shapes=[pltpu.VMEM((tm, d), BF16)],
        compiler_params=_params(("parallel", "arbitrary")),
        name="norm_matmul",
    )(x, g, w)


def _norm_matmul_t_kernel(x_ref, g_ref, wt_ref, o_ref, h_ref):
    @pl.when(pl.program_id(1) == 0)
    def _():
        h_ref[...] = _rms(x_ref[...], g_ref[...]).astype(BF16)

    o_ref[...] = _dot_nt(wt_ref[...], h_ref[...]).astype(o_ref.dtype)


def norm_matmul_t(x, g, wt, out_dtype, tm, tn):
    t, d = x.shape
    n = wt.shape[0]
    return pl.pallas_call(
        _norm_matmul_t_kernel,
        grid=(t // tm, n // tn),
        in_specs=[
            pl.BlockSpec((tm, d), lambda i, j: (i, 0)),
            pl.BlockSpec((1, d), lambda i, j: (0, 0)),
            pl.BlockSpec((tn, d), lambda i, j: (j, 0)),
        ],
        out_specs=pl.BlockSpec((tn, tm), lambda i, j: (j, i)),
        out_shape=jax.ShapeDtypeStruct((n, t), out_dtype),
        scratch_shapes=[pltpu.VMEM((tm, d), BF16)],
        compiler_params=_params(("parallel", "arbitrary")),
        name="norm_matmul_t",
    )(x, g, wt)


def _dt_proj_kernel(x_ref, g_ref, w_ref, wt_ref, dt_ref, dtt_ref):
    h = _rms(x_ref[...], g_ref[...]).astype(BF16)
    dt_ref[...] = _dot(h, w_ref[...])
    dtt_ref[...] = _dot_nt(wt_ref[...], h)


def dt_proj(x, g, w, wt, tm):
    t, d = x.shape
    r = wt.shape[0]
    return pl.pallas_call(
        _dt_proj_kernel,
        grid=(t // tm,),
        in_specs=[
            pl.BlockSpec((tm, d), lambda i: (i, 0)),
            pl.BlockSpec((1, d), lambda i: (0, 0)),
            pl.BlockSpec((d, DT_PAD), lambda i: (0, 0)),
            pl.BlockSpec((r, d), lambda i: (0, 0)),
        ],
        out_specs=[
            pl.BlockSpec((tm, DT_PAD), lambda i: (i, 0)),
            pl.BlockSpec((r, tm), lambda i: (0, i)),
        ],
        out_shape=[
            jax.ShapeDtypeStruct((t, DT_PAD), F32),
            jax.ShapeDtypeStruct((r, t), F32),
        ],
        compiler_params=_params(("parallel",)),
        name="dt_proj",
    )(x, g, w, wt)


def _attn_kernel(slopes_ref, lam_ref, q_ref, k_ref, vt_ref, lq1_ref, lk1_ref, lq2_ref, lk2_ref,
                 subln_ref, o_ref, *, tq, tk, nk):
    h = pl.program_id(1)
    qi = pl.program_id(2)
    slope = slopes_ref[h]
    scale = 1.0 / math.sqrt(HEAD_DIM)

    q = q_ref[...]
    lane = lax.broadcasted_iota(jnp.int32, q.shape, 1)
    zero = jnp.zeros_like(q)
    q_halves = (jnp.where(lane < HEAD_DIM, q, zero), jnp.where(lane >= HEAD_DIM, q, zero))
    qpos = qi * tq + lax.broadcasted_iota(jnp.int32, (1, tq), 1)

    def body(kb, carry):
        start = pl.multiple_of(kb * tk, tk)
        k = k_ref[pl.ds(start, tk), :]
        vt = vt_ref[:, pl.ds(start, tk)]
        kpos = start + lax.broadcasted_iota(jnp.int32, (tk, 1), 0)
        bias = slope * jnp.abs(qpos - kpos).astype(F32)
        new = []
        for half in range(2):
            m_old, l_old, acc_old = carry[half]
            s = _dot_nt(k, q_halves[half]) * scale - bias
            m_new = jnp.maximum(m_old, jnp.max(s, axis=0, keepdims=True))
            p = jnp.exp(s - m_new)
            alpha = jnp.exp(m_old - m_new)
            l_new = alpha * l_old + jnp.sum(p, axis=0, keepdims=True)
            acc_new = alpha * acc_old + _dot(vt, p.astype(BF16))
            new.append((m_new, l_new, acc_new))
        return tuple(new)

    init = tuple(
        (jnp.full((1, tq), -jnp.inf, F32), jnp.zeros((1, tq), F32), jnp.zeros((V_DIM, tq), F32))
        for _ in range(2))
    (_, l0, acc0), (_, l1, acc1) = lax.fori_loop(0, nk, body, init)

    lam_init = lam_ref[0]
    one_minus = lam_ref[1]
    lam = (jnp.exp(jnp.sum(lq1_ref[...] * lk1_ref[...], axis=-1, keepdims=True))
           - jnp.exp(jnp.sum(lq2_ref[...] * lk2_ref[...], axis=-1, keepdims=True)) + lam_init)
    o = acc0 / l0 - lam * (acc1 / l1)
    ms = jnp.mean(o * o, axis=0, keepdims=True)
    y = o * lax.rsqrt(ms + EPS) * subln_ref[...] * one_minus
    o_ref[...] = y.T.astype(o_ref.dtype)


def diff_attention(qk, vt, slopes, lam_consts, lq1, lk1, lq2, lk2, subln_col, b, s, tq, tk):
    t = b * s
    nq = s // tq
    nk = s // tk
    hh = ATTN_HEADS
    smem = pl.BlockSpec(memory_space=pltpu.SMEM)
    vec = pl.BlockSpec((1, HEAD_DIM), lambda bi, h, qi: (0, 0))
    return pl.pallas_call(
        functools.partial(_attn_kernel, tq=tq, tk=tk, nk=nk),
        grid=(b, hh, nq),
        in_specs=[
            smem, smem,
            pl.BlockSpec((tq, V_DIM), lambda bi, h, qi: (bi * nq + qi, h)),
            pl.BlockSpec((s, V_DIM), lambda bi, h, qi: (bi, hh + h)),
            pl.BlockSpec((V_DIM, s), lambda bi, h, qi: (h, bi)),
            vec, vec, vec, vec,
            pl.BlockSpec((V_DIM, 1), lambda bi, h, qi: (0, 0)),
        ],
        out_specs=pl.BlockSpec((tq, V_DIM), lambda bi, h, qi: (bi * nq + qi, h)),
        out_shape=jax.ShapeDtypeStruct((t, hh * V_DIM), BF16),
        compiler_params=_params(("parallel", "parallel", "arbitrary")),
        name="diff_attention",
    )(slopes, lam_consts, qk, qk, vt, lq1, lk1, lq2, lk2, subln_col)


def _ssd_kernel(*refs, rev, final, nc):
    if final:
        (xbc_ref, prev_ref, next_ref, dt_ref, dtt_ref, cw_ref, cb_ref, bias_f_ref, bias_c_ref,
         alog_f_ref, alog_c_ref, z_ref, yf_ref, dskip_ref, nw_ref, y_ref, st_ref) = refs
    else:
        (xbc_ref, prev_ref, next_ref, dt_ref, dtt_ref, cw_ref, cb_ref, bias_f_ref, bias_c_ref,
         alog_f_ref, alog_c_ref, y_ref, st_ref) = refs
    direction = 1 if rev else 0
    c = pl.program_id(1)
    cc = (nc - 1 - c) if rev else c
    ll = CHUNK

    @pl.when(c == 0)
    def _():
        st_ref[...] = jnp.zeros_like(st_ref)

    x = xbc_ref[...]
    prow = jnp.where(cc == 0, 0.0, prev_ref[7:8, :])
    nrow = jnp.where(cc == nc - 1, 0.0, next_ref[0:1, :])
    row = lax.broadcasted_iota(jnp.int32, (ll, 1), 0)
    xp = jnp.where(row == 0, prow, pltpu.roll(x, 1, axis=0))
    xn = jnp.where(row == ll - 1, nrow, pltpu.roll(x, ll - 1, axis=0))
    cw = cw_ref[...]
    u = _silu(cb_ref[...] + xp * cw[0:1, :] + x * cw[1:2, :] + xn * cw[2:3, :])
    xs = u[:, :D_INNER]
    bm = u[:, D_INNER:D_INNER + SSD_GROUPS * D_STATE]
    cm = u[:, D_INNER + SSD_GROUPS * D_STATE:]

    jj = lax.broadcasted_iota(jnp.int32, (DT_PAD, D_INNER), 0)
    col = lax.broadcasted_iota(jnp.int32, (DT_PAD, D_INNER), 1)
    head_of_col = lax.shift_right_logical(col, int(math.log2(SSD_HEAD_DIM)))
    expand = jnp.where(jj == direction * SSD_HEADS + head_of_col, 1.0, 0.0).astype(BF16)
    dt_full = _softplus(_dot_f32_by_01(dt_ref[...], expand) + bias_f_ref[...])
    a_full = dt_full * (-jnp.exp(alog_f_ref[...]))
    dt_rows = _softplus(dtt_ref[direction * SSD_HEADS:(direction + 1) * SSD_HEADS, :] + bias_c_ref[...])
    a_rows = dt_rows * (-jnp.exp(alog_c_ref[...]))

    ri = lax.broadcasted_iota(jnp.int32, (ll, ll), 0)
    ci = lax.broadcasted_iota(jnp.int32, (ll, ll), 1)
    if rev:
        keep = ci >= ri
        edge = 0
    else:
        keep = ci <= ri
        edge = ll - 1
    tri = jnp.where(keep, 1.0, 0.0).astype(BF16)
    tri_t = jnp.where((ri >= ci) if rev else (ri <= ci), 1.0, 0.0).astype(BF16)
    cum_full = _dot_01_by_f32(tri, a_full)
    cum_rows = _dot_f32_by_01(a_rows, tri_t)

    xd = xs * dt_full
    xd_b = xd.astype(BF16)
    cum_edge = cum_full[edge:edge + 1, :]
    xdw = (xd * jnp.exp(cum_edge - cum_full)).astype(BF16)
    grow = jnp.exp(cum_full)
    lane = lax.broadcasted_iota(jnp.int32, (ll, LANES), 1)

    y_parts = []
    for g in range(SSD_GROUPS):
        bg = bm[:, g * D_STATE:(g + 1) * D_STATE]
        cg = cm[:, g * D_STATE:(g + 1) * D_STATE].astype(BF16)
        cb = _dot_nt(cg, bg.astype(BF16))
        gs = slice(g * GROUP_COLS, (g + 1) * GROUP_COLS)
        st_in = st_ref[:, gs]
        y_off = _dot(cg, st_in.astype(BF16)) * grow[:, gs]
        heads_per_group = SSD_HEADS // SSD_GROUPS
        for pair in range(heads_per_group // 2):
            lo = g * GROUP_COLS + pair * LANES
            xd_pair = xd_b[:, lo:lo + LANES]
            outs = []
            for sub in range(2):
                hd = g * heads_per_group + pair * 2 + sub
                seg = cum_full[:, hd * SSD_HEAD_DIM:hd * SSD_HEAD_DIM + 1] - cum_rows[hd:hd + 1, :]
                dec = jnp.exp(jnp.where(keep, seg, -jnp.inf))
                outs.append(_dot((cb * dec).astype(BF16), xd_pair))
            y_diag = jnp.where(lane < SSD_HEAD_DIM, outs[0], outs[1])
            y_parts.append(y_diag + y_off[:, pair * LANES:(pair + 1) * LANES])
        st_chunk = _dot(bg.T.astype(BF16), xdw[:, gs])
        st_ref[:, gs] = st_in * jnp.exp(cum_edge[:, gs]) + st_chunk
    y = jnp.concatenate(y_parts, axis=1)

    if final:
        y = yf_ref[...] + y + xs * dskip_ref[...]
        y = y * _silu(z_ref[...])
        nw = nw_ref[...]
        normed = []
        for g in range(SSD_GROUPS):
            gs = slice(g * GROUP_COLS, (g + 1) * GROUP_COLS)
            normed.append(_rms(y[:, gs], nw[:, gs]))
        y = jnp.concatenate(normed, axis=1)
    y_ref[...] = y.astype(y_ref.dtype)


def ssd_pass(xbc, dt, dtt, cw, cb, bias_f, bias_c, alog_f, alog_c, extras, b, s, rev):
    final = extras is not None
    t = b * s
    nc = s // CHUNK
    rows8 = CHUNK // 8
    last8 = t // 8 - 1

    def cidx(bi, c):
        return bi * nc + ((nc - 1 - c) if rev else c)

    def full(shape):
        return pl.BlockSpec(shape, lambda bi, c: (0, 0))

    chunk_rows = lambda w: pl.BlockSpec((CHUNK, w), lambda bi, c: (cidx(bi, c), 0))
    in_specs = [
        chunk_rows(CONV_DIM),
        pl.BlockSpec((8, CONV_DIM), lambda bi, c: (jnp.maximum(cidx(bi, c) * rows8 - 1, 0), 0)),
        pl.BlockSpec((8, CONV_DIM), lambda bi, c: (jnp.minimum((cidx(bi, c) + 1) * rows8, last8), 0)),
        chunk_rows(DT_PAD),
        pl.BlockSpec((2 * SSD_HEADS, CHUNK), lambda bi, c: (0, cidx(bi, c))),
        full((3, CONV_DIM)), full((1, CONV_DIM)),
        full((1, D_INNER)), full((SSD_HEADS, 1)), full((1, D_INNER)), full((SSD_HEADS, 1)),
    ]
    args = [xbc, xbc, xbc, dt, dtt, cw, cb, bias_f, bias_c, alog_f, alog_c]
    if final:
        z, yf, dskip_f, nw = extras
        in_specs += [chunk_rows(D_INNER), chunk_rows(D_INNER), full((1, D_INNER)), full((1, D_INNER))]
        args += [z, yf, dskip_f, nw]
    return pl.pallas_call(
        functools.partial(_ssd_kernel, rev=rev, final=final, nc=nc),
        grid=(b, nc),
        in_specs=in_specs,
        out_specs=chunk_rows(D_INNER),
        out_shape=jax.ShapeDtypeStruct((t, D_INNER), BF16 if final else F32),
        scratch_shapes=[pltpu.VMEM((D_STATE, D_INNER), F32)],
        compiler_params=_params(("parallel", "arbitrary")),
        name="ssd_bwd_final" if final else "ssd_fwd",
    )(*args)


def _merge_kernel(attn_ref, ssd_ref, gates_ref, x_ref, woa_ref, wos_ref, wout_ref, nw_ref, o_ref):
    a = _dot(attn_ref[...], woa_ref[...])
    s = _dot(ssd_ref[...], wos_ref[...])
    gates = gates_ref[...]
    merged = jax.nn.sigmoid(gates[:, :D_MODEL]) * a + jax.nn.sigmoid(gates[:, D_MODEL:]) * s
    mo = _dot(merged.astype(BF16), wout_ref[...])
    o_ref[...] = x_ref[...] + _rms(mo, nw_ref[...])


def merge_out(attn, ssd, gates, x, woa, wos, wout, nw, tm):
    t, d = x.shape
    rows = lambda w: pl.BlockSpec((tm, w), lambda i: (i, 0))
    full = lambda shape: pl.BlockSpec(shape, lambda i: (0, 0))
    return pl.pallas_call(
        _merge_kernel,
        grid=(t // tm,),
        in_specs=[rows(d), rows(d), rows(2 * d), rows(d), full((d, d)), full((d, d)), full((d, d)),
                  full((1, d))],
        out_specs=rows(d),
        out_shape=jax.ShapeDtypeStruct((t, d), F32),
        compiler_params=_params(("parallel",)),
        name="merge_out",
    )(attn, ssd, gates, x, woa, wos, wout, nw)


def _ffn_kernel(x_ref, xp_ref, xn_ref, gpre_ref, wa_ref, wg_ref, cwa_ref, cwg_ref, cba_ref, cbg_ref,
                wd_ref, gpost_ref, o_ref, h_ref, acc_ref, *, tm, tiles_per_seq, nf):
    i = pl.program_id(0)
    f = pl.program_id(1)

    @pl.when(f == 0)
    def _():
        gpre = gpre_ref[...]
        pos = i % tiles_per_seq
        h_ref[0:tm, :] = _rms(x_ref[...], gpre).astype(BF16)
        hp = jnp.where(pos == 0, 0.0, _rms(xp_ref[...], gpre))
        hn = jnp.where(pos == tiles_per_seq - 1, 0.0, _rms(xn_ref[...], gpre))
        h_ref[tm:tm + HALO, :] = hp.astype(BF16)
        h_ref[tm + HALO:tm + 2 * HALO, :] = hn.astype(BF16)
        acc_ref[...] = jnp.zeros_like(acc_ref)

    h = h_ref[...]
    row = lax.broadcasted_iota(jnp.int32, (tm, 1), 0)

    def conv_branch(w_ref, cw_ref, cb_ref):
        u = _dot(h, w_ref[...])
        um = u[0:tm, :]
        prow = u[tm + HALO - 1:tm + HALO, :]
        nrow = u[tm + HALO:tm + HALO + 1, :]
        up = jnp.where(row == 0, prow, pltpu.roll(um, 1, axis=0))
        un = jnp.where(row == tm - 1, nrow, pltpu.roll(um, tm - 1, axis=0))
        cw = cw_ref[...]
        return cb_ref[...] + up * cw[0:1, :] + um * cw[1:2, :] + un * cw[2:3, :]

    a = conv_branch(wa_ref, cwa_ref, cba_ref)
    g = conv_branch(wg_ref, cwg_ref, cbg_ref)
    act = (_silu(g) * a).astype(BF16)
    acc_ref[...] += _dot(act, wd_ref[...])

    @pl.when(f == nf - 1)
    def _():
        o_ref[...] = x_ref[...] + _rms(acc_ref[...], gpost_ref[...])


def ffn(x, gpre, wa, wg, cwa, cwg, cba, cbg, wd, gpost, s, tm, tf):
    t, d = x.shape
    nf = D_FF_PAD // tf
    tiles_per_seq = s // tm
    blocks = tm // HALO
    last = t // HALO - 1
    full = lambda shape: pl.BlockSpec(shape, lambda i, f: (0, 0))
    colblk = lambda r: pl.BlockSpec((r, tf), lambda i, f: (0, f))
    return pl.pallas_call(
        functools.partial(_ffn_kernel, tm=tm, tiles_per_seq=tiles_per_seq, nf=nf),
        grid=(t // tm, nf),
        in_specs=[
            pl.BlockSpec((tm, d), lambda i, f: (i, 0)),
            pl.BlockSpec((HALO, d), lambda i, f: (jnp.maximum(i * blocks - 1, 0), 0)),
            pl.BlockSpec((HALO, d), lambda i, f: (jnp.minimum((i + 1) * blocks, last), 0)),
            full((1, d)),
            colblk(d), colblk(d), colblk(3), colblk(3), colblk(1), colblk(1),
            pl.BlockSpec((tf, d), lambda i, f: (f, 0)),
            full((1, d)),
        ],
        out_specs=pl.BlockSpec((tm, d), lambda i, f: (i, 0)),
        out_shape=jax.ShapeDtypeStruct((t, d), F32),
        scratch_shapes=[pltpu.VMEM((tm + 2 * HALO, d), BF16), pltpu.VMEM((tm, d), F32)],
        compiler_params=_params(("parallel", "arbitrary")),
        name="ffn",
    )(x, x, x, gpre, wa, wg, cwa, cwg, cba, cbg, wd, gpost)


def _tiles(s):
    return dict(
        tm_proj=min(1024, s), tn_proj=512,
        tq=min(256, s), tk=min(512, s),
        tm_merge=min(256, s),
        tm_ffn=min(1024, s), tf=256,
    )


def _layer(x, b, s, w, cfg):
    g_pre = w["norm_mix_pre"]
    tm, tn = cfg["tm_proj"], cfg["tn_proj"]
    qk = norm_matmul(x, g_pre, w["w_qk"], BF16, tm, tn)
    vt = norm_matmul_t(x, g_pre, w["w_vt"], BF16, tm, tn)
    z = norm_matmul(x, g_pre, w["w_z"], F32, tm, tn)
    xbc = norm_matmul(x, g_pre, w["w_xbc"], F32, tm, tn)
    gates = norm_matmul(x, g_pre, w["w_gates"], F32, tm, tn)
    dt, dtt = dt_proj(x, g_pre, w["w_dt"], w["w_dtt"], tm)

    attn = diff_attention(qk, vt, w["slopes"], w["lam_consts"], w["lam_q1"], w["lam_k1"], w["lam_q2"],
                          w["lam_k2"], w["subln_col"], b, s, cfg["tq"], cfg["tk"])

    yf = ssd_pass(xbc, dt, dtt, w["conv_ssd_w"], w["conv_ssd_b"], w["bias_f"][0:1], w["bias_c"][0],
                  w["alog_f"][0:1], w["alog_c"][0], None, b, s, rev=False)
    ssd = ssd_pass(xbc, dt, dtt, w["conv_ssd_w"], w["conv_ssd_b"], w["bias_f"][1:2], w["bias_c"][1],
                   w["alog_f"][1:2], w["alog_c"][1], (z, yf, w["dskip_f"], w["ssd_norm"]), b, s, rev=True)

    x = merge_out(attn, ssd, gates, x, w["w_o_attn"], w["w_o_ssd"], w["w_out"], w["norm_mix_post"],
                  cfg["tm_merge"])
    x = ffn(x, w["norm_ffn_pre"], w["w_up_a"], w["w_up_g"], w["cw_a"], w["cw_g"], w["cb_a"], w["cb_g"],
            w["w_down"], w["norm_ffn_post"], s, cfg["tm_ffn"], cfg["tf"])
    return x


def _prepare_weights(norm_mix_pre, norm_mix_post, norm_ffn_pre, norm_ffn_post, w_in, lam_q1, lam_k1,
                     lam_q2, lam_k2, attn_subln, conv_ssd_w, conv_ssd_b, dt_bias, a_log, d_skip, ssd_norm,
                     w_o_attn, w_o_ssd, w_out, w_up, conv_ffn_w, conv_ffn_b, w_down):
    depth = w_in.shape[0]
    qk_cols = ATTN_HEADS * 2 * HEAD_DIM
    attn_w = ATTN_HEADS * V_DIM
    cuts = [0, 2 * qk_cols]
    for width in (attn_w, D_INNER, CONV_DIM, 2 * SSD_HEADS, 2 * D_MODEL):
        cuts.append(cuts[-1] + width)
    seg = lambda i: w_in[:, :, cuts[i]:cuts[i + 1]]
    w_dt = seg(4)
    row = lambda a: a[:, None, :]
    rep = lambda a: jnp.repeat(a, SSD_HEAD_DIM, axis=-1)
    pad_ff = lambda a: jnp.pad(a, [(0, 0)] * (a.ndim - 1) + [(0, D_FF_PAD - D_FF)])
    lam_init = [0.8 - 0.6 * math.exp(-0.3 * l) for l in range(depth)]
    return dict(
        norm_mix_pre=row(norm_mix_pre), norm_mix_post=row(norm_mix_post),
        norm_ffn_pre=row(norm_ffn_pre), norm_ffn_post=row(norm_ffn_post),
        w_qk=seg(0).astype(BF16),
        w_vt=jnp.swapaxes(seg(1), 1, 2).astype(BF16),
        w_z=seg(2).astype(BF16),
        w_xbc=seg(3).astype(BF16),
        w_dt=jnp.pad(w_dt, ((0, 0), (0, 0), (0, DT_PAD - 2 * SSD_HEADS))).astype(BF16),
        w_dtt=jnp.swapaxes(w_dt, 1, 2).astype(BF16),
        w_gates=seg(5).astype(BF16),
        slopes=jnp.tile(jnp.asarray([2.0 ** (-8.0 * (i + 1) / ATTN_HEADS) for i in range(ATTN_HEADS)],
                                    F32)[None], (depth, 1)),
        lam_consts=jnp.asarray([[li, 1.0 - li] for li in lam_init], F32),
        lam_q1=row(lam_q1), lam_k1=row(lam_k1), lam_q2=row(lam_q2), lam_k2=row(lam_k2),
        subln_col=attn_subln[:, :, None],
        conv_ssd_w=conv_ssd_w, conv_ssd_b=row(conv_ssd_b),
        bias_f=rep(dt_bias), bias_c=dt_bias[..., None],
        alog_f=rep(a_log), alog_c=a_log[..., None],
        dskip_f=row(rep(d_skip)), ssd_norm=row(ssd_norm),
        w_o_attn=w_o_attn.astype(BF16), w_o_ssd=w_o_ssd.astype(BF16), w_out=w_out.astype(BF16),
        w_up_a=pad_ff(w_up[:, :, :D_FF]).astype(BF16), w_up_g=pad_ff(w_up[:, :, D_FF:]).astype(BF16),
        cw_a=pad_ff(conv_ffn_w[:, :, :D_FF]), cw_g=pad_ff(conv_ffn_w[:, :, D_FF:]),
        cb_a=row(pad_ff(conv_ffn_b[:, :D_FF])), cb_g=row(pad_ff(conv_ffn_b[:, D_FF:])),
        w_down=jnp.pad(w_down, ((0, 0), (0, D_FF_PAD - D_FF), (0, 0))).astype(BF16),
    )


def kernel(x_prompt, x_sample, norm_mix_pre, norm_mix_post, norm_ffn_pre, norm_ffn_post, w_in, lam_q1, lam_k1, lam_q2, lam_k2, attn_subln, conv_ssd_w, conv_ssd_b, dt_bias, a_log, d_skip, ssd_norm, w_o_attn, w_o_ssd, w_out, w_up, conv_ffn_w, conv_ffn_b, w_down):
    weights = _prepare_weights(norm_mix_pre, norm_mix_post, norm_ffn_pre, norm_ffn_post, w_in, lam_q1,
                               lam_k1, lam_q2, lam_k2, attn_subln, conv_ssd_w, conv_ssd_b, dt_bias, a_log,
                               d_skip, ssd_norm, w_o_attn, w_o_ssd, w_out, w_up, conv_ffn_w, conv_ffn_b,
                               w_down)
    groups = []
    for xg in (x_prompt, x_sample):
        b, s, d = xg.shape
        groups.append((b, s, _tiles(s)))

    def step(carry, w):
        out = tuple(_layer(x, b, s, w, cfg) for x, (b, s, cfg) in zip(carry, groups))
        return out, None

    init = tuple(xg.reshape(-1, xg.shape[-1]) for xg in (x_prompt, x_sample))
    out, _ = lax.scan(step, init, weights)
    return tuple(o.reshape(xg.shape) for o, xg in zip(out, (x_prompt, x_sample)))
```

```python
import functools
import math

import jax
import jax.numpy as jnp
from jax import lax
from jax.experimental import pallas as pl
from jax.experimental.pallas import tpu as pltpu

F32 = jnp.float32
BF16 = jnp.bfloat16

D_MODEL = 1024
ATTN_HEADS = 8
HEAD_DIM = 64
V_DIM = 2 * HEAD_DIM
SSD_HEADS = 16
SSD_HEAD_DIM = 64
D_INNER = SSD_HEADS * SSD_HEAD_DIM
SSD_GROUPS = 2
GROUP_COLS = D_INNER // SSD_GROUPS
D_STATE = 128
CHUNK = 128
CONV_DIM = D_INNER + 2 * SSD_GROUPS * D_STATE
D_FF = 2752
EPS = 1e-6

LANES = 128
D_FF_PAD = 2816
DT_PAD = LANES
HALO = 16
VMEM_LIMIT = 56 * 1024 * 1024


def _params(sem):
    return pltpu.CompilerParams(dimension_semantics=sem, vmem_limit_bytes=VMEM_LIMIT)


def _rms(x, g):
    ms = jnp.mean(x * x, axis=-1, keepdims=True)
    return x * lax.rsqrt(ms + EPS) * g


def _dot(a, b):
    return jnp.dot(a, b, preferred_element_type=F32)


def _dot_nt(a, b):
    return lax.dot_general(a, b, (((1,), (1,)), ((), ())), preferred_element_type=F32)


def _dot_f32_by_01(x, m01):
    hi = x.astype(BF16)
    r1 = x - hi.astype(F32)
    mid = r1.astype(BF16)
    lo = (r1 - mid.astype(F32)).astype(BF16)
    return _dot(hi, m01) + _dot(mid, m01) + _dot(lo, m01)


def _dot_01_by_f32(m01, x):
    hi = x.astype(BF16)
    r1 = x - hi.astype(F32)
    mid = r1.astype(BF16)
    lo = (r1 - mid.astype(F32)).astype(BF16)
    return _dot(m01, hi) + _dot(m01, mid) + _dot(m01, lo)


def _softplus(x):
    return jnp.maximum(x, 0.0) + jnp.log1p(jnp.exp(-jnp.abs(x)))


def _silu(x):
    return x * jax.nn.sigmoid(x)


def _norm_matmul_kernel(x_ref, g_ref, w_ref, o_ref, h_ref):
    @pl.when(pl.program_id(1) == 0)
    def _():
        h_ref[...] = _rms(x_ref[...], g_ref[...]).astype(BF16)

    o_ref[...] = _dot(h_ref[...], w_ref[...]).astype(o_ref.dtype)


def norm_matmul(x, g, w, out_dtype, tm, tn):
    t, d = x.shape
    n = w.shape[1]
    return pl.pallas_call(
        _norm_matmul_kernel,
        grid=(t // tm, n // tn),
        in_specs=[
            pl.BlockSpec((tm, d), lambda i, j: (i, 0)),
            pl.BlockSpec((1, d), lambda i, j: (0, 0)),
            pl.BlockSpec((d, tn), lambda i, j: (0, j)),
        ],
        out_specs=pl.BlockSpec((tm, tn), lambda i, j: (i, j)),
        out_shape=jax.ShapeDtypeStruct((t, n), out_dtype),
        scratch_shapes=[pltpu.VMEM((tm, d), BF16)],
        compiler_params=_params(("parallel", "arbitrary")),
        name="norm_matmul",
    )(x, g, w)


def _norm_matmul_t_kernel(x_ref, g_ref, wt_ref, o_ref, h_ref):
    @pl.when(pl.program_id(1) == 0)
    def _():
        h_ref[...] = _rms(x_ref[...], g_ref[...]).astype(BF16)

    o_ref[...] = _dot_nt(wt_ref[...], h_ref[...]).astype(o_ref.dtype)


def norm_matmul_t(x, g, wt, out_dtype, tm, tn):
    t, d = x.shape
    n = wt.shape[0]
    return pl.pallas_call(
        _norm_matmul_t_kernel,
        grid=(t // tm, n // tn),
        in_specs=[
            pl.BlockSpec((tm, d), lambda i, j: (i, 0)),
            pl.BlockSpec((1, d), lambda i, j: (0, 0)),
            pl.BlockSpec((tn, d), lambda i, j: (j, 0)),
        ],
        out_specs=pl.BlockSpec((tn, tm), lambda i, j: (j, i)),
        out_shape=jax.ShapeDtypeStruct((n, t), out_dtype),
        scratch_shapes=[pltpu.VMEM((tm, d), BF16)],
        compiler_params=_params(("parallel", "arbitrary")),
        name="norm_matmul_t",
    )(x, g, wt)


def _dt_proj_kernel(x_ref, g_ref, w_ref, wt_ref, dt_ref, dtt_ref):
    h = _rms(x_ref[...], g_ref[...]).astype(BF16)
    dt_ref[...] = _dot(h, w_ref[...])
    dtt_ref[...] = _dot_nt(wt_ref[...], h)


def dt_proj(x, g, w, wt, tm):
    t, d = x.shape
    r = wt.shape[0]
    return pl.pallas_call(
        _dt_proj_kernel,
        grid=(t // tm,),
        in_specs=[
            pl.BlockSpec((tm, d), lambda i: (i, 0)),
            pl.BlockSpec((1, d), lambda i: (0, 0)),
            pl.BlockSpec((d, DT_PAD), lambda i: (0, 0)),
            pl.BlockSpec((r, d), lambda i: (0, 0)),
        ],
        out_specs=[
            pl.BlockSpec((tm, DT_PAD), lambda i: (i, 0)),
            pl.BlockSpec((r, tm), lambda i: (0, i)),
        ],
        out_shape=[
            jax.ShapeDtypeStruct((t, DT_PAD), F32),
            jax.ShapeDtypeStruct((r, t), F32),
        ],
        compiler_params=_params(("parallel",)),
        name="dt_proj",
    )(x, g, w, wt)


def _attn_kernel(slopes_ref, lam_ref, q_ref, k_ref, vt_ref, lq1_ref, lk1_ref, lq2_ref, lk2_ref,
                 subln_ref, o_ref, kfeat_ref, sa_ref, sb_ref, *, tq, tk, nk):
    h = pl.program_id(1)
    qi = pl.program_id(2)
    slope = slopes_ref[h]
    scale = 1.0 / math.sqrt(HEAD_DIM)
    q0pos = qi * tq
    n_off = nk - 1
    kd = lax.shift_right_logical(qi, int(math.log2(tk // tq)))

    q = q_ref[...] * jnp.asarray(scale, BF16)
    lane = lax.broadcasted_iota(jnp.int32, q.shape, 1)
    zero = jnp.zeros_like(q)
    ii = lax.broadcasted_iota(jnp.int32, (tq, LANES), 0).astype(F32)
    fq = lax.broadcasted_iota(jnp.int32, (tq, LANES), 1)
    q_feat = jnp.where(fq == 0, -slope * ii, jnp.where(fq <= 2, 1.0, 0.0)).astype(BF16)
    q_ops = jnp.concatenate([
        jnp.concatenate([jnp.where(lane < HEAD_DIM, q, zero), q_feat], axis=1),
        jnp.concatenate([jnp.where(lane >= HEAD_DIM, q, zero), q_feat], axis=1)], axis=0)

    jj = lax.broadcasted_iota(jnp.int32, (tk, LANES), 0)
    fk = lax.broadcasted_iota(jnp.int32, (tk, LANES), 1)
    jj_lo = jnp.bitwise_and(jj, 255)
    jj_hi = (jj - jj_lo).astype(F32)
    jj_lo = jj_lo.astype(F32)
    k_feat = jnp.where(fk == 0, 1.0, jnp.where(fk == 1, slope * jj_lo, jnp.where(fk == 2, slope * jj_hi, 0.0)))
    kfeat_ref[0] = k_feat.astype(BF16)
    kfeat_ref[1] = (-k_feat).astype(BF16)

    def raw_scores(kb, side):
        start = pl.multiple_of(kb * tk, tk)
        k_ops = jnp.concatenate([k_ref[pl.ds(start, tk), :], kfeat_ref[side]], axis=1)
        return _dot_nt(k_ops, q_ops)

    def offset(kb):
        return slope * jnp.abs(q0pos - kb * tk).astype(F32)

    def update(s, mx, c, kb, state):
        m_old, l_old, acc_old = state
        m_new = jnp.maximum(m_old, mx - c)
        p = jnp.exp(s - (m_new + c))
        alpha = jnp.exp(m_old - m_new)
        l_new = alpha * l_old + jnp.sum(p, axis=0, keepdims=True)
        vt = vt_ref[:, pl.ds(pl.multiple_of(kb * tk, tk), tk)]
        acc_new = alpha * acc_old + _dot(vt, p.astype(BF16))
        return m_new, l_new, acc_new

    def off_tile(t):
        side = (t >= kd).astype(jnp.int32)
        return t + side, side

    state = (jnp.full((1, 2 * tq), -jnp.inf, F32), jnp.zeros((1, 2 * tq), F32),
             jnp.zeros((V_DIM, 2 * tq), F32))

    def produce(t, dst_ref):
        kb, side = off_tile(t)
        s = raw_scores(kb, side)
        dst_ref[...] = s
        return jnp.max(s, axis=0, keepdims=True)

    def consume(t, src_ref, mx, st):
        kb, _ = off_tile(t)
        return update(src_ref[...], mx, offset(kb), kb, st)

    if n_off > 0:
        mx_a = produce(jnp.int32(0), sa_ref)

    c_d = offset(kd)
    s_d = jnp.minimum(raw_scores(kd, 0) - c_d, raw_scores(kd, 1) + c_d)
    state = update(s_d, jnp.max(s_d, axis=0, keepdims=True), 0.0, kd, state)

    if n_off > 0:
        def body(u, carry):
            st, mx_even = carry[:3], carry[3]
            mx_odd = produce(2 * u + 1, sb_ref)
            st = consume(2 * u, sa_ref, mx_even, st)
            mx_even = produce(2 * u + 2, sa_ref)
            st = consume(2 * u + 1, sb_ref, mx_odd, st)
            return st + (mx_even,)

        carry = lax.fori_loop(0, (n_off - 1) // 2, body, state + (mx_a,))
        m_fin, l_fin, acc_fin = consume(jnp.int32(n_off - 1), sa_ref, carry[3], carry[:3])
    else:
        m_fin, l_fin, acc_fin = state

    lam_init = lam_ref[0]
    one_minus = lam_ref[1]
    lam = (jnp.exp(jnp.sum(lq1_ref[...] * lk1_ref[...], axis=-1, keepdims=True))
           - jnp.exp(jnp.sum(lq2_ref[...] * lk2_ref[...], axis=-1, keepdims=True)) + lam_init)
    o_both = acc_fin / l_fin
    o = o_both[:, :tq] - lam * o_both[:, tq:]
    ms = jnp.mean(o * o, axis=0, keepdims=True)
    y = o * lax.rsqrt(ms + EPS) * subln_ref[...] * one_minus
    o_ref[...] = y.T.astype(o_ref.dtype)


def diff_attention(qk, vt, slopes, lam_consts, lq1, lk1, lq2, lk2, subln_col, b, s, tq, tk):
    t = b * s
    nq = s // tq
    nk = s // tk
    hh = ATTN_HEADS
    smem = pl.BlockSpec(memory_space=pltpu.SMEM)
    vec = pl.BlockSpec((1, HEAD_DIM), lambda bi, h, qi: (0, 0))
    return pl.pallas_call(
        functools.partial(_attn_kernel, tq=tq, tk=tk, nk=nk),
        grid=(b, hh, nq),
        in_specs=[
            smem, smem,
            pl.BlockSpec((tq, V_DIM), lambda bi, h, qi: (bi * nq + qi, h)),
            pl.BlockSpec((s, V_DIM), lambda bi, h, qi: (bi, hh + h)),
            pl.BlockSpec((V_DIM, s), lambda bi, h, qi: (h, bi)),
            vec, vec, vec, vec,
            pl.BlockSpec((V_DIM, 1), lambda bi, h, qi: (0, 0)),
        ],
        out_specs=pl.BlockSpec((tq, V_DIM), lambda bi, h, qi: (bi * nq + qi, h)),
        out_shape=jax.ShapeDtypeStruct((t, hh * V_DIM), BF16),
        scratch_shapes=[pltpu.VMEM((2, tk, LANES), BF16), pltpu.VMEM((tk, 2 * tq), F32),
                        pltpu.VMEM((tk, 2 * tq), F32)],
        compiler_params=_params(("parallel", "parallel", "arbitrary")),
        name="diff_attention",
    )(slopes, lam_consts, qk, qk, vt, lq1, lk1, lq2, lk2, subln_col)


def _ssd_kernel(*refs, rev, final, nc):
    if final:
        (xbc_ref, prev_ref, next_ref, dt_ref, dtt_ref, cw_ref, cb_ref, bias_f_ref, bias_c_ref,
         alog_f_ref, alog_c_ref, z_ref, yf_ref, dskip_ref, nw_ref, y_ref, st_ref) = refs
    else:
        (xbc_ref, prev_ref, next_ref, dt_ref, dtt_ref, cw_ref, cb_ref, bias_f_ref, bias_c_ref,
         alog_f_ref, alog_c_ref, y_ref, st_ref) = refs
    direction = 1 if rev else 0
    c = pl.program_id(1)
    cc = (nc - 1 - c) if rev else c
    ll = CHUNK

    @pl.when(c == 0)
    def _():
        st_ref[...] = jnp.zeros_like(st_ref)

    x = xbc_ref[...]
    prow = jnp.where(cc == 0, 0.0, prev_ref[7:8, :])
    nrow = jnp.where(cc == nc - 1, 0.0, next_ref[0:1, :])
    row = lax.broadcasted_iota(jnp.int32, (ll, 1), 0)
    xp = jnp.where(row == 0, prow, pltpu.roll(x, 1, axis=0))
    xn = jnp.where(row == ll - 1, nrow, pltpu.roll(x, ll - 1, axis=0))
    cw = cw_ref[...]
    u = _silu(cb_ref[...] + xp * cw[0:1, :] + x * cw[1:2, :] + xn * cw[2:3, :])
    xs = u[:, :D_INNER]
    bm = u[:, D_INNER:D_INNER + SSD_GROUPS * D_STATE]
    cm = u[:, D_INNER + SSD_GROUPS * D_STATE:]

    jj = lax.broadcasted_iota(jnp.int32, (DT_PAD, D_INNER), 0)
    col = lax.broadcasted_iota(jnp.int32, (DT_PAD, D_INNER), 1)
    head_of_col = lax.shift_right_logical(col, int(math.log2(SSD_HEAD_DIM)))
    expand = jnp.where(jj == direction * SSD_HEADS + head_of_col, 1.0, 0.0).astype(BF16)
    dt_full = _softplus(_dot_f32_by_01(dt_ref[...], expand) + bias_f_ref[...])
    a_full = dt_full * (-jnp.exp(alog_f_ref[...]))
    dt_rows = _softplus(dtt_ref[direction * SSD_HEADS:(direction + 1) * SSD_HEADS, :] + bias_c_ref[...])
    a_rows = dt_rows * (-jnp.exp(alog_c_ref[...]))

    ri = lax.broadcasted_iota(jnp.int32, (ll, ll), 0)
    ci = lax.broadcasted_iota(jnp.int32, (ll, ll), 1)
    if rev:
        keep = ci >= ri
        edge = 0
    else:
        keep = ci <= ri
        edge = ll - 1
    tri = jnp.where(keep, 1.0, 0.0).astype(BF16)
    tri_t = jnp.where((ri >= ci) if rev else (ri <= ci), 1.0, 0.0).astype(BF16)
    cum_full = _dot_01_by_f32(tri, a_full)
    cum_rows = _dot_f32_by_01(a_rows, tri_t)

    xd = xs * dt_full
    xd_b = xd.astype(BF16)
    cum_edge = cum_full[edge:edge + 1, :]
    xdw = (xd * jnp.exp(cum_edge - cum_full)).astype(BF16)
    grow = jnp.exp(cum_full)
    lane = lax.broadcasted_iota(jnp.int32, (ll, LANES), 1)

    y_parts = []
    for g in range(SSD_GROUPS):
        bg = bm[:, g * D_STATE:(g + 1) * D_STATE]
        cg = cm[:, g * D_STATE:(g + 1) * D_STATE].astype(BF16)
        cb = _dot_nt(cg, bg.astype(BF16))
        gs = slice(g * GROUP_COLS, (g + 1) * GROUP_COLS)
        st_in = st_ref[:, gs]
        y_off = _dot(cg, st_in.astype(BF16)) * grow[:, gs]
        heads_per_group = SSD_HEADS // SSD_GROUPS
        for pair in range(heads_per_group // 2):
            lo = g * GROUP_COLS + pair * LANES
            xd_pair = xd_b[:, lo:lo + LANES]
            outs = []
            for sub in range(2):
                hd = g * heads_per_group + pair * 2 + sub
                seg = cum_full[:, hd * SSD_HEAD_DIM:hd * SSD_HEAD_DIM + 1] - cum_rows[hd:hd + 1, :]
                dec = jnp.exp(jnp.where(keep, seg, -jnp.inf))
                outs.append(_dot((cb * dec).astype(BF16), xd_pair))
            y_diag = jnp.where(lane < SSD_HEAD_DIM, outs[0], outs[1])
            y_parts.append(y_diag + y_off[:, pair * LANES:(pair + 1) * LANES])
        st_chunk = _dot(bg.T.astype(BF16), xdw[:, gs])
        st_ref[:, gs] = st_in * jnp.exp(cum_edge[:, gs]) + st_chunk
    y = jnp.concatenate(y_parts, axis=1)

    if final:
        y = yf_ref[...] + y + xs * dskip_ref[...]
        y = y * _silu(z_ref[...])
        nw = nw_ref[...]
        normed = []
        for g in range(SSD_GROUPS):
            gs = slice(g * GROUP_COLS, (g + 1) * GROUP_COLS)
            normed.append(_rms(y[:, gs], nw[:, gs]))
        y = jnp.concatenate(normed, axis=1)
    y_ref[...] = y.astype(y_ref.dtype)


def ssd_pass(xbc, dt, dtt, cw, cb, bias_f, bias_c, alog_f, alog_c, extras, b, s, rev):
    final = extras is not None
    t = b * s
    nc = s // CHUNK
    rows8 = CHUNK // 8
    last8 = t // 8 - 1

    def cidx(bi, c):
        return bi * nc + ((nc - 1 - c) if rev else c)

    def full(shape):
        return pl.BlockSpec(shape, lambda bi, c: (0, 0))

    chunk_rows = lambda w: pl.BlockSpec((CHUNK, w), lambda bi, c: (cidx(bi, c), 0))
    in_specs = [
        chunk_rows(CONV_DIM),
        pl.BlockSpec((8, CONV_DIM), lambda bi, c: (jnp.maximum(cidx(bi, c) * rows8 - 1, 0), 0)),
        pl.BlockSpec((8, CONV_DIM), lambda bi, c: (jnp.minimum((cidx(bi, c) + 1) * rows8, last8), 0)),
        chunk_rows(DT_PAD),
        pl.BlockSpec((2 * SSD_HEADS, CHUNK), lambda bi, c: (0, cidx(bi, c))),
        full((3, CONV_DIM)), full((1, CONV_DIM)),
        full((1, D_INNER)), full((SSD_HEADS, 1)), full((1, D_INNER)), full((SSD_HEADS, 1)),
    ]
    args = [xbc, xbc, xbc, dt, dtt, cw, cb, bias_f, bias_c, alog_f, alog_c]
    if final:
        z, yf, dskip_f, nw = extras
        in_specs += [chunk_rows(D_INNER), chunk_rows(D_INNER), full((1, D_INNER)), full((1, D_INNER))]
        args += [z, yf, dskip_f, nw]
    return pl.pallas_call(
        functools.partial(_ssd_kernel, rev=rev, final=final, nc=nc),
        grid=(b, nc),
        in_specs=in_specs,
        out_specs=chunk_rows(D_INNER),
        out_shape=jax.ShapeDtypeStruct((t, D_INNER), BF16 if final else F32),
        scratch_shapes=[pltpu.VMEM((D_STATE, D_INNER), F32)],
        compiler_params=_params(("parallel", "arbitrary")),
        name="ssd_bwd_final" if final else "ssd_fwd",
    )(*args)


def _merge_kernel(attn_ref, ssd_ref, gates_ref, x_ref, woa_ref, wos_ref, wout_ref, nw_ref, o_ref):
    a = _dot(attn_ref[...], woa_ref[...])
    s = _dot(ssd_ref[...], wos_ref[...])
    gates = gates_ref[...]
    merged = jax.nn.sigmoid(gates[:, :D_MODEL]) * a + jax.nn.sigmoid(gates[:, D_MODEL:]) * s
    mo = _dot(merged.astype(BF16), wout_ref[...])
    o_ref[...] = x_ref[...] + _rms(mo, nw_ref[...])


def merge_out(attn, ssd, gates, x, woa, wos, wout, nw, tm):
    t, d = x.shape
    rows = lambda w: pl.BlockSpec((tm, w), lambda i: (i, 0))
    full = lambda shape: pl.BlockSpec(shape, lambda i: (0, 0))
    return pl.pallas_call(
        _merge_kernel,
        grid=(t // tm,),
        in_specs=[rows(d), rows(d), rows(2 * d), rows(d), full((d, d)), full((d, d)), full((d, d)),
                  full((1, d))],
        out_specs=rows(d),
        out_shape=jax.ShapeDtypeStruct((t, d), F32),
        compiler_params=_params(("parallel",)),
        name="merge_out",
    )(attn, ssd, gates, x, woa, wos, wout, nw)


def _ffn_kernel(x_ref, xp_ref, xn_ref, gpre_ref, wa_ref, wg_ref, cwa_ref, cwg_ref, cba_ref, cbg_ref,
                wd_ref, gpost_ref, o_ref, h_ref, acc_ref, *, tm, tiles_per_seq, nf):
    i = pl.program_id(0)
    f = pl.program_id(1)

    @pl.when(f == 0)
    def _():
        gpre = gpre_ref[...]
        pos = i % tiles_per_seq
        h_ref[0:tm, :] = _rms(x_ref[...], gpre).astype(BF16)
        hp = jnp.where(pos == 0, 0.0, _rms(xp_ref[...], gpre))
        hn = jnp.where(pos == tiles_per_seq - 1, 0.0, _rms(xn_ref[...], gpre))
        h_ref[tm:tm + HALO, :] = hp.astype(BF16)
        h_ref[tm + HALO:tm + 2 * HALO, :] = hn.astype(BF16)
        acc_ref[...] = jnp.zeros_like(acc_ref)

    h = h_ref[...]
    row = lax.broadcasted_iota(jnp.int32, (tm, 1), 0)

    def conv_branch(w_ref, cw_ref, cb_ref):
        u = _dot(h, w_ref[...])
        um = u[0:tm, :]
        prow = u[tm + HALO - 1:tm + HALO, :]
        nrow = u[tm + HALO:tm + HALO + 1, :]
        up = jnp.where(row == 0, prow, pltpu.roll(um, 1, axis=0))
        un = jnp.where(row == tm - 1, nrow, pltpu.roll(um, tm - 1, axis=0))
        cw = cw_ref[...]
        return cb_ref[...] + up * cw[0:1, :] + um * cw[1:2, :] + un * cw[2:3, :]

    a = conv_branch(wa_ref, cwa_ref, cba_ref)
    g = conv_branch(wg_ref, cwg_ref, cbg_ref)
    act = (_silu(g) * a).astype(BF16)
    acc_ref[...] += _dot(act, wd_ref[...])

    @pl.when(f == nf - 1)
    def _():
        o_ref[...] = x_ref[...] + _rms(acc_ref[...], gpost_ref[...])


def ffn(x, gpre, wa, wg, cwa, cwg, cba, cbg, wd, gpost, s, tm, tf):
    t, d = x.shape
    nf = D_FF_PAD // tf
    tiles_per_seq = s // tm
    blocks = tm // HALO
    last = t // HALO - 1
    full = lambda shape: pl.BlockSpec(shape, lambda i, f: (0, 0))
    colblk = lambda r: pl.BlockSpec((r, tf), lambda i, f: (0, f))
    return pl.pallas_call(
        functools.partial(_ffn_kernel, tm=tm, tiles_per_seq=tiles_per_seq, nf=nf),
        grid=(t // tm, nf),
        in_specs=[
            pl.BlockSpec((tm, d), lambda i, f: (i, 0)),
            pl.BlockSpec((HALO, d), lambda i, f: (jnp.maximum(i * blocks - 1, 0), 0)),
            pl.BlockSpec((HALO, d), lambda i, f: (jnp.minimum((i + 1) * blocks, last), 0)),
            full((1, d)),
            colblk(d), colblk(d), colblk(3), colblk(3), colblk(1), colblk(1),
            pl.BlockSpec((tf, d), lambda i, f: (f, 0)),
            full((1, d)),
        ],
        out_specs=pl.BlockSpec((tm, d), lambda i, f: (i, 0)),
        out_shape=jax.ShapeDtypeStruct((t, d), F32),
        scratch_shapes=[pltpu.VMEM((tm + 2 * HALO, d), BF16), pltpu.VMEM((tm, d), F32)],
        compiler_params=_params(("parallel", "arbitrary")),
        name="ffn",
    )(x, x, x, gpre, wa, wg, cwa, cwg, cba, cbg, wd, gpost)


def _tiles(s):
    return dict(
        tm_proj=min(1024, s), tn_proj=512,
        tq=min(256, s), tk=min(512, s),
        tm_merge=min(256, s),
        tm_ffn=min(1024, s), tf=256,
    )


def _layer(x, b, s, w, cfg):
    g_pre = w["norm_mix_pre"]
    tm, tn = cfg["tm_proj"], cfg["tn_proj"]
    qk = norm_matmul(x, g_pre, w["w_qk"], BF16, tm, tn)
    vt = norm_matmul_t(x, g_pre, w["w_vt"], BF16, tm, tn)
    z = norm_matmul(x, g_pre, w["w_z"], F32, tm, tn)
    xbc = norm_matmul(x, g_pre, w["w_xbc"], F32, tm, tn)
    gates = norm_matmul(x, g_pre, w["w_gates"], F32, tm, tn)
    dt, dtt = dt_proj(x, g_pre, w["w_dt"], w["w_dtt"], tm)

    attn = diff_attention(qk, vt, w["slopes"], w["lam_consts"], w["lam_q1"], w["lam_k1"], w["lam_q2"],
                          w["lam_k2"], w["subln_col"], b, s, cfg["tq"], cfg["tk"])

    yf = ssd_pass(xbc, dt, dtt, w["conv_ssd_w"], w["conv_ssd_b"], w["bias_f"][0:1], w["bias_c"][0],
                  w["alog_f"][0:1], w["alog_c"][0], None, b, s, rev=False)
    ssd = ssd_pass(xbc, dt, dtt, w["conv_ssd_w"], w["conv_ssd_b"], w["bias_f"][1:2], w["bias_c"][1],
                   w["alog_f"][1:2], w["alog_c"][1], (z, yf, w["dskip_f"], w["ssd_norm"]), b, s, rev=True)

    x = merge_out(attn, ssd, gates, x, w["w_o_attn"], w["w_o_ssd"], w["w_out"], w["norm_mix_post"],
                  cfg["tm_merge"])
    x = ffn(x, w["norm_ffn_pre"], w["w_up_a"], w["w_up_g"], w["cw_a"], w["cw_g"], w["cb_a"], w["cb_g"],
            w["w_down"], w["norm_ffn_post"], s, cfg["tm_ffn"], cfg["tf"])
    return x


def _prepare_weights(norm_mix_pre, norm_mix_post, norm_ffn_pre, norm_ffn_post, w_in, lam_q1, lam_k1,
                     lam_q2, lam_k2, attn_subln, conv_ssd_w, conv_ssd_b, dt_bias, a_log, d_skip, ssd_norm,
                     w_o_attn, w_o_ssd, w_out, w_up, conv_ffn_w, conv_ffn_b, w_down):
    depth = w_in.shape[0]
    qk_cols = ATTN_HEADS * 2 * HEAD_DIM
    attn_w = ATTN_HEADS * V_DIM
    cuts = [0, 2 * qk_cols]
    for width in (attn_w, D_INNER, CONV_DIM, 2 * SSD_HEADS, 2 * D_MODEL):
        cuts.append(cuts[-1] + width)
    seg = lambda i: w_in[:, :, cuts[i]:cuts[i + 1]]
    w_dt = seg(4)
    row = lambda a: a[:, None, :]
    rep = lambda a: jnp.repeat(a, SSD_HEAD_DIM, axis=-1)
    pad_ff = lambda a: jnp.pad(a, [(0, 0)] * (a.ndim - 1) + [(0, D_FF_PAD - D_FF)])
    lam_init = [0.8 - 0.6 * math.exp(-0.3 * l) for l in range(depth)]
    return dict(
        norm_mix_pre=row(norm_mix_pre), norm_mix_post=row(norm_mix_post),
        norm_ffn_pre=row(norm_ffn_pre), norm_ffn_post=row(norm_ffn_post),
        w_qk=seg(0).astype(BF16),
        w_vt=jnp.swapaxes(seg(1), 1, 2).astype(BF16),
        w_z=seg(2).astype(BF16),
        w_xbc=seg(3).astype(BF16),
        w_dt=jnp.pad(w_dt, ((0, 0), (0, 0), (0, DT_PAD - 2 * SSD_HEADS))).astype(BF16),
        w_dtt=jnp.swapaxes(w_dt, 1, 2).astype(BF16),
        w_gates=seg(5).astype(BF16),
        slopes=jnp.tile(jnp.asarray([2.0 ** (-8.0 * (i + 1) / ATTN_HEADS) for i in range(ATTN_HEADS)],
                                    F32)[None], (depth, 1)),
        lam_consts=jnp.asarray([[li, 1.0 - li] for li in lam_init], F32),
        lam_q1=row(lam_q1), lam_k1=row(lam_k1), lam_q2=row(lam_q2), lam_k2=row(lam_k2),
        subln_col=attn_subln[:, :, None],
        conv_ssd_w=conv_ssd_w, conv_ssd_b=row(conv_ssd_b),
        bias_f=rep(dt_bias), bias_c=dt_bias[..., None],
        alog_f=rep(a_log), alog_c=a_log[..., None],
        dskip_f=row(rep(d_skip)), ssd_norm=row(ssd_norm),
        w_o_attn=w_o_attn.astype(BF16), w_o_ssd=w_o_ssd.astype(BF16), w_out=w_out.astype(BF16),
        w_up_a=pad_ff(w_up[:, :, :D_FF]).astype(BF16), w_up_g=pad_ff(w_up[:, :, D_FF:]).astype(BF16),
        cw_a=pad_ff(conv_ffn_w[:, :, :D_FF]), cw_g=pad_ff(conv_ffn_w[:, :, D_FF:]),
        cb_a=row(pad_ff(conv_ffn_b[:, :D_FF])), cb_g=row(pad_ff(conv_ffn_b[:, D_FF:])),
        w_down=jnp.pad(w_down, ((0, 0), (0, D_FF_PAD - D_FF), (0, 0))).astype(BF16),
    )


def kernel(x_prompt, x_sample, norm_mix_pre, norm_mix_post, norm_ffn_pre, norm_ffn_post, w_in, lam_q1, lam_k1, lam_q2, lam_k2, attn_subln, conv_ssd_w, conv_ssd_b, dt_bias, a_log, d_skip, ssd_norm, w_o_attn, w_o_ssd, w_out, w_up, conv_ffn_w, conv_ffn_b, w_down):
    weights = _prepare_weights(norm_mix_pre, norm_mix_post, norm_ffn_pre, norm_ffn_post, w_in, lam_q1,
                               lam_k1, lam_q2, lam_k2, attn_subln, conv_ssd_w, conv_ssd_b, dt_bias, a_log,
                               d_skip, ssd_norm, w_o_attn, w_o_ssd, w_out, w_up, conv_ffn_w, conv_ffn_b,
                               w_down)
    groups = []
    for xg in (x_prompt, x_sample):
        b, s, d = xg.shape
        groups.append((b, s, _tiles(s)))

    def step(carry, w):
        out = tuple(_layer(x, b, s, w, cfg) for x, (b, s, cfg) in zip(carry, groups))
        return out, None

    init = tuple(xg.reshape(-1, xg.shape[-1]) for xg in (x_prompt, x_sample))
    out, _ = lax.scan(step, init, weights)
    return tuple(o.reshape(xg.shape) for o, xg in zip(out, (x_prompt, x_sample)))
```

```python
import functools
import math

import jax
import jax.numpy as jnp
from jax import lax
from jax.experimental import pallas as pl
from jax.experimental.pallas import tpu as pltpu

F32 = jnp.float32
BF16 = jnp.bfloat16

D_MODEL = 1024
ATTN_HEADS = 8
HEAD_DIM = 64
V_DIM = 2 * HEAD_DIM
SSD_HEADS = 16
SSD_HEAD_DIM = 64
D_INNER = SSD_HEADS * SSD_HEAD_DIM
SSD_GROUPS = 2
GROUP_COLS = D_INNER // SSD_GROUPS
D_STATE = 128
CHUNK = 128
CONV_DIM = D_INNER + 2 * SSD_GROUPS * D_STATE
D_FF = 2752
EPS = 1e-6

LANES = 128
D_FF_PAD = 2816
DT_PAD = LANES
HALO = 16
DEPTH = 3
SUM_ROWS = 16
LOG2E = math.log2(math.e)
Q_PRESCALE = LOG2E / math.sqrt(HEAD_DIM)
VMEM_LIMIT = 56 * 1024 * 1024


def _params(sem):
    return pltpu.CompilerParams(dimension_semantics=sem, vmem_limit_bytes=VMEM_LIMIT)


def _rms(x, g):
    ms = jnp.mean(x * x, axis=-1, keepdims=True)
    return x * lax.rsqrt(ms + EPS) * g


def _dot(a, b):
    return jnp.dot(a, b, preferred_element_type=F32)


def _dot_nt(a, b):
    return lax.dot_general(a, b, (((1,), (1,)), ((), ())), preferred_element_type=F32)


def _dot_f32_by_01(x, m01):
    hi = x.astype(BF16)
    r1 = x - hi.astype(F32)
    mid = r1.astype(BF16)
    lo = (r1 - mid.astype(F32)).astype(BF16)
    return _dot(hi, m01) + _dot(mid, m01) + _dot(lo, m01)


def _dot_01_by_f32(m01, x):
    hi = x.astype(BF16)
    r1 = x - hi.astype(F32)
    mid = r1.astype(BF16)
    lo = (r1 - mid.astype(F32)).astype(BF16)
    return _dot(m01, hi) + _dot(m01, mid) + _dot(m01, lo)


def _softplus(x):
    return jnp.maximum(x, 0.0) + jnp.log1p(jnp.exp(-jnp.abs(x)))


def _silu(x):
    return x * jax.nn.sigmoid(x)


def _norm_matmul_kernel(x_ref, g_ref, w_ref, cs_ref, o_ref, h_ref):
    @pl.when(pl.program_id(1) == 0)
    def _():
        h_ref[...] = _rms(x_ref[...], g_ref[...]).astype(BF16)

    o_ref[...] = (_dot(h_ref[...], w_ref[...]) * cs_ref[...]).astype(o_ref.dtype)


def norm_matmul(x, g, w, col_scale, out_dtype, tm, tn):
    t, d = x.shape
    n = w.shape[1]
    return pl.pallas_call(
        _norm_matmul_kernel,
        grid=(t // tm, n // tn),
        in_specs=[
            pl.BlockSpec((tm, d), lambda i, j: (i, 0)),
            pl.BlockSpec((1, d), lambda i, j: (0, 0)),
            pl.BlockSpec((d, tn), lambda i, j: (0, j)),
            pl.BlockSpec((1, tn), lambda i, j: (0, j)),
        ],
        out_specs=pl.BlockSpec((tm, tn), lambda i, j: (i, j)),
        out_shape=jax.ShapeDtypeStruct((t, n), out_dtype),
        scratch_shapes=[pltpu.VMEM((tm, d), BF16)],
        compiler_params=_params(("parallel", "arbitrary")),
        name="norm_matmul",
    )(x, g, w, col_scale)


def _norm_matmul_t_kernel(x_ref, g_ref, wt_ref, o_ref, h_ref):
    @pl.when(pl.program_id(1) == 0)
    def _():
        h_ref[...] = _rms(x_ref[...], g_ref[...]).astype(BF16)

    o_ref[...] = _dot_nt(wt_ref[...], h_ref[...]).astype(o_ref.dtype)


def norm_matmul_t(x, g, wt, out_dtype, tm, tn):
    t, d = x.shape
    n = wt.shape[0]
    return pl.pallas_call(
        _norm_matmul_t_kernel,
        grid=(t // tm, n // tn),
        in_specs=[
            pl.BlockSpec((tm, d), lambda i, j: (i, 0)),
            pl.BlockSpec((1, d), lambda i, j: (0, 0)),
            pl.BlockSpec((tn, d), lambda i, j: (j, 0)),
        ],
        out_specs=pl.BlockSpec((tn, tm), lambda i, j: (j, i)),
        out_shape=jax.ShapeDtypeStruct((n, t), out_dtype),
        scratch_shapes=[pltpu.VMEM((tm, d), BF16)],
        compiler_params=_params(("parallel", "arbitrary")),
        name="norm_matmul_t",
    )(x, g, wt)


def _dt_proj_kernel(x_ref, g_ref, w_ref, wt_ref, dt_ref, dtt_ref):
    h = _rms(x_ref[...], g_ref[...]).astype(BF16)
    dt_ref[...] = _dot(h, w_ref[...])
    dtt_ref[...] = _dot_nt(wt_ref[...], h)


def dt_proj(x, g, w, wt, tm):
    t, d = x.shape
    r = wt.shape[0]
    return pl.pallas_call(
        _dt_proj_kernel,
        grid=(t // tm,),
        in_specs=[
            pl.BlockSpec((tm, d), lambda i: (i, 0)),
            pl.BlockSpec((1, d), lambda i: (0, 0)),
            pl.BlockSpec((d, DT_PAD), lambda i: (0, 0)),
            pl.BlockSpec((r, d), lambda i: (0, 0)),
        ],
        out_specs=[
            pl.BlockSpec((tm, DT_PAD), lambda i: (i, 0)),
            pl.BlockSpec((r, tm), lambda i: (0, i)),
        ],
        out_shape=[
            jax.ShapeDtypeStruct((t, DT_PAD), F32),
            jax.ShapeDtypeStruct((r, t), F32),
        ],
        compiler_params=_params(("parallel",)),
        name="dt_proj",
    )(x, g, w, wt)


def _attn_kernel(slopes_ref, lam_ref, q_ref, k_ref, vt_ref, lq1_ref, lk1_ref, lq2_ref, lk2_ref,
                 subln_ref, o_ref, kfeat_ref, s0_ref, s1_ref, s2_ref, p0_ref, p1_ref, p2_ref, *, tq, tk, nk):
    h = pl.program_id(1)
    qi = pl.program_id(2)
    slope2 = slopes_ref[h] * LOG2E
    q0pos = qi * tq
    n_off = nk - 1
    kd = lax.shift_right_logical(qi, int(math.log2(tk // tq)))

    def pieces(x):
        hi = x.astype(BF16).astype(F32)
        mid = (x - hi).astype(BF16).astype(F32)
        lo = (x - hi - mid).astype(BF16).astype(F32)
        return hi, mid, lo

    q = q_ref[...]
    lane = lax.broadcasted_iota(jnp.int32, q.shape, 1)
    zero = jnp.zeros_like(q)
    ii = lax.broadcasted_iota(jnp.int32, (tq, LANES), 0).astype(F32)
    fq = lax.broadcasted_iota(jnp.int32, (tq, LANES), 1)
    row_terms = pieces(-slope2 * ii)
    slope_terms = pieces(jnp.full((tq, LANES), slope2, F32))
    q_feat = jnp.zeros((tq, LANES), F32)
    for n in range(3):
        q_feat = jnp.where(fq == n, row_terms[n], q_feat)
        q_feat = jnp.where((fq == 3 + n) | (fq == 6 + n), slope_terms[n], q_feat)
    q_feat = q_feat.astype(BF16)
    q_ops = jnp.concatenate([
        jnp.concatenate([jnp.where(lane < HEAD_DIM, q, zero), q_feat], axis=1),
        jnp.concatenate([jnp.where(lane >= HEAD_DIM, q, zero), q_feat], axis=1)], axis=0)

    jj = lax.broadcasted_iota(jnp.int32, (tk, LANES), 0)
    fk = lax.broadcasted_iota(jnp.int32, (tk, LANES), 1)
    jj_lo = jnp.bitwise_and(jj, 255)
    jj_hi = (jj - jj_lo).astype(F32)
    jj_lo = jj_lo.astype(F32)
    k_feat = jnp.where(fk < 3, 1.0, jnp.where(fk < 6, jj_lo, jnp.where(fk < 9, jj_hi, 0.0)))
    kfeat_ref[0] = k_feat.astype(BF16)
    kfeat_ref[1] = (-k_feat).astype(BF16)
    ones_rows = jnp.ones((SUM_ROWS, tk), BF16)

    def raw_scores(kb, side):
        start = pl.multiple_of(kb * tk, tk)
        k_ops = jnp.concatenate([k_ref[pl.ds(start, tk), :], kfeat_ref[side]], axis=1)
        return _dot_nt(k_ops, q_ops)

    def offset(kb):
        return slope2 * jnp.abs(q0pos - kb * tk).astype(F32)

    def off_tile(t):
        side = (t >= kd).astype(jnp.int32)
        return t + side, side

    def produce(t, dst_ref):
        kb, side = off_tile(t)
        s = raw_scores(kb, side)
        dst_ref[...] = s
        return jnp.max(s, axis=0, keepdims=True)

    def softmax_stage(s, mx, c, m_old, p_ref):
        m_new = jnp.maximum(m_old, mx - c)
        p_ref[...] = jnp.exp2(s - (m_new + c)).astype(BF16)
        return m_new, jnp.exp2(m_old - m_new)

    def pv_stage(kb, p_ref, alpha, acc):
        vt = vt_ref[:, pl.ds(pl.multiple_of(kb * tk, tk), tk)]
        vt_ops = jnp.concatenate([vt, ones_rows], axis=0)
        return alpha * acc + _dot(vt_ops, p_ref[...])

    def prev_tile(t):
        return jnp.where(t < DEPTH, kd, off_tile(t - DEPTH)[0])

    trips = (n_off - 1) // DEPTH
    s_refs, p_refs = (s0_ref, s1_ref, s2_ref), (p0_ref, p1_ref, p2_ref)
    first = min(DEPTH, n_off)
    mx = [produce(jnp.int32(i), s_refs[i]) for i in range(first)]
    mx += [jnp.zeros((1, 2 * tq), F32)] * (DEPTH - first)

    c_d = offset(kd)
    s_d = jnp.minimum(raw_scores(kd, 0) - c_d, raw_scores(kd, 1) + c_d)
    m_run, alpha_d = softmax_stage(s_d, jnp.max(s_d, axis=0, keepdims=True), 0.0,
                                   jnp.full((1, 2 * tq), -jnp.inf, F32), p_refs[DEPTH - 1])
    for i in range(DEPTH - 1):
        p_refs[i][...] = jnp.zeros_like(p_refs[i])
    alphas = [jnp.ones((1, 2 * tq), F32)] * (DEPTH - 1) + [alpha_d]
    acc = jnp.zeros((V_DIM + SUM_ROWS, 2 * tq), F32)

    def trip(r, carry, n_produce):
        m_run, acc, mx, alphas = carry
        mx, alphas = list(mx), list(alphas)
        for i in range(DEPTH):
            t = DEPTH * r + i
            acc = pv_stage(prev_tile(t), p_refs[i], alphas[i], acc)
            m_run, alphas[i] = softmax_stage(s_refs[i][...], mx[i], offset(off_tile(t)[0]), m_run, p_refs[i])
            if i < n_produce:
                mx[i] = produce(t + DEPTH, s_refs[i])
        return m_run, acc, tuple(mx), tuple(alphas)

    carry = (m_run, acc, tuple(mx), tuple(alphas))
    if trips > 1:
        carry = lax.fori_loop(0, trips - 1, functools.partial(trip, n_produce=DEPTH), carry)
    if trips > 0:
        carry = trip(jnp.int32(trips - 1), carry, n_produce=1)
    m_run, acc, mx, alphas = carry
    last = jnp.int32(n_off - 1)
    if n_off > 0:
        acc = pv_stage(prev_tile(last), p_refs[0], alphas[0], acc)
        _, alpha_last = softmax_stage(s_refs[0][...], mx[0], offset(off_tile(last)[0]), m_run, p_refs[0])
        for i in range(1, DEPTH):
            acc = pv_stage(prev_tile(last + i), p_refs[i], alphas[i], acc)
        acc_fin = pv_stage(off_tile(last)[0], p_refs[0], alpha_last, acc)
    else:
        acc_fin = pv_stage(kd, p_refs[DEPTH - 1], alpha_d, acc)

    lam_init = lam_ref[0]
    one_minus = lam_ref[1]
    lam = (jnp.exp(jnp.sum(lq1_ref[...] * lk1_ref[...], axis=-1, keepdims=True))
           - jnp.exp(jnp.sum(lq2_ref[...] * lk2_ref[...], axis=-1, keepdims=True)) + lam_init)
    o_both = acc_fin[:V_DIM, :] / acc_fin[V_DIM:V_DIM + 1, :]
    o = o_both[:, :tq] - lam * o_both[:, tq:]
    ms = jnp.mean(o * o, axis=0, keepdims=True)
    y = o * lax.rsqrt(ms + EPS) * subln_ref[...] * one_minus
    o_ref[...] = y.T.astype(o_ref.dtype)


def diff_attention(qk, vt, slopes, lam_consts, lq1, lk1, lq2, lk2, subln_col, b, s, tq, tk):
    t = b * s
    nq = s // tq
    nk = s // tk
    assert nk == 1 or (nk - 2) % DEPTH == 0, (s, tk)
    assert tk % tq == 0 and tq <= 256
    hh = ATTN_HEADS
    smem = pl.BlockSpec(memory_space=pltpu.SMEM)
    vec = pl.BlockSpec((1, HEAD_DIM), lambda bi, h, qi: (0, 0))
    return pl.pallas_call(
        functools.partial(_attn_kernel, tq=tq, tk=tk, nk=nk),
        grid=(b, hh, nq),
        in_specs=[
            smem, smem,
            pl.BlockSpec((tq, V_DIM), lambda bi, h, qi: (bi * nq + qi, h)),
            pl.BlockSpec((s, V_DIM), lambda bi, h, qi: (bi, hh + h)),
            pl.BlockSpec((V_DIM, s), lambda bi, h, qi: (h, bi)),
            vec, vec, vec, vec,
            pl.BlockSpec((V_DIM, 1), lambda bi, h, qi: (0, 0)),
        ],
        out_specs=pl.BlockSpec((tq, V_DIM), lambda bi, h, qi: (bi * nq + qi, h)),
        out_shape=jax.ShapeDtypeStruct((t, hh * V_DIM), BF16),
        scratch_shapes=([pltpu.VMEM((2, tk, LANES), BF16)]
                        + [pltpu.VMEM((tk, 2 * tq), F32)] * DEPTH
                        + [pltpu.VMEM((tk, 2 * tq), BF16)] * DEPTH),
        compiler_params=_params(("parallel", "parallel", "arbitrary")),
        name="diff_attention",
    )(slopes, lam_consts, qk, qk, vt, lq1, lk1, lq2, lk2, subln_col)


def _ssd_kernel(*refs, rev, final, nc):
    if final:
        (xbc_ref, prev_ref, next_ref, dt_ref, dtt_ref, cw_ref, cb_ref, bias_f_ref, bias_c_ref,
         alog_f_ref, alog_c_ref, z_ref, yf_ref, dskip_ref, nw_ref, y_ref, st_ref) = refs
    else:
        (xbc_ref, prev_ref, next_ref, dt_ref, dtt_ref, cw_ref, cb_ref, bias_f_ref, bias_c_ref,
         alog_f_ref, alog_c_ref, y_ref, st_ref) = refs
    direction = 1 if rev else 0
    c = pl.program_id(1)
    cc = (nc - 1 - c) if rev else c
    ll = CHUNK

    @pl.when(c == 0)
    def _():
        st_ref[...] = jnp.zeros_like(st_ref)

    x = xbc_ref[...]
    prow = jnp.where(cc == 0, 0.0, prev_ref[7:8, :])
    nrow = jnp.where(cc == nc - 1, 0.0, next_ref[0:1, :])
    row = lax.broadcasted_iota(jnp.int32, (ll, 1), 0)
    xp = jnp.where(row == 0, prow, pltpu.roll(x, 1, axis=0))
    xn = jnp.where(row == ll - 1, nrow, pltpu.roll(x, ll - 1, axis=0))
    cw = cw_ref[...]
    u = _silu(cb_ref[...] + xp * cw[0:1, :] + x * cw[1:2, :] + xn * cw[2:3, :])
    xs = u[:, :D_INNER]
    bm = u[:, D_INNER:D_INNER + SSD_GROUPS * D_STATE]
    cm = u[:, D_INNER + SSD_GROUPS * D_STATE:]

    jj = lax.broadcasted_iota(jnp.int32, (DT_PAD, D_INNER), 0)
    col = lax.broadcasted_iota(jnp.int32, (DT_PAD, D_INNER), 1)
    head_of_col = lax.shift_right_logical(col, int(math.log2(SSD_HEAD_DIM)))
    expand = jnp.where(jj == direction * SSD_HEADS + head_of_col, 1.0, 0.0).astype(BF16)
    dt_full = _softplus(_dot_f32_by_01(dt_ref[...], expand) + bias_f_ref[...])
    a_full = dt_full * (-jnp.exp(alog_f_ref[...]))
    dt_rows = _softplus(dtt_ref[direction * SSD_HEADS:(direction + 1) * SSD_HEADS, :] + bias_c_ref[...])
    a_rows = dt_rows * (-jnp.exp(alog_c_ref[...]))

    ri = lax.broadcasted_iota(jnp.int32, (ll, ll), 0)
    ci = lax.broadcasted_iota(jnp.int32, (ll, ll), 1)
    if rev:
        keep = ci >= ri
        edge = 0
    else:
        keep = ci <= ri
        edge = ll - 1
    tri = jnp.where(keep, 1.0, 0.0).astype(BF16)
    tri_t = jnp.where((ri >= ci) if rev else (ri <= ci), 1.0, 0.0).astype(BF16)
    cum_full = _dot_01_by_f32(tri, a_full)
    cum_rows = _dot_f32_by_01(a_rows, tri_t)

    xd = xs * dt_full
    xd_b = xd.astype(BF16)
    cum_edge = cum_full[edge:edge + 1, :]
    xdw = (xd * jnp.exp(cum_edge - cum_full)).astype(BF16)
    grow = jnp.exp(cum_full)
    lane = lax.broadcasted_iota(jnp.int32, (ll, LANES), 1)

    y_parts = []
    for g in range(SSD_GROUPS):
        bg = bm[:, g * D_STATE:(g + 1) * D_STATE]
        cg = cm[:, g * D_STATE:(g + 1) * D_STATE].astype(BF16)
        cb = _dot_nt(cg, bg.astype(BF16))
        gs = slice(g * GROUP_COLS, (g + 1) * GROUP_COLS)
        st_in = st_ref[:, gs]
        y_off = _dot(cg, st_in.astype(BF16)) * grow[:, gs]
        heads_per_group = SSD_HEADS // SSD_GROUPS
        for pair in range(heads_per_group // 2):
            lo = g * GROUP_COLS + pair * LANES
            xd_pair = xd_b[:, lo:lo + LANES]
            outs = []
            for sub in range(2):
                hd = g * heads_per_group + pair * 2 + sub
                seg = cum_full[:, hd * SSD_HEAD_DIM:hd * SSD_HEAD_DIM + 1] - cum_rows[hd:hd + 1, :]
                dec = jnp.exp(jnp.where(keep, seg, -jnp.inf))
                outs.append(_dot((cb * dec).astype(BF16), xd_pair))
            y_diag = jnp.where(lane < SSD_HEAD_DIM, outs[0], outs[1])
            y_parts.append(y_diag + y_off[:, pair * LANES:(pair + 1) * LANES])
        st_chunk = _dot(bg.T.astype(BF16), xdw[:, gs])
        st_ref[:, gs] = st_in * jnp.exp(cum_edge[:, gs]) + st_chunk
    y = jnp.concatenate(y_parts, axis=1)

    if final:
        y = yf_ref[...] + y + xs * dskip_ref[...]
        y = y * _silu(z_ref[...])
        nw = nw_ref[...]
        normed = []
        for g in range(SSD_GROUPS):
            gs = slice(g * GROUP_COLS, (g + 1) * GROUP_COLS)
            normed.append(_rms(y[:, gs], nw[:, gs]))
        y = jnp.concatenate(normed, axis=1)
    y_ref[...] = y.astype(y_ref.dtype)


def ssd_pass(xbc, dt, dtt, cw, cb, bias_f, bias_c, alog_f, alog_c, extras, b, s, rev):
    final = extras is not None
    t = b * s
    nc = s // CHUNK
    rows8 = CHUNK // 8
    last8 = t // 8 - 1

    def cidx(bi, c):
        return bi * nc + ((nc - 1 - c) if rev else c)

    def full(shape):
        return pl.BlockSpec(shape, lambda bi, c: (0, 0))

    chunk_rows = lambda w: pl.BlockSpec((CHUNK, w), lambda bi, c: (cidx(bi, c), 0))
    in_specs = [
        chunk_rows(CONV_DIM),
        pl.BlockSpec((8, CONV_DIM), lambda bi, c: (jnp.maximum(cidx(bi, c) * rows8 - 1, 0), 0)),
        pl.BlockSpec((8, CONV_DIM), lambda bi, c: (jnp.minimum((cidx(bi, c) + 1) * rows8, last8), 0)),
        chunk_rows(DT_PAD),
        pl.BlockSpec((2 * SSD_HEADS, CHUNK), lambda bi, c: (0, cidx(bi, c))),
        full((3, CONV_DIM)), full((1, CONV_DIM)),
        full((1, D_INNER)), full((SSD_HEADS, 1)), full((1, D_INNER)), full((SSD_HEADS, 1)),
    ]
    args = [xbc, xbc, xbc, dt, dtt, cw, cb, bias_f, bias_c, alog_f, alog_c]
    if final:
        z, yf, dskip_f, nw = extras
        in_specs += [chunk_rows(D_INNER), chunk_rows(D_INNER), full((1, D_INNER)), full((1, D_INNER))]
        args += [z, yf, dskip_f, nw]
    return pl.pallas_call(
        functools.partial(_ssd_kernel, rev=rev, final=final, nc=nc),
        grid=(b, nc),
        in_specs=in_specs,
        out_specs=chunk_rows(D_INNER),
        out_shape=jax.ShapeDtypeStruct((t, D_INNER), BF16 if final else F32),
        scratch_shapes=[pltpu.VMEM((D_STATE, D_INNER), F32)],
        compiler_params=_params(("parallel", "arbitrary")),
        name="ssd_bwd_final" if final else "ssd_fwd",
    )(*args)


def _merge_kernel(attn_ref, ssd_ref, gates_ref, x_ref, woa_ref, wos_ref, wout_ref, nw_ref, o_ref):
    a = _dot(attn_ref[...], woa_ref[...])
    s = _dot(ssd_ref[...], wos_ref[...])
    gates = gates_ref[...]
    merged = jax.nn.sigmoid(gates[:, :D_MODEL]) * a + jax.nn.sigmoid(gates[:, D_MODEL:]) * s
    mo = _dot(merged.astype(BF16), wout_ref[...])
    o_ref[...] = x_ref[...] + _rms(mo, nw_ref[...])


def merge_out(attn, ssd, gates, x, woa, wos, wout, nw, tm):
    t, d = x.shape
    rows = lambda w: pl.BlockSpec((tm, w), lambda i: (i, 0))
    full = lambda shape: pl.BlockSpec(shape, lambda i: (0, 0))
    return pl.pallas_call(
        _merge_kernel,
        grid=(t // tm,),
        in_specs=[rows(d), rows(d), rows(2 * d), rows(d), full((d, d)), full((d, d)), full((d, d)),
                  full((1, d))],
        out_specs=rows(d),
        out_shape=jax.ShapeDtypeStruct((t, d), F32),
        compiler_params=_params(("parallel",)),
        name="merge_out",
    )(attn, ssd, gates, x, woa, wos, wout, nw)


def _ffn_kernel(x_ref, xp_ref, xn_ref, gpre_ref, wa_ref, wg_ref, cwa_ref, cwg_ref, cba_ref, cbg_ref,
                wd_ref, gpost_ref, o_ref, h_ref, acc_ref, *, tm, tiles_per_seq, nf):
    i = pl.program_id(0)
    f = pl.program_id(1)

    @pl.when(f == 0)
    def _():
        gpre = gpre_ref[...]
        pos = i % tiles_per_seq
        h_ref[0:tm, :] = _rms(x_ref[...], gpre).astype(BF16)
        hp = jnp.where(pos == 0, 0.0, _rms(xp_ref[...], gpre))
        hn = jnp.where(pos == tiles_per_seq - 1, 0.0, _rms(xn_ref[...], gpre))
        h_ref[tm:tm + HALO, :] = hp.astype(BF16)
        h_ref[tm + HALO:tm + 2 * HALO, :] = hn.astype(BF16)
        acc_ref[...] = jnp.zeros_like(acc_ref)

    h = h_ref[...]
    row = lax.broadcasted_iota(jnp.int32, (tm, 1), 0)

    def conv_branch(w_ref, cw_ref, cb_ref):
        u = _dot(h, w_ref[...])
        um = u[0:tm, :]
        prow = u[tm + HALO - 1:tm + HALO, :]
        nrow = u[tm + HALO:tm + HALO + 1, :]
        up = jnp.where(row == 0, prow, pltpu.roll(um, 1, axis=0))
        un = jnp.where(row == tm - 1, nrow, pltpu.roll(um, tm - 1, axis=0))
        cw = cw_ref[...]
        return cb_ref[...] + up * cw[0:1, :] + um * cw[1:2, :] + un * cw[2:3, :]

    a = conv_branch(wa_ref, cwa_ref, cba_ref)
    g = conv_branch(wg_ref, cwg_ref, cbg_ref)
    act = (_silu(g) * a).astype(BF16)
    acc_ref[...] += _dot(act, wd_ref[...])

    @pl.when(f == nf - 1)
    def _():
        o_ref[...] = x_ref[...] + _rms(acc_ref[...], gpost_ref[...])


def ffn(x, gpre, wa, wg, cwa, cwg, cba, cbg, wd, gpost, s, tm, tf):
    t, d = x.shape
    nf = D_FF_PAD // tf
    tiles_per_seq = s // tm
    blocks = tm // HALO
    last = t // HALO - 1
    full = lambda shape: pl.BlockSpec(shape, lambda i, f: (0, 0))
    colblk = lambda r: pl.BlockSpec((r, tf), lambda i, f: (0, f))
    return pl.pallas_call(
        functools.partial(_ffn_kernel, tm=tm, tiles_per_seq=tiles_per_seq, nf=nf),
        grid=(t // tm, nf),
        in_specs=[
            pl.BlockSpec((tm, d), lambda i, f: (i, 0)),
            pl.BlockSpec((HALO, d), lambda i, f: (jnp.maximum(i * blocks - 1, 0), 0)),
            pl.BlockSpec((HALO, d), lambda i, f: (jnp.minimum((i + 1) * blocks, last), 0)),
            full((1, d)),
            colblk(d), colblk(d), colblk(3), colblk(3), colblk(1), colblk(1),
            pl.BlockSpec((tf, d), lambda i, f: (f, 0)),
            full((1, d)),
        ],
        out_specs=pl.BlockSpec((tm, d), lambda i, f: (i, 0)),
        out_shape=jax.ShapeDtypeStruct((t, d), F32),
        scratch_shapes=[pltpu.VMEM((tm + 2 * HALO, d), BF16), pltpu.VMEM((tm, d), F32)],
        compiler_params=_params(("parallel", "arbitrary")),
        name="ffn",
    )(x, x, x, gpre, wa, wg, cwa, cwg, cba, cbg, wd, gpost)


def _tiles(s):
    return dict(
        tm_proj=min(1024, s), tn_proj=512,
        tq=min(256, s), tk=min(512, s),
        tm_merge=min(256, s),
        tm_ffn=min(1024, s), tf=256,
    )


def _layer(x, b, s, w, cfg):
    g_pre = w["norm_mix_pre"]
    tm, tn = cfg["tm_proj"], cfg["tn_proj"]
    qk_cols = ATTN_HEADS * 2 * HEAD_DIM
    qk_scale = jnp.concatenate([jnp.full((1, qk_cols), Q_PRESCALE, F32), jnp.ones((1, qk_cols), F32)], axis=1)
    unit = lambda n: jnp.ones((1, n), F32)
    qk = norm_matmul(x, g_pre, w["w_qk"], qk_scale, BF16, tm, tn)
    vt = norm_matmul_t(x, g_pre, w["w_vt"], BF16, tm, tn)
    z = norm_matmul(x, g_pre, w["w_z"], unit(D_INNER), F32, tm, tn)
    xbc = norm_matmul(x, g_pre, w["w_xbc"], unit(CONV_DIM), F32, tm, tn)
    gates = norm_matmul(x, g_pre, w["w_gates"], unit(2 * D_MODEL), F32, tm, tn)
    dt, dtt = dt_proj(x, g_pre, w["w_dt"], w["w_dtt"], tm)

    attn = diff_attention(qk, vt, w["slopes"], w["lam_consts"], w["lam_q1"], w["lam_k1"], w["lam_q2"],
                          w["lam_k2"], w["subln_col"], b, s, cfg["tq"], cfg["tk"])

    yf = ssd_pass(xbc, dt, dtt, w["conv_ssd_w"], w["conv_ssd_b"], w["bias_f"][0:1], w["bias_c"][0],
                  w["alog_f"][0:1], w["alog_c"][0], None, b, s, rev=False)
    ssd = ssd_pass(xbc, dt, dtt, w["conv_ssd_w"], w["conv_ssd_b"], w["bias_f"][1:2], w["bias_c"][1],
                   w["alog_f"][1:2], w["alog_c"][1], (z, yf, w["dskip_f"], w["ssd_norm"]), b, s, rev=True)

    x = merge_out(attn, ssd, gates, x, w["w_o_attn"], w["w_o_ssd"], w["w_out"], w["norm_mix_post"],
                  cfg["tm_merge"])
    x = ffn(x, w["norm_ffn_pre"], w["w_up_a"], w["w_up_g"], w["cw_a"], w["cw_g"], w["cb_a"], w["cb_g"],
            w["w_down"], w["norm_ffn_post"], s, cfg["tm_ffn"], cfg["tf"])
    return x


def _prepare_weights(norm_mix_pre, norm_mix_post, norm_ffn_pre, norm_ffn_post, w_in, lam_q1, lam_k1,
                     lam_q2, lam_k2, attn_subln, conv_ssd_w, conv_ssd_b, dt_bias, a_log, d_skip, ssd_norm,
                     w_o_attn, w_o_ssd, w_out, w_up, conv_ffn_w, conv_ffn_b, w_down):
    depth = w_in.shape[0]
    qk_cols = ATTN_HEADS * 2 * HEAD_DIM
    attn_w = ATTN_HEADS * V_DIM
    cuts = [0, 2 * qk_cols]
    for width in (attn_w, D_INNER, CONV_DIM, 2 * SSD_HEADS, 2 * D_MODEL):
        cuts.append(cuts[-1] + width)
    seg = lambda i: w_in[:, :, cuts[i]:cuts[i + 1]]
    w_dt = seg(4)
    row = lambda a: a[:, None, :]
    rep = lambda a: jnp.repeat(a, SSD_HEAD_DIM, axis=-1)
    pad_ff = lambda a: jnp.pad(a, [(0, 0)] * (a.ndim - 1) + [(0, D_FF_PAD - D_FF)])
    lam_init = [0.8 - 0.6 * math.exp(-0.3 * l) for l in range(depth)]
    return dict(
        norm_mix_pre=row(norm_mix_pre), norm_mix_post=row(norm_mix_post),
        norm_ffn_pre=row(norm_ffn_pre), norm_ffn_post=row(norm_ffn_post),
        w_qk=seg(0).astype(BF16),
        w_vt=jnp.swapaxes(seg(1), 1, 2).astype(BF16),
        w_z=seg(2).astype(BF16),
        w_xbc=seg(3).astype(BF16),
        w_dt=jnp.pad(w_dt, ((0, 0), (0, 0), (0, DT_PAD - 2 * SSD_HEADS))).astype(BF16),
        w_dtt=jnp.swapaxes(w_dt, 1, 2).astype(BF16),
        w_gates=seg(5).astype(BF16),
        slopes=jnp.tile(jnp.asarray([2.0 ** (-8.0 * (i + 1) / ATTN_HEADS) for i in range(ATTN_HEADS)],
                                    F32)[None], (depth, 1)),
        lam_consts=jnp.asarray([[li, 1.0 - li] for li in lam_init], F32),
        lam_q1=row(lam_q1), lam_k1=row(lam_k1), lam_q2=row(lam_q2), lam_k2=row(lam_k2),
        subln_col=attn_subln[:, :, None],
        conv_ssd_w=conv_ssd_w, conv_ssd_b=row(conv_ssd_b),
        bias_f=rep(dt_bias), bias_c=dt_bias[..., None],
        alog_f=rep(a_log), alog_c=a_log[..., None],
        dskip_f=row(rep(d_skip)), ssd_norm=row(ssd_norm),
        w_o_attn=w_o_attn.astype(BF16), w_o_ssd=w_o_ssd.astype(BF16), w_out=w_out.astype(BF16),
        w_up_a=pad_ff(w_up[:, :, :D_FF]).astype(BF16), w_up_g=pad_ff(w_up[:, :, D_FF:]).astype(BF16),
        cw_a=pad_ff(conv_ffn_w[:, :, :D_FF]), cw_g=pad_ff(conv_ffn_w[:, :, D_FF:]),
        cb_a=row(pad_ff(conv_ffn_b[:, :D_FF])), cb_g=row(pad_ff(conv_ffn_b[:, D_FF:])),
        w_down=jnp.pad(w_down, ((0, 0), (0, D_FF_PAD - D_FF), (0, 0))).astype(BF16),
    )


def kernel(x_prompt, x_sample, norm_mix_pre, norm_mix_post, norm_ffn_pre, norm_ffn_post, w_in, lam_q1, lam_k1, lam_q2, lam_k2, attn_subln, conv_ssd_w, conv_ssd_b, dt_bias, a_log, d_skip, ssd_norm, w_o_attn, w_o_ssd, w_out, w_up, conv_ffn_w, conv_ffn_b, w_down):
    weights = _prepare_weights(norm_mix_pre, norm_mix_post, norm_ffn_pre, norm_ffn_post, w_in, lam_q1,
                               lam_k1, lam_q2, lam_k2, attn_subln, conv_ssd_w, conv_ssd_b, dt_bias, a_log,
                               d_skip, ssd_norm, w_o_attn, w_o_ssd, w_out, w_up, conv_ffn_w, conv_ffn_b,
                               w_down)
    groups = []
    for xg in (x_prompt, x_sample):
        b, s, d = xg.shape
        groups.append((b, s, _tiles(s)))

    def step(carry, w):
        out = tuple(_layer(x, b, s, w, cfg) for x, (b, s, cfg) in zip(carry, groups))
        return out, None

    init = tuple(xg.reshape(-1, xg.shape[-1]) for xg in (x_prompt, x_sample))
    out, _ = lax.scan(step, init, weights)
    return tuple(o.reshape(xg.shape) for o, xg in zip(out, (x_prompt, x_sample)))
```

```python
import functools
import math

import jax
import jax.numpy as jnp
from jax import lax
from jax.experimental import pallas as pl
from jax.experimental.pallas import tpu as pltpu

F32 = jnp.float32
BF16 = jnp.bfloat16

D_MODEL = 1024
ATTN_HEADS = 8
HEAD_DIM = 64
V_DIM = 2 * HEAD_DIM
SSD_HEADS = 16
SSD_HEAD_DIM = 64
D_INNER = SSD_HEADS * SSD_HEAD_DIM
SSD_GROUPS = 2
GROUP_COLS = D_INNER // SSD_GROUPS
D_STATE = 128
CHUNK = 128
CONV_DIM = D_INNER + 2 * SSD_GROUPS * D_STATE
D_FF = 2752
EPS = 1e-6

LANES = 128
D_FF_PAD = 2816
DT_PAD = LANES
HALO = 16
PROJ_CHUNK = 512
DEPTH = 3
SUM_ROWS = 16
LOG2E = math.log2(math.e)
Q_PRESCALE = LOG2E / math.sqrt(HEAD_DIM)
VMEM_LIMIT = 56 * 1024 * 1024


def _params(sem):
    return pltpu.CompilerParams(dimension_semantics=sem, vmem_limit_bytes=VMEM_LIMIT)


def _rms(x, g):
    ms = jnp.mean(x * x, axis=-1, keepdims=True)
    return x * lax.rsqrt(ms + EPS) * g


def _dot(a, b):
    return jnp.dot(a, b, preferred_element_type=F32)


def _dot_nt(a, b):
    return lax.dot_general(a, b, (((1,), (1,)), ((), ())), preferred_element_type=F32)


def _dot_f32_by_01(x, m01):
    hi = x.astype(BF16)
    r1 = x - hi.astype(F32)
    mid = r1.astype(BF16)
    lo = (r1 - mid.astype(F32)).astype(BF16)
    return _dot(hi, m01) + _dot(mid, m01) + _dot(lo, m01)


def _dot_01_by_f32(m01, x):
    hi = x.astype(BF16)
    r1 = x - hi.astype(F32)
    mid = r1.astype(BF16)
    lo = (r1 - mid.astype(F32)).astype(BF16)
    return _dot(m01, hi) + _dot(m01, mid) + _dot(m01, lo)


def _softplus(x):
    return jnp.maximum(x, 0.0) + jnp.log1p(jnp.exp(-jnp.abs(x)))


def _silu(x):
    return x * jax.nn.sigmoid(x)


def _in_proj_kernel(x_ref, g_ref, wqk_ref, wvt_ref, wz_ref, wxbc_ref, wg_ref, wdt_ref, wdtt_ref,
                    qk_ref, vt_ref, z_ref, xbc_ref, gates_ref, dt_ref, dtt_ref):
    h = _rms(x_ref[...], g_ref[...]).astype(BF16)
    qk_cols = ATTN_HEADS * 2 * HEAD_DIM

    def project(w_ref, o_ref, scale_upto=0):
        n = w_ref.shape[1]
        for c0 in range(0, n, PROJ_CHUNK):
            c1 = min(c0 + PROJ_CHUNK, n)
            y = _dot(h, w_ref[:, c0:c1])
            if c1 <= scale_upto:
                y = y * Q_PRESCALE
            o_ref[:, c0:c1] = y.astype(o_ref.dtype)

    project(wqk_ref, qk_ref, scale_upto=qk_cols)
    for r0 in range(0, wvt_ref.shape[0], PROJ_CHUNK):
        vt_ref[r0:r0 + PROJ_CHUNK, :] = _dot_nt(wvt_ref[r0:r0 + PROJ_CHUNK, :], h).astype(vt_ref.dtype)
    project(wz_ref, z_ref)
    project(wxbc_ref, xbc_ref)
    project(wg_ref, gates_ref)
    project(wdt_ref, dt_ref)
    dtt_ref[...] = _dot_nt(wdtt_ref[...], h)


def in_proj(x, g, w, tm):
    t, d = x.shape
    weights = [w["w_qk"], w["w_vt"], w["w_z"], w["w_xbc"], w["w_gates"], w["w_dt"], w["w_dtt"]]
    rows = lambda n: pl.BlockSpec((tm, n), lambda i: (i, 0))
    cols = lambda n: pl.BlockSpec((n, tm), lambda i: (0, i))
    resident = lambda a: pl.BlockSpec(a.shape, lambda i: (0, 0), pipeline_mode=pl.Buffered(1))
    n_qk, n_v, n_z, n_xbc, n_g = (w["w_qk"].shape[1], w["w_vt"].shape[0], w["w_z"].shape[1],
                                  w["w_xbc"].shape[1], w["w_gates"].shape[1])
    n_dtt = w["w_dtt"].shape[0]
    return pl.pallas_call(
        _in_proj_kernel,
        grid=(t // tm,),
        in_specs=[rows(d), resident(g)] + [resident(a) for a in weights],
        out_specs=[rows(n_qk), cols(n_v), rows(n_z), rows(n_xbc), rows(n_g), rows(DT_PAD), cols(n_dtt)],
        out_shape=[
            jax.ShapeDtypeStruct((t, n_qk), BF16), jax.ShapeDtypeStruct((n_v, t), BF16),
            jax.ShapeDtypeStruct((t, n_z), F32), jax.ShapeDtypeStruct((t, n_xbc), F32),
            jax.ShapeDtypeStruct((t, n_g), F32), jax.ShapeDtypeStruct((t, DT_PAD), F32),
            jax.ShapeDtypeStruct((n_dtt, t), F32),
        ],
        compiler_params=_params(("parallel",)),
        name="in_proj",
    )(x, g, *weights)


def _attn_kernel(slopes_ref, lam_ref, q_ref, k_ref, vt_ref, lq1_ref, lk1_ref, lq2_ref, lk2_ref,
                 subln_ref, o_ref, kfeat_ref, s0_ref, s1_ref, s2_ref, p0_ref, p1_ref, p2_ref, *, tq, tk, nk):
    h = pl.program_id(1)
    qi = pl.program_id(2)
    slope2 = slopes_ref[h] * LOG2E
    q0pos = qi * tq
    n_off = nk - 1
    kd = lax.shift_right_logical(qi, int(math.log2(tk // tq)))

    def pieces(x):
        hi = x.astype(BF16).astype(F32)
        mid = (x - hi).astype(BF16).astype(F32)
        lo = (x - hi - mid).astype(BF16).astype(F32)
        return hi, mid, lo

    q = q_ref[...]
    lane = lax.broadcasted_iota(jnp.int32, q.shape, 1)
    zero = jnp.zeros_like(q)
    ii = lax.broadcasted_iota(jnp.int32, (tq, LANES), 0).astype(F32)
    fq = lax.broadcasted_iota(jnp.int32, (tq, LANES), 1)
    row_terms = pieces(-slope2 * ii)
    slope_terms = pieces(jnp.full((tq, LANES), slope2, F32))
    q_feat = jnp.zeros((tq, LANES), F32)
    for n in range(3):
        q_feat = jnp.where(fq == n, row_terms[n], q_feat)
        q_feat = jnp.where((fq == 3 + n) | (fq == 6 + n), slope_terms[n], q_feat)
    q_feat = q_feat.astype(BF16)
    q_ops = jnp.concatenate([
        jnp.concatenate([jnp.where(lane < HEAD_DIM, q, zero), q_feat], axis=1),
        jnp.concatenate([jnp.where(lane >= HEAD_DIM, q, zero), q_feat], axis=1)], axis=0)

    jj = lax.broadcasted_iota(jnp.int32, (tk, LANES), 0)
    fk = lax.broadcasted_iota(jnp.int32, (tk, LANES), 1)
    jj_lo = jnp.bitwise_and(jj, 255)
    jj_hi = (jj - jj_lo).astype(F32)
    jj_lo = jj_lo.astype(F32)
    k_feat = jnp.where(fk < 3, 1.0, jnp.where(fk < 6, jj_lo, jnp.where(fk < 9, jj_hi, 0.0)))
    kfeat_ref[0] = k_feat.astype(BF16)
    kfeat_ref[1] = (-k_feat).astype(BF16)
    ones_rows = jnp.ones((SUM_ROWS, tk), BF16)

    def raw_scores(kb, side):
        start = pl.multiple_of(kb * tk, tk)
        k_ops = jnp.concatenate([k_ref[pl.ds(start, tk), :], kfeat_ref[side]], axis=1)
        return _dot_nt(k_ops, q_ops)

    def offset(kb):
        return slope2 * jnp.abs(q0pos - kb * tk).astype(F32)

    def off_tile(t):
        side = (t >= kd).astype(jnp.int32)
        return t + side, side

    def produce(t, dst_ref):
        kb, side = off_tile(t)
        s = raw_scores(kb, side)
        dst_ref[...] = s
        return jnp.max(s, axis=0, keepdims=True)

    def softmax_stage(s, mx, c, m_old, p_ref):
        m_new = jnp.maximum(m_old, mx - c)
        p_ref[...] = jnp.exp2(s - (m_new + c)).astype(BF16)
        return m_new, jnp.exp2(m_old - m_new)

    def pv_stage(kb, p_ref, alpha, acc):
        vt = vt_ref[:, pl.ds(pl.multiple_of(kb * tk, tk), tk)]
        vt_ops = jnp.concatenate([vt, ones_rows], axis=0)
        return alpha * acc + _dot(vt_ops, p_ref[...])

    def prev_tile(t):
        return jnp.where(t < DEPTH, kd, off_tile(t - DEPTH)[0])

    trips = (n_off - 1) // DEPTH
    s_refs, p_refs = (s0_ref, s1_ref, s2_ref), (p0_ref, p1_ref, p2_ref)
    first = min(DEPTH, n_off)
    mx = [produce(jnp.int32(i), s_refs[i]) for i in range(first)]
    mx += [jnp.zeros((1, 2 * tq), F32)] * (DEPTH - first)

    c_d = offset(kd)
    s_d = jnp.minimum(raw_scores(kd, 0) - c_d, raw_scores(kd, 1) + c_d)
    m_run, alpha_d = softmax_stage(s_d, jnp.max(s_d, axis=0, keepdims=True), 0.0,
                                   jnp.full((1, 2 * tq), -jnp.inf, F32), p_refs[DEPTH - 1])
    for i in range(DEPTH - 1):
        p_refs[i][...] = jnp.zeros_like(p_refs[i])
    alphas = [jnp.ones((1, 2 * tq), F32)] * (DEPTH - 1) + [alpha_d]
    acc = jnp.zeros((V_DIM + SUM_ROWS, 2 * tq), F32)

    def trip(r, carry, n_produce):
        m_run, acc, mx, alphas = carry
        mx, alphas = list(mx), list(alphas)
        for i in range(DEPTH):
            t = DEPTH * r + i
            acc = pv_stage(prev_tile(t), p_refs[i], alphas[i], acc)
            m_run, alphas[i] = softmax_stage(s_refs[i][...], mx[i], offset(off_tile(t)[0]), m_run, p_refs[i])
            if i < n_produce:
                mx[i] = produce(t + DEPTH, s_refs[i])
        return m_run, acc, tuple(mx), tuple(alphas)

    carry = (m_run, acc, tuple(mx), tuple(alphas))
    if trips > 1:
        carry = lax.fori_loop(0, trips - 1, functools.partial(trip, n_produce=DEPTH), carry)
    if trips > 0:
        carry = trip(jnp.int32(trips - 1), carry, n_produce=1)
    m_run, acc, mx, alphas = carry
    last = jnp.int32(n_off - 1)
    if n_off > 0:
        acc = pv_stage(prev_tile(last), p_refs[0], alphas[0], acc)
        _, alpha_last = softmax_stage(s_refs[0][...], mx[0], offset(off_tile(last)[0]), m_run, p_refs[0])
        for i in range(1, DEPTH):
            acc = pv_stage(prev_tile(last + i), p_refs[i], alphas[i], acc)
        acc_fin = pv_stage(off_tile(last)[0], p_refs[0], alpha_last, acc)
    else:
        acc_fin = pv_stage(kd, p_refs[DEPTH - 1], alpha_d, acc)

    lam_init = lam_ref[0]
    one_minus = lam_ref[1]
    lam = (jnp.exp(jnp.sum(lq1_ref[...] * lk1_ref[...], axis=-1, keepdims=True))
           - jnp.exp(jnp.sum(lq2_ref[...] * lk2_ref[...], axis=-1, keepdims=True)) + lam_init)
    o_both = acc_fin[:V_DIM, :] / acc_fin[V_DIM:V_DIM + 1, :]
    o = o_both[:, :tq] - lam * o_both[:, tq:]
    ms = jnp.mean(o * o, axis=0, keepdims=True)
    y = o * lax.rsqrt(ms + EPS) * subln_ref[...] * one_minus
    o_ref[...] = y.T.astype(o_ref.dtype)


def diff_attention(qk, vt, slopes, lam_consts, lq1, lk1, lq2, lk2, subln_col, b, s, tq, tk):
    t = b * s
    nq = s // tq
    nk = s // tk
    assert nk == 1 or (nk - 2) % DEPTH == 0, (s, tk)
    assert tk % tq == 0 and tq <= 256
    hh = ATTN_HEADS
    smem = pl.BlockSpec(memory_space=pltpu.SMEM)
    vec = pl.BlockSpec((1, HEAD_DIM), lambda bi, h, qi: (0, 0))
    return pl.pallas_call(
        functools.partial(_attn_kernel, tq=tq, tk=tk, nk=nk),
        grid=(b, hh, nq),
        in_specs=[
            smem, smem,
            pl.BlockSpec((tq, V_DIM), lambda bi, h, qi: (bi * nq + qi, h)),
            pl.BlockSpec((s, V_DIM), lambda bi, h, qi: (bi, hh + h)),
            pl.BlockSpec((V_DIM, s), lambda bi, h, qi: (h, bi)),
            vec, vec, vec, vec,
            pl.BlockSpec((V_DIM, 1), lambda bi, h, qi: (0, 0)),
        ],
        out_specs=pl.BlockSpec((tq, V_DIM), lambda bi, h, qi: (bi * nq + qi, h)),
        out_shape=jax.ShapeDtypeStruct((t, hh * V_DIM), BF16),
        scratch_shapes=([pltpu.VMEM((2, tk, LANES), BF16)]
                        + [pltpu.VMEM((tk, 2 * tq), F32)] * DEPTH
                        + [pltpu.VMEM((tk, 2 * tq), BF16)] * DEPTH),
        compiler_params=_params(("parallel", "parallel", "arbitrary")),
        name="diff_attention",
    )(slopes, lam_consts, qk, qk, vt, lq1, lk1, lq2, lk2, subln_col)


def _ssd_kernel(*refs, rev, final, nc):
    if final:
        (xbc_ref, prev_ref, next_ref, dt_ref, dtt_ref, cw_ref, cb_ref, bias_p_ref, bias_c_ref,
         alog_p_ref, alog_c_ref, z_ref, yf_ref, dskip_ref, nw_ref, y_ref, st_ref) = refs
    else:
        (xbc_ref, prev_ref, next_ref, dt_ref, dtt_ref, cw_ref, cb_ref, bias_p_ref, bias_c_ref,
         alog_p_ref, alog_c_ref, y_ref, st_ref) = refs
    direction = 1 if rev else 0
    c = pl.program_id(1)
    cc = (nc - 1 - c) if rev else c
    ll = CHUNK

    @pl.when(c == 0)
    def _():
        st_ref[...] = jnp.zeros_like(st_ref)

    x = xbc_ref[...]
    before = jnp.where(cc == 0, 0.0, prev_ref[...])
    after = jnp.where(cc == nc - 1, 0.0, next_ref[...])
    x_ext = jnp.concatenate([x, after, before], axis=0)
    n_ext = x_ext.shape[0]
    xp = pltpu.roll(x_ext, 1, axis=0)[0:ll, :]
    xn = pltpu.roll(x_ext, n_ext - 1, axis=0)[0:ll, :]
    cw = cw_ref[...]
    u = _silu(cb_ref[...] + xp * cw[0:1, :] + x * cw[1:2, :] + xn * cw[2:3, :])
    xs = u[:, :D_INNER]
    bm = u[:, D_INNER:D_INNER + SSD_GROUPS * D_STATE]
    cm = u[:, D_INNER + SSD_GROUPS * D_STATE:]

    jj = lax.broadcasted_iota(jnp.int32, (DT_PAD, D_INNER), 0)
    col = lax.broadcasted_iota(jnp.int32, (DT_PAD, D_INNER), 1)
    head_of_col = lax.shift_right_logical(col, int(math.log2(SSD_HEAD_DIM)))
    expand = jnp.where(jj == direction * SSD_HEADS + head_of_col, 1.0, 0.0).astype(BF16)
    dt_cols = _softplus(dt_ref[...] + bias_p_ref[...])
    a_cols = dt_cols * (-jnp.exp(alog_p_ref[...]))
    dt_rows = _softplus(dtt_ref[direction * SSD_HEADS:(direction + 1) * SSD_HEADS, :] + bias_c_ref[...])
    a_rows = dt_rows * (-jnp.exp(alog_c_ref[...]))

    ri = lax.broadcasted_iota(jnp.int32, (ll, ll), 0)
    ci = lax.broadcasted_iota(jnp.int32, (ll, ll), 1)
    if rev:
        keep = ci >= ri
        edge = 0
    else:
        keep = ci <= ri
        edge = ll - 1
    tri = jnp.where(keep, 1.0, 0.0).astype(BF16)
    tri_t = jnp.where((ri >= ci) if rev else (ri <= ci), 1.0, 0.0).astype(BF16)
    cum_cols = _dot_01_by_f32(tri, a_cols)
    dt_full = _dot_f32_by_01(dt_cols, expand)
    cum_full = _dot_f32_by_01(cum_cols, expand)
    cum_rows = _dot_f32_by_01(a_rows, tri_t)

    xd = xs * dt_full
    xd_b = xd.astype(BF16)
    cum_edge = cum_full[edge:edge + 1, :]
    xdw = (xd * jnp.exp(cum_edge - cum_full)).astype(BF16)
    grow = jnp.exp(cum_full)
    lane = lax.broadcasted_iota(jnp.int32, (ll, LANES), 1)

    y_parts = []
    for g in range(SSD_GROUPS):
        bg = bm[:, g * D_STATE:(g + 1) * D_STATE]
        cg = cm[:, g * D_STATE:(g + 1) * D_STATE].astype(BF16)
        cb = _dot_nt(cg, bg.astype(BF16))
        gs = slice(g * GROUP_COLS, (g + 1) * GROUP_COLS)
        st_in = st_ref[:, gs]
        y_off = _dot(cg, st_in.astype(BF16)) * grow[:, gs]
        heads_per_group = SSD_HEADS // SSD_GROUPS
        for pair in range(heads_per_group // 2):
            lo = g * GROUP_COLS + pair * LANES
            xd_pair = xd_b[:, lo:lo + LANES]
            outs = []
            for sub in range(2):
                hd = g * heads_per_group + pair * 2 + sub
                seg = cum_full[:, hd * SSD_HEAD_DIM:hd * SSD_HEAD_DIM + 1] - cum_rows[hd:hd + 1, :]
                dec = jnp.exp(jnp.where(keep, seg, -jnp.inf))
                outs.append(_dot((cb * dec).astype(BF16), xd_pair))
            y_diag = jnp.where(lane < SSD_HEAD_DIM, outs[0], outs[1])
            y_parts.append(y_diag + y_off[:, pair * LANES:(pair + 1) * LANES])
        st_chunk = _dot(bg.T.astype(BF16), xdw[:, gs])
        st_ref[:, gs] = st_in * jnp.exp(cum_edge[:, gs]) + st_chunk
    y = jnp.concatenate(y_parts, axis=1)

    if final:
        y = yf_ref[...] + y + xs * dskip_ref[...]
        y = y * _silu(z_ref[...])
        nw = nw_ref[...]
        normed = []
        for g in range(SSD_GROUPS):
            gs = slice(g * GROUP_COLS, (g + 1) * GROUP_COLS)
            normed.append(_rms(y[:, gs], nw[:, gs]))
        y = jnp.concatenate(normed, axis=1)
    y_ref[...] = y.astype(y_ref.dtype)


def ssd_pass(xbc, dt, dtt, cw, cb, bias_p, bias_c, alog_p, alog_c, extras, b, s, rev):
    final = extras is not None
    t = b * s
    nc = s // CHUNK
    rows8 = CHUNK // 8
    last8 = t // 8 - 1

    def cidx(bi, c):
        return bi * nc + ((nc - 1 - c) if rev else c)

    def full(shape):
        return pl.BlockSpec(shape, lambda bi, c: (0, 0))

    chunk_rows = lambda w: pl.BlockSpec((CHUNK, w), lambda bi, c: (cidx(bi, c), 0))
    in_specs = [
        chunk_rows(CONV_DIM),
        pl.BlockSpec((8, CONV_DIM), lambda bi, c: (jnp.maximum(cidx(bi, c) * rows8 - 1, 0), 0)),
        pl.BlockSpec((8, CONV_DIM), lambda bi, c: (jnp.minimum((cidx(bi, c) + 1) * rows8, last8), 0)),
        chunk_rows(DT_PAD),
        pl.BlockSpec((2 * SSD_HEADS, CHUNK), lambda bi, c: (0, cidx(bi, c))),
        full((3, CONV_DIM)), full((1, CONV_DIM)),
        full((1, DT_PAD)), full((SSD_HEADS, 1)), full((1, DT_PAD)), full((SSD_HEADS, 1)),
    ]
    args = [xbc, xbc, xbc, dt, dtt, cw, cb, bias_p, bias_c, alog_p, alog_c]
    if final:
        z, yf, dskip_f, nw = extras
        in_specs += [chunk_rows(D_INNER), chunk_rows(D_INNER), full((1, D_INNER)), full((1, D_INNER))]
        args += [z, yf, dskip_f, nw]
    return pl.pallas_call(
        functools.partial(_ssd_kernel, rev=rev, final=final, nc=nc),
        grid=(b, nc),
        in_specs=in_specs,
        out_specs=chunk_rows(D_INNER),
        out_shape=jax.ShapeDtypeStruct((t, D_INNER), BF16 if final else F32),
        scratch_shapes=[pltpu.VMEM((D_STATE, D_INNER), F32)],
        compiler_params=_params(("parallel", "arbitrary")),
        name="ssd_bwd_final" if final else "ssd_fwd",
    )(*args)


def _merge_kernel(attn_ref, ssd_ref, gates_ref, x_ref, woa_ref, wos_ref, wout_ref, nw_ref, o_ref):
    a = _dot(attn_ref[...], woa_ref[...])
    s = _dot(ssd_ref[...], wos_ref[...])
    gates = gates_ref[...]
    merged = jax.nn.sigmoid(gates[:, :D_MODEL]) * a + jax.nn.sigmoid(gates[:, D_MODEL:]) * s
    mo = _dot(merged.astype(BF16), wout_ref[...])
    o_ref[...] = x_ref[...] + _rms(mo, nw_ref[...])


def merge_out(attn, ssd, gates, x, woa, wos, wout, nw, tm):
    t, d = x.shape
    rows = lambda w: pl.BlockSpec((tm, w), lambda i: (i, 0))
    full = lambda shape: pl.BlockSpec(shape, lambda i: (0, 0))
    return pl.pallas_call(
        _merge_kernel,
        grid=(t // tm,),
        in_specs=[rows(d), rows(d), rows(2 * d), rows(d), full((d, d)), full((d, d)), full((d, d)),
                  full((1, d))],
        out_specs=rows(d),
        out_shape=jax.ShapeDtypeStruct((t, d), F32),
        compiler_params=_params(("parallel",)),
        name="merge_out",
    )(attn, ssd, gates, x, woa, wos, wout, nw)


def _ffn_kernel(x_ref, xp_ref, xn_ref, gpre_ref, wa_ref, wg_ref, cwa_ref, cwg_ref, cba_ref, cbg_ref,
                wd_ref, gpost_ref, o_ref, h_ref, acc_ref, *, tm, tiles_per_seq, nf):
    i = pl.program_id(0)
    f = pl.program_id(1)

    @pl.when(f == 0)
    def _():
        gpre = gpre_ref[...]
        pos = i % tiles_per_seq
        h_ref[0:tm, :] = _rms(x_ref[...], gpre).astype(BF16)
        hp = jnp.where(pos == 0, 0.0, _rms(xp_ref[...], gpre))
        hn = jnp.where(pos == tiles_per_seq - 1, 0.0, _rms(xn_ref[...], gpre))
        h_ref[tm:tm + HALO, :] = hn.astype(BF16)
        h_ref[tm + HALO:tm + 2 * HALO, :] = hp.astype(BF16)
        acc_ref[...] = jnp.zeros_like(acc_ref)

    h = h_ref[...]
    n_ext = tm + 2 * HALO

    def conv_branch(w_ref, cw_ref, cb_ref):
        u = _dot(h, w_ref[...])
        um = u[0:tm, :]
        up = pltpu.roll(u, 1, axis=0)[0:tm, :]
        un = pltpu.roll(u, n_ext - 1, axis=0)[0:tm, :]
        cw = cw_ref[...]
        return cb_ref[...] + up * cw[0:1, :] + um * cw[1:2, :] + un * cw[2:3, :]

    a = conv_branch(wa_ref, cwa_ref, cba_ref)
    g = conv_branch(wg_ref, cwg_ref, cbg_ref)
    act = (_silu(g) * a).astype(BF16)
    acc_ref[...] += _dot(act, wd_ref[...])

    @pl.when(f == nf - 1)
    def _():
        o_ref[...] = x_ref[...] + _rms(acc_ref[...], gpost_ref[...])


def ffn(x, gpre, wa, wg, cwa, cwg, cba, cbg, wd, gpost, s, tm, tf):
    t, d = x.shape
    nf = D_FF_PAD // tf
    tiles_per_seq = s // tm
    blocks = tm // HALO
    last = t // HALO - 1
    full = lambda shape: pl.BlockSpec(shape, lambda i, f: (0, 0))
    colblk = lambda r: pl.BlockSpec((r, tf), lambda i, f: (0, f))
    return pl.pallas_call(
        functools.partial(_ffn_kernel, tm=tm, tiles_per_seq=tiles_per_seq, nf=nf),
        grid=(t // tm, nf),
        in_specs=[
            pl.BlockSpec((tm, d), lambda i, f: (i, 0)),
            pl.BlockSpec((HALO, d), lambda i, f: (jnp.maximum(i * blocks - 1, 0), 0)),
            pl.BlockSpec((HALO, d), lambda i, f: (jnp.minimum((i + 1) * blocks, last), 0)),
            full((1, d)),
            colblk(d), colblk(d), colblk(3), colblk(3), colblk(1), colblk(1),
            pl.BlockSpec((tf, d), lambda i, f: (f, 0)),
            full((1, d)),
        ],
        out_specs=pl.BlockSpec((tm, d), lambda i, f: (i, 0)),
        out_shape=jax.ShapeDtypeStruct((t, d), F32),
        scratch_shapes=[pltpu.VMEM((tm + 2 * HALO, d), BF16), pltpu.VMEM((tm, d), F32)],
        compiler_params=_params(("parallel", "arbitrary")),
        name="ffn",
    )(x, x, x, gpre, wa, wg, cwa, cwg, cba, cbg, wd, gpost)


def _tiles(s):
    return dict(
        tm_proj=min(256, s),
        tq=min(256, s), tk=min(512, s),
        tm_merge=min(256, s),
        tm_ffn=min(1024, s), tf=256,
    )


def _layer(x, b, s, w, cfg):
    g_pre = w["norm_mix_pre"]
    qk, vt, z, xbc, gates, dt, dtt = in_proj(x, g_pre, w, cfg["tm_proj"])

    attn = diff_attention(qk, vt, w["slopes"], w["lam_consts"], w["lam_q1"], w["lam_k1"], w["lam_q2"],
                          w["lam_k2"], w["subln_col"], b, s, cfg["tq"], cfg["tk"])

    yf = ssd_pass(xbc, dt, dtt, w["conv_ssd_w"], w["conv_ssd_b"], w["bias_p"], w["bias_c"][0],
                  w["alog_p"], w["alog_c"][0], None, b, s, rev=False)
    ssd = ssd_pass(xbc, dt, dtt, w["conv_ssd_w"], w["conv_ssd_b"], w["bias_p"], w["bias_c"][1],
                   w["alog_p"], w["alog_c"][1], (z, yf, w["dskip_f"], w["ssd_norm"]), b, s, rev=True)

    x = merge_out(attn, ssd, gates, x, w["w_o_attn"], w["w_o_ssd"], w["w_out"], w["norm_mix_post"],
                  cfg["tm_merge"])
    x = ffn(x, w["norm_ffn_pre"], w["w_up_a"], w["w_up_g"], w["cw_a"], w["cw_g"], w["cb_a"], w["cb_g"],
            w["w_down"], w["norm_ffn_post"], s, cfg["tm_ffn"], cfg["tf"])
    return x


def _prepare_weights(norm_mix_pre, norm_mix_post, norm_ffn_pre, norm_ffn_post, w_in, lam_q1, lam_k1,
                     lam_q2, lam_k2, attn_subln, conv_ssd_w, conv_ssd_b, dt_bias, a_log, d_skip, ssd_norm,
                     w_o_attn, w_o_ssd, w_out, w_up, conv_ffn_w, conv_ffn_b, w_down):
    depth = w_in.shape[0]
    qk_cols = ATTN_HEADS * 2 * HEAD_DIM
    attn_w = ATTN_HEADS * V_DIM
    cuts = [0, 2 * qk_cols]
    for width in (attn_w, D_INNER, CONV_DIM, 2 * SSD_HEADS, 2 * D_MODEL):
        cuts.append(cuts[-1] + width)
    seg = lambda i: w_in[:, :, cuts[i]:cuts[i + 1]]
    w_dt = seg(4)
    row = lambda a: a[:, None, :]
    rep = lambda a: jnp.repeat(a, SSD_HEAD_DIM, axis=-1)
    pad_ff = lambda a: jnp.pad(a, [(0, 0)] * (a.ndim - 1) + [(0, D_FF_PAD - D_FF)])
    pad_dt = lambda a: jnp.pad(a.reshape(depth, 1, 2 * SSD_HEADS), ((0, 0), (0, 0), (0, DT_PAD - 2 * SSD_HEADS)))
    lam_init = [0.8 - 0.6 * math.exp(-0.3 * l) for l in range(depth)]
    return dict(
        norm_mix_pre=row(norm_mix_pre), norm_mix_post=row(norm_mix_post),
        norm_ffn_pre=row(norm_ffn_pre), norm_ffn_post=row(norm_ffn_post),
        w_qk=seg(0).astype(BF16),
        w_vt=jnp.swapaxes(seg(1), 1, 2).astype(BF16),
        w_z=seg(2).astype(BF16),
        w_xbc=seg(3).astype(BF16),
        w_dt=jnp.pad(w_dt, ((0, 0), (0, 0), (0, DT_PAD - 2 * SSD_HEADS))).astype(BF16),
        w_dtt=jnp.swapaxes(w_dt, 1, 2).astype(BF16),
        w_gates=seg(5).astype(BF16),
        slopes=jnp.tile(jnp.asarray([2.0 ** (-8.0 * (i + 1) / ATTN_HEADS) for i in range(ATTN_HEADS)],
                                    F32)[None], (depth, 1)),
        lam_consts=jnp.asarray([[li, 1.0 - li] for li in lam_init], F32),
        lam_q1=row(lam_q1), lam_k1=row(lam_k1), lam_q2=row(lam_q2), lam_k2=row(lam_k2),
        subln_col=attn_subln[:, :, None],
        conv_ssd_w=conv_ssd_w, conv_ssd_b=row(conv_ssd_b),
        bias_p=pad_dt(dt_bias), bias_c=dt_bias[..., None],
        alog_p=pad_dt(a_log), alog_c=a_log[..., None],
        dskip_f=row(rep(d_skip)), ssd_norm=row(ssd_norm),
        w_o_attn=w_o_attn.astype(BF16), w_o_ssd=w_o_ssd.astype(BF16), w_out=w_out.astype(BF16),
        w_up_a=pad_ff(w_up[:, :, :D_FF]).astype(BF16), w_up_g=pad_ff(w_up[:, :, D_FF:]).astype(BF16),
        cw_a=pad_ff(conv_ffn_w[:, :, :D_FF]), cw_g=pad_ff(conv_ffn_w[:, :, D_FF:]),
        cb_a=row(pad_ff(conv_ffn_b[:, :D_FF])), cb_g=row(pad_ff(conv_ffn_b[:, D_FF:])),
        w_down=jnp.pad(w_down, ((0, 0), (0, D_FF_PAD - D_FF), (0, 0))).astype(BF16),
    )


def kernel(x_prompt, x_sample, norm_mix_pre, norm_mix_post, norm_ffn_pre, norm_ffn_post, w_in, lam_q1, lam_k1, lam_q2, lam_k2, attn_subln, conv_ssd_w, conv_ssd_b, dt_bias, a_log, d_skip, ssd_norm, w_o_attn, w_o_ssd, w_out, w_up, conv_ffn_w, conv_ffn_b, w_down):
    weights = _prepare_weights(norm_mix_pre, norm_mix_post, norm_ffn_pre, norm_ffn_post, w_in, lam_q1,
                               lam_k1, lam_q2, lam_k2, attn_subln, conv_ssd_w, conv_ssd_b, dt_bias, a_log,
                               d_skip, ssd_norm, w_o_attn, w_o_ssd, w_out, w_up, conv_ffn_w, conv_ffn_b,
                               w_down)
    groups = []
    for xg in (x_prompt, x_sample):
        b, s, d = xg.shape
        groups.append((b, s, _tiles(s)))

    def step(carry, w):
        out = tuple(_layer(x, b, s, w, cfg) for x, (b, s, cfg) in zip(carry, groups))
        return out, None

    init = tuple(xg.reshape(-1, xg.shape[-1]) for xg in (x_prompt, x_sample))
    out, _ = lax.scan(step, init, weights)
    return tuple(o.reshape(xg.shape) for o, xg in zip(out, (x_prompt, x_sample)))
```

```python
import functools
import math

import jax
import jax.numpy as jnp
from jax import lax
from jax.experimental import pallas as pl
from jax.experimental.pallas import tpu as pltpu

F32 = jnp.float32
BF16 = jnp.bfloat16

D_MODEL = 1024
ATTN_HEADS = 8
HEAD_DIM = 64
V_DIM = 2 * HEAD_DIM
SSD_HEADS = 16
SSD_HEAD_DIM = 64
D_INNER = SSD_HEADS * SSD_HEAD_DIM
SSD_GROUPS = 2
GROUP_COLS = D_INNER // SSD_GROUPS
D_STATE = 128
CHUNK = 128
CONV_DIM = D_INNER + 2 * SSD_GROUPS * D_STATE
D_FF = 2752
EPS = 1e-6

LANES = 128
D_FF_PAD = 2816
DT_PAD = LANES
HALO = 16
PROJ_CHUNK = 512
DEPTH = 3
SUM_ROWS = 16
LOG2E = math.log2(math.e)
Q_PRESCALE = LOG2E / math.sqrt(HEAD_DIM)
VMEM_LIMIT = 56 * 1024 * 1024


def _params(sem):
    return pltpu.CompilerParams(dimension_semantics=sem, vmem_limit_bytes=VMEM_LIMIT)


def _rms(x, g):
    ms = jnp.mean(x * x, axis=-1, keepdims=True)
    return x * lax.rsqrt(ms + EPS) * g


def _dot(a, b):
    return jnp.dot(a, b, preferred_element_type=F32)


def _dot_nt(a, b):
    return lax.dot_general(a, b, (((1,), (1,)), ((), ())), preferred_element_type=F32)


def _bf16_terms(x):
    def top(v):
        bits = lax.bitcast_convert_type(v, jnp.uint32) & jnp.uint32(0xFFFF0000)
        return lax.bitcast_convert_type(bits, F32)

    hi = top(x)
    mid = top(x - hi)
    lo = x - hi - mid
    return hi, mid, lo


def _dot_f32_by_01(x, m01):
    return sum(_dot(term.astype(BF16), m01) for term in _bf16_terms(x))


def _dot_01_by_f32(m01, x):
    return sum(_dot(m01, term.astype(BF16)) for term in _bf16_terms(x))


def _softplus(x):
    return jnp.maximum(x, 0.0) + jnp.log1p(jnp.exp(-jnp.abs(x)))


def _silu(x):
    return x * jax.nn.sigmoid(x)


def _in_proj_kernel(x_ref, g_ref, wqk_ref, wvt_ref, wz_ref, wxbc_ref, wg_ref, wdt_ref, wdtt_ref,
                    qk_ref, vt_ref, z_ref, xbc_ref, gates_ref, dt_ref, dtt_ref):
    h = _rms(x_ref[...], g_ref[...]).astype(BF16)
    qk_cols = ATTN_HEADS * 2 * HEAD_DIM

    def project(w_ref, o_ref, scale_upto=0):
        n = w_ref.shape[1]
        for c0 in range(0, n, PROJ_CHUNK):
            c1 = min(c0 + PROJ_CHUNK, n)
            y = _dot(h, w_ref[:, c0:c1])
            if c1 <= scale_upto:
                y = y * Q_PRESCALE
            o_ref[:, c0:c1] = y.astype(o_ref.dtype)

    project(wqk_ref, qk_ref, scale_upto=qk_cols)
    for r0 in range(0, wvt_ref.shape[0], PROJ_CHUNK):
        vt_ref[r0:r0 + PROJ_CHUNK, :] = _dot_nt(wvt_ref[r0:r0 + PROJ_CHUNK, :], h).astype(vt_ref.dtype)
    project(wz_ref, z_ref)
    project(wxbc_ref, xbc_ref)
    project(wg_ref, gates_ref)
    project(wdt_ref, dt_ref)
    dtt_ref[...] = _dot_nt(wdtt_ref[...], h)


def in_proj(x, g, w, tm):
    t, d = x.shape
    weights = [w["w_qk"], w["w_vt"], w["w_z"], w["w_xbc"], w["w_gates"], w["w_dt"], w["w_dtt"]]
    rows = lambda n: pl.BlockSpec((tm, n), lambda i: (i, 0))
    cols = lambda n: pl.BlockSpec((n, tm), lambda i: (0, i))
    resident = lambda a: pl.BlockSpec(a.shape, lambda i: (0, 0), pipeline_mode=pl.Buffered(1))
    n_qk, n_v, n_z, n_xbc, n_g = (w["w_qk"].shape[1], w["w_vt"].shape[0], w["w_z"].shape[1],
                                  w["w_xbc"].shape[1], w["w_gates"].shape[1])
    n_dtt = w["w_dtt"].shape[0]
    return pl.pallas_call(
        _in_proj_kernel,
        grid=(t // tm,),
        in_specs=[rows(d), resident(g)] + [resident(a) for a in weights],
        out_specs=[rows(n_qk), cols(n_v), rows(n_z), rows(n_xbc), rows(n_g), rows(DT_PAD), cols(n_dtt)],
        out_shape=[
            jax.ShapeDtypeStruct((t, n_qk), BF16), jax.ShapeDtypeStruct((n_v, t), BF16),
            jax.ShapeDtypeStruct((t, n_z), F32), jax.ShapeDtypeStruct((t, n_xbc), F32),
            jax.ShapeDtypeStruct((t, n_g), F32), jax.ShapeDtypeStruct((t, DT_PAD), F32),
            jax.ShapeDtypeStruct((n_dtt, t), F32),
        ],
        compiler_params=_params(("parallel",)),
        name="in_proj",
    )(x, g, *weights)


def _attn_kernel(slopes_ref, lam_ref, q_ref, k_ref, vt_ref, lq1_ref, lk1_ref, lq2_ref, lk2_ref,
                 subln_ref, o_ref, kfeat_ref, s0_ref, s1_ref, s2_ref, p0_ref, p1_ref, p2_ref, *, tq, tk, nk):
    h = pl.program_id(1)
    qi = pl.program_id(2)
    slope2 = slopes_ref[h] * LOG2E
    q0pos = qi * tq
    n_off = nk - 1
    kd = lax.shift_right_logical(qi, int(math.log2(tk // tq)))

    q = q_ref[...]
    lane = lax.broadcasted_iota(jnp.int32, q.shape, 1)
    zero = jnp.zeros_like(q)
    ii = lax.broadcasted_iota(jnp.int32, (tq, LANES), 0).astype(F32)
    fq = lax.broadcasted_iota(jnp.int32, (tq, LANES), 1)
    row_terms = _bf16_terms(-slope2 * ii)
    slope_terms = _bf16_terms(jnp.full((tq, LANES), slope2, F32))
    q_feat = jnp.zeros((tq, LANES), F32)
    for n in range(3):
        q_feat = jnp.where(fq == n, row_terms[n], q_feat)
        q_feat = jnp.where((fq == 3 + n) | (fq == 6 + n), slope_terms[n], q_feat)
    q_feat = q_feat.astype(BF16)
    q_ops = jnp.concatenate([
        jnp.concatenate([jnp.where(lane < HEAD_DIM, q, zero), q_feat], axis=1),
        jnp.concatenate([jnp.where(lane >= HEAD_DIM, q, zero), q_feat], axis=1)], axis=0)

    jj = lax.broadcasted_iota(jnp.int32, (tk, LANES), 0)
    fk = lax.broadcasted_iota(jnp.int32, (tk, LANES), 1)
    jj_lo = jnp.bitwise_and(jj, 255)
    jj_hi = (jj - jj_lo).astype(F32)
    jj_lo = jj_lo.astype(F32)
    k_feat = jnp.where(fk < 3, 1.0, jnp.where(fk < 6, jj_lo, jnp.where(fk < 9, jj_hi, 0.0)))
    kfeat_ref[0] = k_feat.astype(BF16)
    kfeat_ref[1] = (-k_feat).astype(BF16)
    ones_rows = jnp.ones((SUM_ROWS, tk), BF16)

    def raw_scores(kb, side):
        start = pl.multiple_of(kb * tk, tk)
        k_ops = jnp.concatenate([k_ref[pl.ds(start, tk), :], kfeat_ref[side]], axis=1)
        return _dot_nt(k_ops, q_ops)

    def offset(kb):
        return slope2 * jnp.abs(q0pos - kb * tk).astype(F32)

    def off_tile(t):
        side = (t >= kd).astype(jnp.int32)
        return t + side, side

    def produce(t, dst_ref):
        kb, side = off_tile(t)
        s = raw_scores(kb, side)
        dst_ref[...] = s
        return jnp.max(s, axis=0, keepdims=True)

    def softmax_stage(s, mx, c, m_old, p_ref):
        m_new = jnp.maximum(m_old, mx - c)
        p_ref[...] = jnp.exp2(s - (m_new + c)).astype(BF16)
        return m_new, jnp.exp2(m_old - m_new)

    def pv_stage(kb, p_ref, alpha, acc):
        vt = vt_ref[:, pl.ds(pl.multiple_of(kb * tk, tk), tk)]
        vt_ops = jnp.concatenate([vt, ones_rows], axis=0)
        return alpha * acc + _dot(vt_ops, p_ref[...])

    def prev_tile(t):
        return jnp.where(t < DEPTH, kd, off_tile(t - DEPTH)[0])

    s_refs = (s0_ref, s1_ref)

    def update(s, mx, c, kb, state):
        m_old, acc_old = state
        m_new = jnp.maximum(m_old, mx - c)
        p = jnp.exp2(s - (m_new + c))
        alpha = jnp.exp2(m_old - m_new)
        vt = vt_ref[:, pl.ds(pl.multiple_of(kb * tk, tk), tk)]
        vt_ops = jnp.concatenate([vt, ones_rows], axis=0)
        return m_new, alpha * acc_old + _dot(vt_ops, p.astype(BF16))

    def consume(t, src_ref, mx, st):
        kb, _ = off_tile(t)
        return update(src_ref[...], mx, offset(kb), kb, st)

    state = (jnp.full((1, 2 * tq), -jnp.inf, F32), jnp.zeros((V_DIM + SUM_ROWS, 2 * tq), F32))
    mx_a = produce(jnp.int32(0), s_refs[0])
    c_d = offset(kd)
    s_d = jnp.minimum(raw_scores(kd, 0) - c_d, raw_scores(kd, 1) + c_d)
    state = update(s_d, jnp.max(s_d, axis=0, keepdims=True), 0.0, kd, state)

    def body(u, carry):
        st, mx_even = carry[:2], carry[2]
        mx_odd = produce(2 * u + 1, s_refs[1])
        st = consume(2 * u, s_refs[0], mx_even, st)
        mx_even = produce(2 * u + 2, s_refs[0])
        st = consume(2 * u + 1, s_refs[1], mx_odd, st)
        return st + (mx_even,)

    carry = lax.fori_loop(0, (n_off - 1) // 2, body, state + (mx_a,))
    _, acc_fin = consume(jnp.int32(n_off - 1), s_refs[0], carry[2], carry[:2])

    lam_init = lam_ref[0]
    one_minus = lam_ref[1]
    lam = (jnp.exp(jnp.sum(lq1_ref[...] * lk1_ref[...], axis=-1, keepdims=True))
           - jnp.exp(jnp.sum(lq2_ref[...] * lk2_ref[...], axis=-1, keepdims=True)) + lam_init)
    o_both = acc_fin[:V_DIM, :] / acc_fin[V_DIM:V_DIM + 1, :]
    o = o_both[:, :tq] - lam * o_both[:, tq:]
    ms = jnp.mean(o * o, axis=0, keepdims=True)
    y = o * lax.rsqrt(ms + EPS) * subln_ref[...] * one_minus
    o_ref[...] = y.T.astype(o_ref.dtype)


def diff_attention(qk, vt, slopes, lam_consts, lq1, lk1, lq2, lk2, subln_col, b, s, tq, tk):
    t = b * s
    nq = s // tq
    nk = s // tk
    assert nk == 1 or (nk - 2) % DEPTH == 0, (s, tk)
    assert tk % tq == 0 and tq <= 256
    hh = ATTN_HEADS
    smem = pl.BlockSpec(memory_space=pltpu.SMEM)
    vec = pl.BlockSpec((1, HEAD_DIM), lambda bi, h, qi: (0, 0))
    return pl.pallas_call(
        functools.partial(_attn_kernel, tq=tq, tk=tk, nk=nk),
        grid=(b, hh, nq),
        in_specs=[
            smem, smem,
            pl.BlockSpec((tq, V_DIM), lambda bi, h, qi: (bi * nq + qi, h)),
            pl.BlockSpec((s, V_DIM), lambda bi, h, qi: (bi, hh + h)),
            pl.BlockSpec((V_DIM, s), lambda bi, h, qi: (h, bi)),
            vec, vec, vec, vec,
            pl.BlockSpec((V_DIM, 1), lambda bi, h, qi: (0, 0)),
        ],
        out_specs=pl.BlockSpec((tq, V_DIM), lambda bi, h, qi: (bi * nq + qi, h)),
        out_shape=jax.ShapeDtypeStruct((t, hh * V_DIM), BF16),
        scratch_shapes=([pltpu.VMEM((2, tk, LANES), BF16)]
                        + [pltpu.VMEM((tk, 2 * tq), F32)] * DEPTH
                        + [pltpu.VMEM((tk, 2 * tq), BF16)] * DEPTH),
        compiler_params=_params(("parallel", "parallel", "arbitrary")),
        name="diff_attention",
    )(slopes, lam_consts, qk, qk, vt, lq1, lk1, lq2, lk2, subln_col)


def _ssd_kernel(*refs, rev, final, nc):
    if final:
        (xbc_ref, prev_ref, next_ref, dt_ref, dtt_ref, cw_ref, cb_ref, bias_p_ref, bias_c_ref,
         alog_p_ref, alog_c_ref, z_ref, yf_ref, dskip_ref, nw_ref, y_ref, st_ref) = refs
    else:
        (xbc_ref, prev_ref, next_ref, dt_ref, dtt_ref, cw_ref, cb_ref, bias_p_ref, bias_c_ref,
         alog_p_ref, alog_c_ref, y_ref, st_ref) = refs
    direction = 1 if rev else 0
    c = pl.program_id(1)
    cc = (nc - 1 - c) if rev else c
    ll = CHUNK

    @pl.when(c == 0)
    def _():
        st_ref[...] = jnp.zeros_like(st_ref)

    x = xbc_ref[...]
    before = jnp.where(cc == 0, 0.0, prev_ref[...])
    after = jnp.where(cc == nc - 1, 0.0, next_ref[...])
    x_ext = jnp.concatenate([x, after, before], axis=0)
    n_ext = x_ext.shape[0]
    xp = pltpu.roll(x_ext, 1, axis=0)[0:ll, :]
    xn = pltpu.roll(x_ext, n_ext - 1, axis=0)[0:ll, :]
    cw = cw_ref[...]
    u = _silu(cb_ref[...] + xp * cw[0:1, :] + x * cw[1:2, :] + xn * cw[2:3, :])
    xs = u[:, :D_INNER]
    bm = u[:, D_INNER:D_INNER + SSD_GROUPS * D_STATE]
    cm = u[:, D_INNER + SSD_GROUPS * D_STATE:]

    jj = lax.broadcasted_iota(jnp.int32, (DT_PAD, D_INNER), 0)
    col = lax.broadcasted_iota(jnp.int32, (DT_PAD, D_INNER), 1)
    head_of_col = lax.shift_right_logical(col, int(math.log2(SSD_HEAD_DIM)))
    expand = jnp.where(jj == direction * SSD_HEADS + head_of_col, 1.0, 0.0).astype(BF16)
    dt_cols = _softplus(dt_ref[...] + bias_p_ref[...])
    a_cols = dt_cols * (-jnp.exp(alog_p_ref[...]))
    dt_rows = _softplus(dtt_ref[direction * SSD_HEADS:(direction + 1) * SSD_HEADS, :] + bias_c_ref[...])
    a_rows = dt_rows * (-jnp.exp(alog_c_ref[...]))

    ri = lax.broadcasted_iota(jnp.int32, (ll, ll), 0)
    ci = lax.broadcasted_iota(jnp.int32, (ll, ll), 1)
    if rev:
        keep = ci >= ri
        edge = 0
    else:
        keep = ci <= ri
        edge = ll - 1
    tri = jnp.where(keep, 1.0, 0.0).astype(BF16)
    tri_t = jnp.where((ri >= ci) if rev else (ri <= ci), 1.0, 0.0).astype(BF16)
    cum_cols = _dot_01_by_f32(tri, a_cols)
    dt_full = _dot_f32_by_01(dt_cols, expand)
    cum_full = _dot_f32_by_01(cum_cols, expand)
    cum_rows = _dot_f32_by_01(a_rows, tri_t)

    xd = xs * dt_full
    xd_b = xd.astype(BF16)
    cum_edge = cum_full[edge:edge + 1, :]
    xdw = (xd * jnp.exp(cum_edge - cum_full)).astype(BF16)
    grow = jnp.exp(cum_full)
    lane = lax.broadcasted_iota(jnp.int32, (ll, LANES), 1)

    y_parts = []
    for g in range(SSD_GROUPS):
        bg = bm[:, g * D_STATE:(g + 1) * D_STATE]
        cg = cm[:, g * D_STATE:(g + 1) * D_STATE].astype(BF16)
        cb = _dot_nt(cg, bg.astype(BF16))
        gs = slice(g * GROUP_COLS, (g + 1) * GROUP_COLS)
        st_in = st_ref[:, gs]
        y_off = _dot(cg, st_in.astype(BF16)) * grow[:, gs]
        heads_per_group = SSD_HEADS // SSD_GROUPS
        for pair in range(heads_per_group // 2):
            lo = g * GROUP_COLS + pair * LANES
            xd_pair = xd_b[:, lo:lo + LANES]
            outs = []
            for sub in range(2):
                hd = g * heads_per_group + pair * 2 + sub
                seg = cum_full[:, hd * SSD_HEAD_DIM:hd * SSD_HEAD_DIM + 1] - cum_rows[hd:hd + 1, :]
                dec = jnp.exp(jnp.where(keep, seg, -jnp.inf))
                outs.append(_dot((cb * dec).astype(BF16), xd_pair))
            y_diag = jnp.where(lane < SSD_HEAD_DIM, outs[0], outs[1])
            y_parts.append(y_diag + y_off[:, pair * LANES:(pair + 1) * LANES])
        st_chunk = _dot(bg.T.astype(BF16), xdw[:, gs])
        st_ref[:, gs] = st_in * jnp.exp(cum_edge[:, gs]) + st_chunk
    y = jnp.concatenate(y_parts, axis=1)

    if final:
        y = yf_ref[...] + y + xs * dskip_ref[...]
        y = y * _silu(z_ref[...])
        nw = nw_ref[...]
        normed = []
        for g in range(SSD_GROUPS):
            gs = slice(g * GROUP_COLS, (g + 1) * GROUP_COLS)
            normed.append(_rms(y[:, gs], nw[:, gs]))
        y = jnp.concatenate(normed, axis=1)
    y_ref[...] = y.astype(y_ref.dtype)


def ssd_pass(xbc, dt, dtt, cw, cb, bias_p, bias_c, alog_p, alog_c, extras, b, s, rev):
    final = extras is not None
    t = b * s
    nc = s // CHUNK
    rows8 = CHUNK // 8
    last8 = t // 8 - 1

    def cidx(bi, c):
        return bi * nc + ((nc - 1 - c) if rev else c)

    def full(shape):
        return pl.BlockSpec(shape, lambda bi, c: (0, 0))

    chunk_rows = lambda w: pl.BlockSpec((CHUNK, w), lambda bi, c: (cidx(bi, c), 0))
    in_specs = [
        chunk_rows(CONV_DIM),
        pl.BlockSpec((8, CONV_DIM), lambda bi, c: (jnp.maximum(cidx(bi, c) * rows8 - 1, 0), 0)),
        pl.BlockSpec((8, CONV_DIM), lambda bi, c: (jnp.minimum((cidx(bi, c) + 1) * rows8, last8), 0)),
        chunk_rows(DT_PAD),
        pl.BlockSpec((2 * SSD_HEADS, CHUNK), lambda bi, c: (0, cidx(bi, c))),
        full((3, CONV_DIM)), full((1, CONV_DIM)),
        full((1, DT_PAD)), full((SSD_HEADS, 1)), full((1, DT_PAD)), full((SSD_HEADS, 1)),
    ]
    args = [xbc, xbc, xbc, dt, dtt, cw, cb, bias_p, bias_c, alog_p, alog_c]
    if final:
        z, yf, dskip_f, nw = extras
        in_specs += [chunk_rows(D_INNER), chunk_rows(D_INNER), full((1, D_INNER)), full((1, D_INNER))]
        args += [z, yf, dskip_f, nw]
    return pl.pallas_call(
        functools.partial(_ssd_kernel, rev=rev, final=final, nc=nc),
        grid=(b, nc),
        in_specs=in_specs,
        out_specs=chunk_rows(D_INNER),
        out_shape=jax.ShapeDtypeStruct((t, D_INNER), BF16 if final else F32),
        scratch_shapes=[pltpu.VMEM((D_STATE, D_INNER), F32)],
        compiler_params=_params(("parallel", "arbitrary")),
        name="ssd_bwd_final" if final else "ssd_fwd",
    )(*args)


def _merge_kernel(attn_ref, ssd_ref, gates_ref, x_ref, woa_ref, wos_ref, wout_ref, nw_ref, o_ref):
    a = _dot(attn_ref[...], woa_ref[...])
    s = _dot(ssd_ref[...], wos_ref[...])
    gates = gates_ref[...]
    merged = jax.nn.sigmoid(gates[:, :D_MODEL]) * a + jax.nn.sigmoid(gates[:, D_MODEL:]) * s
    mo = _dot(merged.astype(BF16), wout_ref[...])
    o_ref[...] = x_ref[...] + _rms(mo, nw_ref[...])


def merge_out(attn, ssd, gates, x, woa, wos, wout, nw, tm):
    t, d = x.shape
    rows = lambda w: pl.BlockSpec((tm, w), lambda i: (i, 0))
    full = lambda shape: pl.BlockSpec(shape, lambda i: (0, 0))
    return pl.pallas_call(
        _merge_kernel,
        grid=(t // tm,),
        in_specs=[rows(d), rows(d), rows(2 * d), rows(d), full((d, d)), full((d, d)), full((d, d)),
                  full((1, d))],
        out_specs=rows(d),
        out_shape=jax.ShapeDtypeStruct((t, d), F32),
        compiler_params=_params(("parallel",)),
        name="merge_out",
    )(attn, ssd, gates, x, woa, wos, wout, nw)


def _ffn_kernel(x_ref, xp_ref, xn_ref, gpre_ref, wa_ref, wg_ref, cwa_ref, cwg_ref, cba_ref, cbg_ref,
                wd_ref, gpost_ref, o_ref, h_ref, acc_ref, *, tm, tiles_per_seq, nf):
    i = pl.program_id(0)
    f = pl.program_id(1)

    @pl.when(f == 0)
    def _():
        gpre = gpre_ref[...]
        pos = i % tiles_per_seq
        h_ref[0:tm, :] = _rms(x_ref[...], gpre).astype(BF16)
        hp = jnp.where(pos == 0, 0.0, _rms(xp_ref[...], gpre))
        hn = jnp.where(pos == tiles_per_seq - 1, 0.0, _rms(xn_ref[...], gpre))
        h_ref[tm:tm + HALO, :] = hn.astype(BF16)
        h_ref[tm + HALO:tm + 2 * HALO, :] = hp.astype(BF16)
        acc_ref[...] = jnp.zeros_like(acc_ref)

    h = h_ref[...]
    n_ext = tm + 2 * HALO

    def conv_branch(w_ref, cw_ref, cb_ref):
        u = _dot(h, w_ref[...])
        um = u[0:tm, :]
        up = pltpu.roll(u, 1, axis=0)[0:tm, :]
        un = pltpu.roll(u, n_ext - 1, axis=0)[0:tm, :]
        cw = cw_ref[...]
        return cb_ref[...] + up * cw[0:1, :] + um * cw[1:2, :] + un * cw[2:3, :]

    a = conv_branch(wa_ref, cwa_ref, cba_ref)
    g = conv_branch(wg_ref, cwg_ref, cbg_ref)
    act = (_silu(g) * a).astype(BF16)
    acc_ref[...] += _dot(act, wd_ref[...])

    @pl.when(f == nf - 1)
    def _():
        o_ref[...] = x_ref[...] + _rms(acc_ref[...], gpost_ref[...])


def ffn(x, gpre, wa, wg, cwa, cwg, cba, cbg, wd, gpost, s, tm, tf):
    t, d = x.shape
    nf = D_FF_PAD // tf
    tiles_per_seq = s // tm
    blocks = tm // HALO
    last = t // HALO - 1
    full = lambda shape: pl.BlockSpec(shape, lambda i, f: (0, 0))
    colblk = lambda r: pl.BlockSpec((r, tf), lambda i, f: (0, f))
    return pl.pallas_call(
        functools.partial(_ffn_kernel, tm=tm, tiles_per_seq=tiles_per_seq, nf=nf),
        grid=(t // tm, nf),
        in_specs=[
            pl.BlockSpec((tm, d), lambda i, f: (i, 0)),
            pl.BlockSpec((HALO, d), lambda i, f: (jnp.maximum(i * blocks - 1, 0), 0)),
            pl.BlockSpec((HALO, d), lambda i, f: (jnp.minimum((i + 1) * blocks, last), 0)),
            full((1, d)),
            colblk(d), colblk(d), colblk(3), colblk(3), colblk(1), colblk(1),
            pl.BlockSpec((tf, d), lambda i, f: (f, 0)),
            full((1, d)),
        ],
        out_specs=pl.BlockSpec((tm, d), lambda i, f: (i, 0)),
        out_shape=jax.ShapeDtypeStruct((t, d), F32),
        scratch_shapes=[pltpu.VMEM((tm + 2 * HALO, d), BF16), pltpu.VMEM((tm, d), F32)],
        compiler_params=_params(("parallel", "arbitrary")),
        name="ffn",
    )(x, x, x, gpre, wa, wg, cwa, cwg, cba, cbg, wd, gpost)


def _tiles(s):
    return dict(
        tm_proj=min(256, s),
        tq=min(256, s), tk=min(512, s),
        tm_merge=min(256, s),
        tm_ffn=min(1024, s), tf=256,
    )


def _layer(x, b, s, w, cfg):
    g_pre = w["norm_mix_pre"]
    qk, vt, z, xbc, gates, dt, dtt = in_proj(x, g_pre, w, cfg["tm_proj"])

    attn = diff_attention(qk, vt, w["slopes"], w["lam_consts"], w["lam_q1"], w["lam_k1"], w["lam_q2"],
                          w["lam_k2"], w["subln_col"], b, s, cfg["tq"], cfg["tk"])

    yf = ssd_pass(xbc, dt, dtt, w["conv_ssd_w"], w["conv_ssd_b"], w["bias_p"], w["bias_c"][0],
                  w["alog_p"], w["alog_c"][0], None, b, s, rev=False)
    ssd = ssd_pass(xbc, dt, dtt, w["conv_ssd_w"], w["conv_ssd_b"], w["bias_p"], w["bias_c"][1],
                   w["alog_p"], w["alog_c"][1], (z, yf, w["dskip_f"], w["ssd_norm"]), b, s, rev=True)

    x = merge_out(attn, ssd, gates, x, w["w_o_attn"], w["w_o_ssd"], w["w_out"], w["norm_mix_post"],
                  cfg["tm_merge"])
    x = ffn(x, w["norm_ffn_pre"], w["w_up_a"], w["w_up_g"], w["cw_a"], w["cw_g"], w["cb_a"], w["cb_g"],
            w["w_down"], w["norm_ffn_post"], s, cfg["tm_ffn"], cfg["tf"])
    return x


def _prepare_weights(norm_mix_pre, norm_mix_post, norm_ffn_pre, norm_ffn_post, w_in, lam_q1, lam_k1,
                     lam_q2, lam_k2, attn_subln, conv_ssd_w, conv_ssd_b, dt_bias, a_log, d_skip, ssd_norm,
                     w_o_attn, w_o_ssd, w_out, w_up, conv_ffn_w, conv_ffn_b, w_down):
    depth = w_in.shape[0]
    qk_cols = ATTN_HEADS * 2 * HEAD_DIM
    attn_w = ATTN_HEADS * V_DIM
    cuts = [0, 2 * qk_cols]
    for width in (attn_w, D_INNER, CONV_DIM, 2 * SSD_HEADS, 2 * D_MODEL):
        cuts.append(cuts[-1] + width)
    seg = lambda i: w_in[:, :, cuts[i]:cuts[i + 1]]
    w_dt = seg(4)
    row = lambda a: a[:, None, :]
    rep = lambda a: jnp.repeat(a, SSD_HEAD_DIM, axis=-1)
    pad_ff = lambda a: jnp.pad(a, [(0, 0)] * (a.ndim - 1) + [(0, D_FF_PAD - D_FF)])
    pad_dt = lambda a: jnp.pad(a.reshape(depth, 1, 2 * SSD_HEADS), ((0, 0), (0, 0), (0, DT_PAD - 2 * SSD_HEADS)))
    lam_init = [0.8 - 0.6 * math.exp(-0.3 * l) for l in range(depth)]
    return dict(
        norm_mix_pre=row(norm_mix_pre), norm_mix_post=row(norm_mix_post),
        norm_ffn_pre=row(norm_ffn_pre), norm_ffn_post=row(norm_ffn_post),
        w_qk=seg(0).astype(BF16),
        w_vt=jnp.swapaxes(seg(1), 1, 2).astype(BF16),
        w_z=seg(2).astype(BF16),
        w_xbc=seg(3).astype(BF16),
        w_dt=jnp.pad(w_dt, ((0, 0), (0, 0), (0, DT_PAD - 2 * SSD_HEADS))).astype(BF16),
        w_dtt=jnp.swapaxes(w_dt, 1, 2).astype(BF16),
        w_gates=seg(5).astype(BF16),
        slopes=jnp.tile(jnp.asarray([2.0 ** (-8.0 * (i + 1) / ATTN_HEADS) for i in range(ATTN_HEADS)],
                                    F32)[None], (depth, 1)),
        lam_consts=jnp.asarray([[li, 1.0 - li] for li in lam_init], F32),
        lam_q1=row(lam_q1), lam_k1=row(lam_k1), lam_q2=row(lam_q2), lam_k2=row(lam_k2),
        subln_col=attn_subln[:, :, None],
        conv_ssd_w=conv_ssd_w, conv_ssd_b=row(conv_ssd_b),
        bias_p=pad_dt(dt_bias), bias_c=dt_bias[..., None],
        alog_p=pad_dt(a_log), alog_c=a_log[..., None],
        dskip_f=row(rep(d_skip)), ssd_norm=row(ssd_norm),
        w_o_attn=w_o_attn.astype(BF16), w_o_ssd=w_o_ssd.astype(BF16), w_out=w_out.astype(BF16),
        w_up_a=pad_ff(w_up[:, :, :D_FF]).astype(BF16), w_up_g=pad_ff(w_up[:, :, D_FF:]).astype(BF16),
        cw_a=pad_ff(conv_ffn_w[:, :, :D_FF]), cw_g=pad_ff(conv_ffn_w[:, :, D_FF:]),
        cb_a=row(pad_ff(conv_ffn_b[:, :D_FF])), cb_g=row(pad_ff(conv_ffn_b[:, D_FF:])),
        w_down=jnp.pad(w_down, ((0, 0), (0, D_FF_PAD - D_FF), (0, 0))).astype(BF16),
    )


def kernel(x_prompt, x_sample, norm_mix_pre, norm_mix_post, norm_ffn_pre, norm_ffn_post, w_in, lam_q1, lam_k1, lam_q2, lam_k2, attn_subln, conv_ssd_w, conv_ssd_b, dt_bias, a_log, d_skip, ssd_norm, w_o_attn, w_o_ssd, w_out, w_up, conv_ffn_w, conv_ffn_b, w_down):
    weights = _prepare_weights(norm_mix_pre, norm_mix_post, norm_ffn_pre, norm_ffn_post, w_in, lam_q1,
                               lam_k1, lam_q2, lam_k2, attn_subln, conv_ssd_w, conv_ssd_b, dt_bias, a_log,
                               d_skip, ssd_norm, w_o_attn, w_o_ssd, w_out, w_up, conv_ffn_w, conv_ffn_b,
                               w_down)
    groups = []
    for xg in (x_prompt, x_sample):
        b, s, d = xg.shape
        groups.append((b, s, _tiles(s)))

    def step(carry, w):
        out = tuple(_layer(x, b, s, w, cfg) for x, (b, s, cfg) in zip(carry, groups))
        return out, None

    init = tuple(xg.reshape(-1, xg.shape[-1]) for xg in (x_prompt, x_sample))
    out, _ = lax.scan(step, init, weights)
    return tuple(o.reshape(xg.shape) for o, xg in zip(out, (x_prompt, x_sample)))
```

```python
import functools
import math

import jax
import jax.numpy as jnp
from jax import lax
from jax.experimental import pallas as pl
from jax.experimental.pallas import tpu as pltpu

F32 = jnp.float32
BF16 = jnp.bfloat16

D_MODEL = 1024
ATTN_HEADS = 8
HEAD_DIM = 64
V_DIM = 2 * HEAD_DIM
SSD_HEADS = 16
SSD_HEAD_DIM = 64
D_INNER = SSD_HEADS * SSD_HEAD_DIM
SSD_GROUPS = 2
GROUP_COLS = D_INNER // SSD_GROUPS
D_STATE = 128
CHUNK = 128
CONV_DIM = D_INNER + 2 * SSD_GROUPS * D_STATE
D_FF = 2752
EPS = 1e-6

LANES = 128
D_FF_PAD = 2816
DT_PAD = LANES
HALO = 16
PROJ_CHUNK = 512
DEPTH = 3
SUM_ROWS = 16
LOG2E = math.log2(math.e)
Q_PRESCALE = LOG2E / math.sqrt(HEAD_DIM)
VMEM_LIMIT = 56 * 1024 * 1024


def _params(sem):
    return pltpu.CompilerParams(dimension_semantics=sem, vmem_limit_bytes=VMEM_LIMIT)


def _rms(x, g):
    ms = jnp.mean(x * x, axis=-1, keepdims=True)
    return x * lax.rsqrt(ms + EPS) * g


def _dot(a, b):
    return jnp.dot(a, b, preferred_element_type=F32)


def _dot_nt(a, b):
    return lax.dot_general(a, b, (((1,), (1,)), ((), ())), preferred_element_type=F32)


def _bf16_terms(x):
    def top(v):
        bits = lax.bitcast_convert_type(v, jnp.uint32) & jnp.uint32(0xFFFF0000)
        return lax.bitcast_convert_type(bits, F32)

    hi = top(x)
    mid = top(x - hi)
    lo = x - hi - mid
    return hi, mid, lo


def _dot_f32_by_01(x, m01):
    return sum(_dot(term.astype(BF16), m01) for term in _bf16_terms(x))


def _dot_01_by_f32(m01, x):
    return sum(_dot(m01, term.astype(BF16)) for term in _bf16_terms(x))


def _softplus(x):
    return jnp.maximum(x, 0.0) + jnp.log1p(jnp.exp(-jnp.abs(x)))


def _silu(x):
    return x * jax.nn.sigmoid(x)


def _in_proj_kernel(x_ref, g_ref, wqk_ref, wvt_ref, wz_ref, wxbc_ref, wg_ref, wdt_ref, wdtt_ref,
                    qk_ref, vt_ref, z_ref, xbc_ref, gates_ref, dt_ref, dtt_ref):
    h = _rms(x_ref[...], g_ref[...]).astype(BF16)
    qk_cols = ATTN_HEADS * 2 * HEAD_DIM

    def project(w_ref, o_ref, scale_upto=0):
        n = w_ref.shape[1]
        for c0 in range(0, n, PROJ_CHUNK):
            c1 = min(c0 + PROJ_CHUNK, n)
            y = _dot(h, w_ref[:, c0:c1])
            if c1 <= scale_upto:
                y = y * Q_PRESCALE
            o_ref[:, c0:c1] = y.astype(o_ref.dtype)

    project(wqk_ref, qk_ref, scale_upto=qk_cols)
    for r0 in range(0, wvt_ref.shape[0], PROJ_CHUNK):
        vt_ref[r0:r0 + PROJ_CHUNK, :] = _dot_nt(wvt_ref[r0:r0 + PROJ_CHUNK, :], h).astype(vt_ref.dtype)
    project(wz_ref, z_ref)
    project(wxbc_ref, xbc_ref)
    project(wg_ref, gates_ref)
    project(wdt_ref, dt_ref)
    dtt_ref[...] = _dot_nt(wdtt_ref[...], h)


def in_proj(x, g, w, tm):
    t, d = x.shape
    weights = [w["w_qk"], w["w_vt"], w["w_z"], w["w_xbc"], w["w_gates"], w["w_dt"], w["w_dtt"]]
    rows = lambda n: pl.BlockSpec((tm, n), lambda i: (i, 0))
    cols = lambda n: pl.BlockSpec((n, tm), lambda i: (0, i))
    resident = lambda a: pl.BlockSpec(a.shape, lambda i: (0, 0), pipeline_mode=pl.Buffered(1))
    n_qk, n_v, n_z, n_xbc, n_g = (w["w_qk"].shape[1], w["w_vt"].shape[0], w["w_z"].shape[1],
                                  w["w_xbc"].shape[1], w["w_gates"].shape[1])
    n_dtt = w["w_dtt"].shape[0]
    return pl.pallas_call(
        _in_proj_kernel,
        grid=(t // tm,),
        in_specs=[rows(d), resident(g)] + [resident(a) for a in weights],
        out_specs=[rows(n_qk), cols(n_v), rows(n_z), rows(n_xbc), rows(n_g), rows(DT_PAD), cols(n_dtt)],
        out_shape=[
            jax.ShapeDtypeStruct((t, n_qk), BF16), jax.ShapeDtypeStruct((n_v, t), BF16),
            jax.ShapeDtypeStruct((t, n_z), F32), jax.ShapeDtypeStruct((t, n_xbc), F32),
            jax.ShapeDtypeStruct((t, n_g), F32), jax.ShapeDtypeStruct((t, DT_PAD), F32),
            jax.ShapeDtypeStruct((n_dtt, t), F32),
        ],
        compiler_params=_params(("parallel",)),
        name="in_proj",
    )(x, g, *weights)


def _attn_kernel(slopes_ref, lam_ref, q_ref, k_ref, vt_ref, lq1_ref, lk1_ref, lq2_ref, lk2_ref,
                 subln_ref, o_ref, kfeat_ref, s0_ref, s1_ref, s2_ref, *, tq, tk, nk):
    h = pl.program_id(1)
    qi = pl.program_id(2)
    slope2 = slopes_ref[h] * LOG2E
    q0pos = qi * tq
    n_off = nk - 1
    kd = lax.shift_right_logical(qi, int(math.log2(tk // tq)))

    q = q_ref[...]
    lane = lax.broadcasted_iota(jnp.int32, q.shape, 1)
    zero = jnp.zeros_like(q)
    ii = lax.broadcasted_iota(jnp.int32, (tq, LANES), 0).astype(F32)
    fq = lax.broadcasted_iota(jnp.int32, (tq, LANES), 1)
    row_terms = _bf16_terms(-slope2 * ii)
    slope_terms = _bf16_terms(jnp.full((tq, LANES), slope2, F32))
    q_feat = jnp.zeros((tq, LANES), F32)
    for n in range(3):
        q_feat = jnp.where(fq == n, row_terms[n], q_feat)
        q_feat = jnp.where((fq == 3 + n) | (fq == 6 + n), slope_terms[n], q_feat)
    q_feat = q_feat.astype(BF16)
    q_ops = jnp.concatenate([
        jnp.concatenate([jnp.where(lane < HEAD_DIM, q, zero), q_feat], axis=1),
        jnp.concatenate([jnp.where(lane >= HEAD_DIM, q, zero), q_feat], axis=1)], axis=0)

    jj = lax.broadcasted_iota(jnp.int32, (tk, LANES), 0)
    fk = lax.broadcasted_iota(jnp.int32, (tk, LANES), 1)
    jj_lo = jnp.bitwise_and(jj, 255)
    jj_hi = (jj - jj_lo).astype(F32)
    jj_lo = jj_lo.astype(F32)
    k_feat = jnp.where(fk < 3, 1.0, jnp.where(fk < 6, jj_lo, jnp.where(fk < 9, jj_hi, 0.0)))
    kfeat_ref[0] = k_feat.astype(BF16)
    kfeat_ref[1] = (-k_feat).astype(BF16)
    ones_rows = jnp.ones((SUM_ROWS, tk), BF16)

    def raw_scores(kb, side):
        start = pl.multiple_of(kb * tk, tk)
        k_ops = jnp.concatenate([k_ref[pl.ds(start, tk), :], kfeat_ref[side]], axis=1)
        return _dot_nt(k_ops, q_ops)

    def offset(kb):
        return slope2 * jnp.abs(q0pos - kb * tk).astype(F32)

    def off_tile(t):
        side = (t >= kd).astype(jnp.int32)
        return t + side, side

    def produce(t, dst_ref):
        kb, side = off_tile(t)
        s = raw_scores(kb, side)
        dst_ref[...] = s
        return jnp.max(s, axis=0, keepdims=True)

    def update(s, mx, c, kb, state):
        m_old, acc_old = state
        m_new = jnp.maximum(m_old, mx - c)
        p = jnp.exp2(s - (m_new + c))
        alpha = jnp.exp2(m_old - m_new)
        vt = vt_ref[:, pl.ds(pl.multiple_of(kb * tk, tk), tk)]
        vt_ops = jnp.concatenate([vt, ones_rows], axis=0)
        return m_new, alpha * acc_old + _dot(vt_ops, p.astype(BF16))

    def consume(t, src_ref, mx, state):
        kb, _ = off_tile(t)
        return update(src_ref[...], mx, offset(kb), kb, state)

    trips = (n_off - 1) // DEPTH
    s_refs = (s0_ref, s1_ref, s2_ref)
    first = min(DEPTH, n_off)
    mx = [produce(jnp.int32(i), s_refs[i]) for i in range(first)]
    mx += [jnp.zeros((1, 2 * tq), F32)] * (DEPTH - first)

    c_d = offset(kd)
    s_d = jnp.minimum(raw_scores(kd, 0) - c_d, raw_scores(kd, 1) + c_d)
    state = (jnp.full((1, 2 * tq), -jnp.inf, F32), jnp.zeros((V_DIM + SUM_ROWS, 2 * tq), F32))
    state = update(s_d, jnp.max(s_d, axis=0, keepdims=True), 0.0, kd, state)

    def trip(r, carry, n_produce):
        state, mx = carry[:2], list(carry[2])
        for i in range(DEPTH):
            t = DEPTH * r + i
            state = consume(t, s_refs[i], mx[i], state)
            if i < n_produce:
                mx[i] = produce(t + DEPTH, s_refs[i])
        return state + (tuple(mx),)

    carry = state + (tuple(mx),)
    if trips > 1:
        carry = lax.fori_loop(0, trips - 1, functools.partial(trip, n_produce=DEPTH), carry)
    if trips > 0:
        carry = trip(jnp.int32(trips - 1), carry, n_produce=1)
    if n_off > 0:
        _, acc_fin = consume(jnp.int32(n_off - 1), s_refs[0], carry[2][0], carry[:2])
    else:
        _, acc_fin = carry[:2]

    lam_init = lam_ref[0]
    one_minus = lam_ref[1]
    lam = (jnp.exp(jnp.sum(lq1_ref[...] * lk1_ref[...], axis=-1, keepdims=True))
           - jnp.exp(jnp.sum(lq2_ref[...] * lk2_ref[...], axis=-1, keepdims=True)) + lam_init)
    o_both = acc_fin[:V_DIM, :] / acc_fin[V_DIM:V_DIM + 1, :]
    o = o_both[:, :tq] - lam * o_both[:, tq:]
    ms = jnp.mean(o * o, axis=0, keepdims=True)
    y = o * lax.rsqrt(ms + EPS) * subln_ref[...] * one_minus
    o_ref[...] = y.T.astype(o_ref.dtype)


def diff_attention(qk, vt, slopes, lam_consts, lq1, lk1, lq2, lk2, subln_col, b, s, tq, tk):
    t = b * s
    nq = s // tq
    nk = s // tk
    assert nk == 1 or (nk - 2) % DEPTH == 0, (s, tk)
    assert tk % tq == 0 and tq <= 256
    hh = ATTN_HEADS
    smem = pl.BlockSpec(memory_space=pltpu.SMEM)
    vec = pl.BlockSpec((1, HEAD_DIM), lambda bi, h, qi: (0, 0))
    return pl.pallas_call(
        functools.partial(_attn_kernel, tq=tq, tk=tk, nk=nk),
        grid=(b, hh, nq),
        in_specs=[
            smem, smem,
            pl.BlockSpec((tq, V_DIM), lambda bi, h, qi: (bi * nq + qi, h)),
            pl.BlockSpec((s, V_DIM), lambda bi, h, qi: (bi, hh + h)),
            pl.BlockSpec((V_DIM, s), lambda bi, h, qi: (h, bi)),
            vec, vec, vec, vec,
            pl.BlockSpec((V_DIM, 1), lambda bi, h, qi: (0, 0)),
        ],
        out_specs=pl.BlockSpec((tq, V_DIM), lambda bi, h, qi: (bi * nq + qi, h)),
        out_shape=jax.ShapeDtypeStruct((t, hh * V_DIM), BF16),
        scratch_shapes=[pltpu.VMEM((2, tk, LANES), BF16)] + [pltpu.VMEM((tk, 2 * tq), F32)] * DEPTH,
        compiler_params=_params(("parallel", "parallel", "arbitrary")),
        name="diff_attention",
    )(slopes, lam_consts, qk, qk, vt, lq1, lk1, lq2, lk2, subln_col)


def _ssd_kernel(*refs, rev, final, nc):
    if final:
        (xbc_ref, prev_ref, next_ref, dt_ref, dtt_ref, cw_ref, cb_ref, bias_p_ref, bias_c_ref,
         alog_p_ref, alog_c_ref, z_ref, yf_ref, dskip_ref, nw_ref, y_ref, st_ref) = refs
    else:
        (xbc_ref, prev_ref, next_ref, dt_ref, dtt_ref, cw_ref, cb_ref, bias_p_ref, bias_c_ref,
         alog_p_ref, alog_c_ref, y_ref, st_ref) = refs
    direction = 1 if rev else 0
    c = pl.program_id(1)
    cc = (nc - 1 - c) if rev else c
    ll = CHUNK

    @pl.when(c == 0)
    def _():
        st_ref[...] = jnp.zeros_like(st_ref)

    x = xbc_ref[...]
    before = jnp.where(cc == 0, 0.0, prev_ref[...])
    after = jnp.where(cc == nc - 1, 0.0, next_ref[...])
    x_ext = jnp.concatenate([x, after, before], axis=0)
    n_ext = x_ext.shape[0]
    xp = pltpu.roll(x_ext, 1, axis=0)[0:ll, :]
    xn = pltpu.roll(x_ext, n_ext - 1, axis=0)[0:ll, :]
    cw = cw_ref[...]
    u = _silu(cb_ref[...] + xp * cw[0:1, :] + x * cw[1:2, :] + xn * cw[2:3, :])
    xs = u[:, :D_INNER]
    bm = u[:, D_INNER:D_INNER + SSD_GROUPS * D_STATE]
    cm = u[:, D_INNER + SSD_GROUPS * D_STATE:]

    jj = lax.broadcasted_iota(jnp.int32, (DT_PAD, D_INNER), 0)
    col = lax.broadcasted_iota(jnp.int32, (DT_PAD, D_INNER), 1)
    head_of_col = lax.shift_right_logical(col, int(math.log2(SSD_HEAD_DIM)))
    expand = jnp.where(jj == direction * SSD_HEADS + head_of_col, 1.0, 0.0).astype(BF16)
    dt_cols = _softplus(dt_ref[...] + bias_p_ref[...])
    a_cols = dt_cols * (-jnp.exp(alog_p_ref[...]))
    dt_rows = _softplus(dtt_ref[direction * SSD_HEADS:(direction + 1) * SSD_HEADS, :] + bias_c_ref[...])
    a_rows = dt_rows * (-jnp.exp(alog_c_ref[...]))

    ri = lax.broadcasted_iota(jnp.int32, (ll, ll), 0)
    ci = lax.broadcasted_iota(jnp.int32, (ll, ll), 1)
    if rev:
        keep = ci >= ri
        edge = 0
    else:
        keep = ci <= ri
        edge = ll - 1
    tri = jnp.where(keep, 1.0, 0.0).astype(BF16)
    tri_t = jnp.where((ri >= ci) if rev else (ri <= ci), 1.0, 0.0).astype(BF16)
    cum_cols = _dot_01_by_f32(tri, a_cols)
    dt_full = _dot_f32_by_01(dt_cols, expand)
    cum_full = _dot_f32_by_01(cum_cols, expand)
    cum_rows = _dot_f32_by_01(a_rows, tri_t)

    xd = xs * dt_full
    xd_b = xd.astype(BF16)
    cum_edge = cum_full[edge:edge + 1, :]
    xdw = (xd * jnp.exp(cum_edge - cum_full)).astype(BF16)
    grow = jnp.exp(cum_full)
    lane = lax.broadcasted_iota(jnp.int32, (ll, LANES), 1)

    y_parts = []
    for g in range(SSD_GROUPS):
        bg = bm[:, g * D_STATE:(g + 1) * D_STATE]
        cg = cm[:, g * D_STATE:(g + 1) * D_STATE].astype(BF16)
        cb = _dot_nt(cg, bg.astype(BF16))
        gs = slice(g * GROUP_COLS, (g + 1) * GROUP_COLS)
        st_in = st_ref[:, gs]
        y_off = _dot(cg, st_in.astype(BF16)) * grow[:, gs]
        heads_per_group = SSD_HEADS // SSD_GROUPS
        for pair in range(heads_per_group // 2):
            lo = g * GROUP_COLS + pair * LANES
            xd_pair = xd_b[:, lo:lo + LANES]
            outs = []
            for sub in range(2):
                hd = g * heads_per_group + pair * 2 + sub
                seg = cum_full[:, hd * SSD_HEAD_DIM:hd * SSD_HEAD_DIM + 1] - cum_rows[hd:hd + 1, :]
                dec = jnp.exp(jnp.where(keep, seg, -jnp.inf))
                outs.append(_dot((cb * dec).astype(BF16), xd_pair))
            y_diag = jnp.where(lane < SSD_HEAD_DIM, outs[0], outs[1])
            y_parts.append(y_diag + y_off[:, pair * LANES:(pair + 1) * LANES])
        st_chunk = _dot(bg.T.astype(BF16), xdw[:, gs])
        st_ref[:, gs] = st_in * jnp.exp(cum_edge[:, gs]) + st_chunk
    y = jnp.concatenate(y_parts, axis=1)

    if final:
        y = yf_ref[...] + y + xs * dskip_ref[...]
        y = y * _silu(z_ref[...])
        nw = nw_ref[...]
        normed = []
        for g in range(SSD_GROUPS):
            gs = slice(g * GROUP_COLS, (g + 1) * GROUP_COLS)
            normed.append(_rms(y[:, gs], nw[:, gs]))
        y = jnp.concatenate(normed, axis=1)
    y_ref[...] = y.astype(y_ref.dtype)


def ssd_pass(xbc, dt, dtt, cw, cb, bias_p, bias_c, alog_p, alog_c, extras, b, s, rev):
    final = extras is not None
    t = b * s
    nc = s // CHUNK
    rows8 = CHUNK // 8
    last8 = t // 8 - 1

    def cidx(bi, c):
        return bi * nc + ((nc - 1 - c) if rev else c)

    def full(shape):
        return pl.BlockSpec(shape, lambda bi, c: (0, 0))

    chunk_rows = lambda w: pl.BlockSpec((CHUNK, w), lambda bi, c: (cidx(bi, c), 0))
    in_specs = [
        chunk_rows(CONV_DIM),
        pl.BlockSpec((8, CONV_DIM), lambda bi, c: (jnp.maximum(cidx(bi, c) * rows8 - 1, 0), 0)),
        pl.BlockSpec((8, CONV_DIM), lambda bi, c: (jnp.minimum((cidx(bi, c) + 1) * rows8, last8), 0)),
        chunk_rows(DT_PAD),
        pl.BlockSpec((2 * SSD_HEADS, CHUNK), lambda bi, c: (0, cidx(bi, c))),
        full((3, CONV_DIM)), full((1, CONV_DIM)),
        full((1, DT_PAD)), full((SSD_HEADS, 1)), full((1, DT_PAD)), full((SSD_HEADS, 1)),
    ]
    args = [xbc, xbc, xbc, dt, dtt, cw, cb, bias_p, bias_c, alog_p, alog_c]
    if final:
        z, yf, dskip_f, nw = extras
        in_specs += [chunk_rows(D_INNER), chunk_rows(D_INNER), full((1, D_INNER)), full((1, D_INNER))]
        args += [z, yf, dskip_f, nw]
    return pl.pallas_call(
        functools.partial(_ssd_kernel, rev=rev, final=final, nc=nc),
        grid=(b, nc),
        in_specs=in_specs,
        out_specs=chunk_rows(D_INNER),
        out_shape=jax.ShapeDtypeStruct((t, D_INNER), BF16 if final else F32),
        scratch_shapes=[pltpu.VMEM((D_STATE, D_INNER), F32)],
        compiler_params=_params(("parallel", "arbitrary")),
        name="ssd_bwd_final" if final else "ssd_fwd",
    )(*args)


def _merge_kernel(attn_ref, ssd_ref, gates_ref, x_ref, woa_ref, wos_ref, wout_ref, nw_ref, o_ref):
    a = _dot(attn_ref[...], woa_ref[...])
    s = _dot(ssd_ref[...], wos_ref[...])
    gates = gates_ref[...]
    merged = jax.nn.sigmoid(gates[:, :D_MODEL]) * a + jax.nn.sigmoid(gates[:, D_MODEL:]) * s
    mo = _dot(merged.astype(BF16), wout_ref[...])
    o_ref[...] = x_ref[...] + _rms(mo, nw_ref[...])


def merge_out(attn, ssd, gates, x, woa, wos, wout, nw, tm):
    t, d = x.shape
    rows = lambda w: pl.BlockSpec((tm, w), lambda i: (i, 0))
    full = lambda shape: pl.BlockSpec(shape, lambda i: (0, 0))
    return pl.pallas_call(
        _merge_kernel,
        grid=(t // tm,),
        in_specs=[rows(d), rows(d), rows(2 * d), rows(d), full((d, d)), full((d, d)), full((d, d)),
                  full((1, d))],
        out_specs=rows(d),
        out_shape=jax.ShapeDtypeStruct((t, d), F32),
        compiler_params=_params(("parallel",)),
        name="merge_out",
    )(attn, ssd, gates, x, woa, wos, wout, nw)


def _ffn_kernel(x_ref, xp_ref, xn_ref, gpre_ref, wa_ref, wg_ref, cwa_ref, cwg_ref, cba_ref, cbg_ref,
                wd_ref, gpost_ref, o_ref, h_ref, acc_ref, *, tm, tiles_per_seq, nf):
    i = pl.program_id(0)
    f = pl.program_id(1)

    @pl.when(f == 0)
    def _():
        gpre = gpre_ref[...]
        pos = i % tiles_per_seq
        h_ref[0:tm, :] = _rms(x_ref[...], gpre).astype(BF16)
        hp = jnp.where(pos == 0, 0.0, _rms(xp_ref[...], gpre))
        hn = jnp.where(pos == tiles_per_seq - 1, 0.0, _rms(xn_ref[...], gpre))
        h_ref[tm:tm + HALO, :] = hn.astype(BF16)
        h_ref[tm + HALO:tm + 2 * HALO, :] = hp.astype(BF16)
        acc_ref[...] = jnp.zeros_like(acc_ref)

    h = h_ref[...]
    n_ext = tm + 2 * HALO

    def conv_branch(w_ref, cw_ref, cb_ref):
        u = _dot(h, w_ref[...])
        um = u[0:tm, :]
        up = pltpu.roll(u, 1, axis=0)[0:tm, :]
        un = pltpu.roll(u, n_ext - 1, axis=0)[0:tm, :]
        cw = cw_ref[...]
        return cb_ref[...] + up * cw[0:1, :] + um * cw[1:2, :] + un * cw[2:3, :]

    a = conv_branch(wa_ref, cwa_ref, cba_ref)
    g = conv_branch(wg_ref, cwg_ref, cbg_ref)
    act = (_silu(g) * a).astype(BF16)
    acc_ref[...] += _dot(act, wd_ref[...])

    @pl.when(f == nf - 1)
    def _():
        o_ref[...] = x_ref[...] + _rms(acc_ref[...], gpost_ref[...])


def ffn(x, gpre, wa, wg, cwa, cwg, cba, cbg, wd, gpost, s, tm, tf):
    t, d = x.shape
    nf = D_FF_PAD // tf
    tiles_per_seq = s // tm
    blocks = tm // HALO
    last = t // HALO - 1
    full = lambda shape: pl.BlockSpec(shape, lambda i, f: (0, 0))
    colblk = lambda r: pl.BlockSpec((r, tf), lambda i, f: (0, f))
    return pl.pallas_call(
        functools.partial(_ffn_kernel, tm=tm, tiles_per_seq=tiles_per_seq, nf=nf),
        grid=(t // tm, nf),
        in_specs=[
            pl.BlockSpec((tm, d), lambda i, f: (i, 0)),
            pl.BlockSpec((HALO, d), lambda i, f: (jnp.maximum(i * blocks - 1, 0), 0)),
            pl.BlockSpec((HALO, d), lambda i, f: (jnp.minimum((i + 1) * blocks, last), 0)),
            full((1, d)),
            colblk(d), colblk(d), colblk(3), colblk(3), colblk(1), colblk(1),
            pl.BlockSpec((tf, d), lambda i, f: (f, 0)),
            full((1, d)),
        ],
        out_specs=pl.BlockSpec((tm, d), lambda i, f: (i, 0)),
        out_shape=jax.ShapeDtypeStruct((t, d), F32),
        scratch_shapes=[pltpu.VMEM((tm + 2 * HALO, d), BF16), pltpu.VMEM((tm, d), F32)],
        compiler_params=_params(("parallel", "arbitrary")),
        name="ffn",
    )(x, x, x, gpre, wa, wg, cwa, cwg, cba, cbg, wd, gpost)


def _tiles(s):
    return dict(
        tm_proj=min(256, s),
        tq=min(256, s), tk=min(512, s),
        tm_merge=min(256, s),
        tm_ffn=min(1024, s), tf=256,
    )


def _layer(x, b, s, w, cfg):
    g_pre = w["norm_mix_pre"]
    qk, vt, z, xbc, gates, dt, dtt = in_proj(x, g_pre, w, cfg["tm_proj"])

    attn = diff_attention(qk, vt, w["slopes"], w["lam_consts"], w["lam_q1"], w["lam_k1"], w["lam_q2"],
                          w["lam_k2"], w["subln_col"], b, s, cfg["tq"], cfg["tk"])

    yf = ssd_pass(xbc, dt, dtt, w["conv_ssd_w"], w["conv_ssd_b"], w["bias_p"], w["bias_c"][0],
                  w["alog_p"], w["alog_c"][0], None, b, s, rev=False)
    ssd = ssd_pass(xbc, dt, dtt, w["conv_ssd_w"], w["conv_ssd_b"], w["bias_p"], w["bias_c"][1],
                   w["alog_p"], w["alog_c"][1], (z, yf, w["dskip_f"], w["ssd_norm"]), b, s, rev=True)

    x = merge_out(attn, ssd, gates, x, w["w_o_attn"], w["w_o_ssd"], w["w_out"], w["norm_mix_post"],
                  cfg["tm_merge"])
    x = ffn(x, w["norm_ffn_pre"], w["w_up_a"], w["w_up_g"], w["cw_a"], w["cw_g"], w["cb_a"], w["cb_g"],
            w["w_down"], w["norm_ffn_post"], s, cfg["tm_ffn"], cfg["tf"])
    return x


def _prepare_weights(norm_mix_pre, norm_mix_post, norm_ffn_pre, norm_ffn_post, w_in, lam_q1, lam_k1,
                     lam_q2, lam_k2, attn_subln, conv_ssd_w, conv_ssd_b, dt_bias, a_log, d_skip, ssd_norm,
                     w_o_attn, w_o_ssd, w_out, w_up, conv_ffn_w, conv_ffn_b, w_down):
    depth = w_in.shape[0]
    qk_cols = ATTN_HEADS * 2 * HEAD_DIM
    attn_w = ATTN_HEADS * V_DIM
    cuts = [0, 2 * qk_cols]
    for width in (attn_w, D_INNER, CONV_DIM, 2 * SSD_HEADS, 2 * D_MODEL):
        cuts.append(cuts[-1] + width)
    seg = lambda i: w_in[:, :, cuts[i]:cuts[i + 1]]
    w_dt = seg(4)
    row = lambda a: a[:, None, :]
    rep = lambda a: jnp.repeat(a, SSD_HEAD_DIM, axis=-1)
    pad_ff = lambda a: jnp.pad(a, [(0, 0)] * (a.ndim - 1) + [(0, D_FF_PAD - D_FF)])
    pad_dt = lambda a: jnp.pad(a.reshape(depth, 1, 2 * SSD_HEADS), ((0, 0), (0, 0), (0, DT_PAD - 2 * SSD_HEADS)))
    lam_init = [0.8 - 0.6 * math.exp(-0.3 * l) for l in range(depth)]
    return dict(
        norm_mix_pre=row(norm_mix_pre), norm_mix_post=row(norm_mix_post),
        norm_ffn_pre=row(norm_ffn_pre), norm_ffn_post=row(norm_ffn_post),
        w_qk=seg(0).astype(BF16),
        w_vt=jnp.swapaxes(seg(1), 1, 2).astype(BF16),
        w_z=seg(2).astype(BF16),
        w_xbc=seg(3).astype(BF16),
        w_dt=jnp.pad(w_dt, ((0, 0), (0, 0), (0, DT_PAD - 2 * SSD_HEADS))).astype(BF16),
        w_dtt=jnp.swapaxes(w_dt, 1, 2).astype(BF16),
        w_gates=seg(5).astype(BF16),
        slopes=jnp.tile(jnp.asarray([2.0 ** (-8.0 * (i + 1) / ATTN_HEADS) for i in range(ATTN_HEADS)],
                                    F32)[None], (depth, 1)),
        lam_consts=jnp.asarray([[li, 1.0 - li] for li in lam_init], F32),
        lam_q1=row(lam_q1), lam_k1=row(lam_k1), lam_q2=row(lam_q2), lam_k2=row(lam_k2),
        subln_col=attn_subln[:, :, None],
        conv_ssd_w=conv_ssd_w, conv_ssd_b=row(conv_ssd_b),
        bias_p=pad_dt(dt_bias), bias_c=dt_bias[..., None],
        alog_p=pad_dt(a_log), alog_c=a_log[..., None],
        dskip_f=row(rep(d_skip)), ssd_norm=row(ssd_norm),
        w_o_attn=w_o_attn.astype(BF16), w_o_ssd=w_o_ssd.astype(BF16), w_out=w_out.astype(BF16),
        w_up_a=pad_ff(w_up[:, :, :D_FF]).astype(BF16), w_up_g=pad_ff(w_up[:, :, D_FF:]).astype(BF16),
        cw_a=pad_ff(conv_ffn_w[:, :, :D_FF]), cw_g=pad_ff(conv_ffn_w[:, :, D_FF:]),
        cb_a=row(pad_ff(conv_ffn_b[:, :D_FF])), cb_g=row(pad_ff(conv_ffn_b[:, D_FF:])),
        w_down=jnp.pad(w_down, ((0, 0), (0, D_FF_PAD - D_FF), (0, 0))).astype(BF16),
    )


def kernel(x_prompt, x_sample, norm_mix_pre, norm_mix_post, norm_ffn_pre, norm_ffn_post, w_in, lam_q1, lam_k1, lam_q2, lam_k2, attn_subln, conv_ssd_w, conv_ssd_b, dt_bias, a_log, d_skip, ssd_norm, w_o_attn, w_o_ssd, w_out, w_up, conv_ffn_w, conv_ffn_b, w_down):
    weights = _prepare_weights(norm_mix_pre, norm_mix_post, norm_ffn_pre, norm_ffn_post, w_in, lam_q1,
                               lam_k1, lam_q2, lam_k2, attn_subln, conv_ssd_w, conv_ssd_b, dt_bias, a_log,
                               d_skip, ssd_norm, w_o_attn, w_o_ssd, w_out, w_up, conv_ffn_w, conv_ffn_b,
                               w_down)
    groups = []
    for xg in (x_prompt, x_sample):
        b, s, d = xg.shape
        groups.append((b, s, _tiles(s)))

    def step(carry, w):
        out = tuple(_layer(x, b, s, w, cfg) for x, (b, s, cfg) in zip(carry, groups))
        return out, None

    init = tuple(xg.reshape(-1, xg.shape[-1]) for xg in (x_prompt, x_sample))
    out, _ = lax.scan(step, init, weights)
    return tuple(o.reshape(xg.shape) for o, xg in zip(out, (x_prompt, x_sample)))
```

```python
import functools
import math

import jax
import jax.numpy as jnp
from jax import lax
from jax.experimental import pallas as pl
from jax.experimental.pallas import tpu as pltpu

F32 = jnp.float32
BF16 = jnp.bfloat16

D_MODEL = 1024
ATTN_HEADS = 8
HEAD_DIM = 64
V_DIM = 2 * HEAD_DIM
SSD_HEADS = 16
SSD_HEAD_DIM = 64
D_INNER = SSD_HEADS * SSD_HEAD_DIM
SSD_GROUPS = 2
GROUP_COLS = D_INNER // SSD_GROUPS
D_STATE = 128
CHUNK = 128
CONV_DIM = D_INNER + 2 * SSD_GROUPS * D_STATE
D_FF = 2752
EPS = 1e-6

LANES = 128
D_FF_PAD = 2816
DT_PAD = LANES
HALO = 16
PROJ_CHUNK = 512

UNDERFLOW = 150.0
SUM_ROWS = 16
LOG2E = math.log2(math.e)
Q_PRESCALE = LOG2E / math.sqrt(HEAD_DIM)
VMEM_LIMIT = 56 * 1024 * 1024


def _params(sem):
    return pltpu.CompilerParams(dimension_semantics=sem, vmem_limit_bytes=VMEM_LIMIT)


def _rms(x, g):
    ms = jnp.mean(x * x, axis=-1, keepdims=True)
    return x * lax.rsqrt(ms + EPS) * g


def _dot(a, b):
    return jnp.dot(a, b, preferred_element_type=F32)


def _dot_nt(a, b):
    return lax.dot_general(a, b, (((1,), (1,)), ((), ())), preferred_element_type=F32)


def _bf16_terms(x):
    def top(v):
        bits = lax.bitcast_convert_type(v, jnp.uint32) & jnp.uint32(0xFFFF0000)
        return lax.bitcast_convert_type(bits, F32)

    hi = top(x)
    mid = top(x - hi)
    lo = x - hi - mid
    return hi, mid, lo


def _dot_f32_by_01(x, m01):
    return sum(_dot(term.astype(BF16), m01) for term in _bf16_terms(x))


def _dot_01_by_f32(m01, x):
    return sum(_dot(m01, term.astype(BF16)) for term in _bf16_terms(x))


def _softplus(x):
    return jnp.maximum(x, 0.0) + jnp.log1p(jnp.exp(-jnp.abs(x)))


def _silu(x):
    return x * jax.nn.sigmoid(x)


def _in_proj_kernel(x_ref, g_ref, wqk_ref, wvt_ref, wz_ref, wxbc_ref, wg_ref, wdt_ref, wdtt_ref,
                    qk_ref, vt_ref, z_ref, xbc_ref, gates_ref, dt_ref, dtt_ref):
    h = _rms(x_ref[...], g_ref[...]).astype(BF16)
    qk_cols = ATTN_HEADS * 2 * HEAD_DIM

    def project(w_ref, o_ref, scale_upto=0):
        n = w_ref.shape[1]
        for c0 in range(0, n, PROJ_CHUNK):
            c1 = min(c0 + PROJ_CHUNK, n)
            y = _dot(h, w_ref[:, c0:c1])
            if c1 <= scale_upto:
                y = y * Q_PRESCALE
            o_ref[:, c0:c1] = y.astype(o_ref.dtype)

    project(wqk_ref, qk_ref, scale_upto=qk_cols)
    for r0 in range(0, wvt_ref.shape[0], PROJ_CHUNK):
        vt_ref[r0:r0 + PROJ_CHUNK, :] = _dot_nt(wvt_ref[r0:r0 + PROJ_CHUNK, :], h).astype(vt_ref.dtype)
    project(wz_ref, z_ref)
    project(wxbc_ref, xbc_ref)
    project(wg_ref, gates_ref)
    project(wdt_ref, dt_ref)
    dtt_ref[...] = _dot_nt(wdtt_ref[...], h)


def in_proj(x, g, w, tm):
    t, d = x.shape
    weights = [w["w_qk"], w["w_vt"], w["w_z"], w["w_xbc"], w["w_gates"], w["w_dt"], w["w_dtt"]]
    rows = lambda n: pl.BlockSpec((tm, n), lambda i: (i, 0))
    cols = lambda n: pl.BlockSpec((n, tm), lambda i: (0, i))
    resident = lambda a: pl.BlockSpec(a.shape, lambda i: (0, 0), pipeline_mode=pl.Buffered(1))
    n_qk, n_v, n_z, n_xbc, n_g = (w["w_qk"].shape[1], w["w_vt"].shape[0], w["w_z"].shape[1],
                                  w["w_xbc"].shape[1], w["w_gates"].shape[1])
    n_dtt = w["w_dtt"].shape[0]
    return pl.pallas_call(
        _in_proj_kernel,
        grid=(t // tm,),
        in_specs=[rows(d), resident(g)] + [resident(a) for a in weights],
        out_specs=[rows(n_qk), cols(n_v), rows(n_z), rows(n_xbc), rows(n_g), rows(DT_PAD), cols(n_dtt)],
        out_shape=[
            jax.ShapeDtypeStruct((t, n_qk), BF16), jax.ShapeDtypeStruct((n_v, t), BF16),
            jax.ShapeDtypeStruct((t, n_z), F32), jax.ShapeDtypeStruct((t, n_xbc), F32),
            jax.ShapeDtypeStruct((t, n_g), F32), jax.ShapeDtypeStruct((t, DT_PAD), F32),
            jax.ShapeDtypeStruct((n_dtt, t), F32),
        ],
        compiler_params=_params(("parallel",)),
        name="in_proj",
    )(x, g, *weights)


def _attn_kernel(slopes_ref, lam_ref, q_ref, k_ref, vt_ref, lq1_ref, lk1_ref, lq2_ref, lk2_ref,
                 subln_ref, o_ref, kfeat_ref, knorm_ref, *s_refs, tq, tk, nk):
    h = pl.program_id(1)
    qi = pl.program_id(2)
    slope2 = slopes_ref[h] * LOG2E
    q0pos = qi * tq
    n_off = nk - 1
    kd = lax.shift_right_logical(qi, int(math.log2(tk // tq)))

    q = q_ref[...]
    lane = lax.broadcasted_iota(jnp.int32, q.shape, 1)
    zero = jnp.zeros_like(q)
    ii = lax.broadcasted_iota(jnp.int32, (tq, LANES), 0).astype(F32)
    fq = lax.broadcasted_iota(jnp.int32, (tq, LANES), 1)
    row_terms = _bf16_terms(-slope2 * ii)
    slope_terms = _bf16_terms(jnp.full((tq, LANES), slope2, F32))
    q_feat = jnp.zeros((tq, LANES), F32)
    for n in range(3):
        q_feat = jnp.where(fq == n, row_terms[n], q_feat)
        q_feat = jnp.where((fq == 3 + n) | (fq == 6 + n), slope_terms[n], q_feat)
    q_feat = q_feat.astype(BF16)
    q_ops = jnp.concatenate([
        jnp.concatenate([jnp.where(lane < HEAD_DIM, q, zero), q_feat], axis=1),
        jnp.concatenate([jnp.where(lane >= HEAD_DIM, q, zero), q_feat], axis=1)], axis=0)

    jj = lax.broadcasted_iota(jnp.int32, (tk, LANES), 0)
    fk = lax.broadcasted_iota(jnp.int32, (tk, LANES), 1)
    jj_lo = jnp.bitwise_and(jj, 255)
    jj_hi = (jj - jj_lo).astype(F32)
    jj_lo = jj_lo.astype(F32)
    k_feat = jnp.where(fk < 3, 1.0, jnp.where(fk < 6, jj_lo, jnp.where(fk < 9, jj_hi, 0.0)))
    kfeat_ref[0] = k_feat.astype(BF16)
    kfeat_ref[1] = (-k_feat).astype(BF16)
    ones_rows = jnp.ones((SUM_ROWS, tk), BF16)

    def raw_scores(kb, side):
        start = pl.multiple_of(kb * tk, tk)
        k_ops = jnp.concatenate([k_ref[pl.ds(start, tk), :], kfeat_ref[side]], axis=1)
        return _dot_nt(k_ops, q_ops)

    def offset(kb):
        return slope2 * jnp.abs(q0pos - kb * tk).astype(F32)

    def off_tile(t):
        side = (t >= kd).astype(jnp.int32)
        return t + side, side

    def produce(t, dst_ref):
        kb, side = off_tile(t)
        s = raw_scores(kb, side)
        dst_ref[...] = s
        return jnp.max(s, axis=0, keepdims=True)

    def update(s, mx, c, kb, state):
        m_old, acc_old = state
        m_new = jnp.maximum(m_old, mx - c)
        p = jnp.exp2(s - (m_new + c))
        alpha = jnp.exp2(m_old - m_new)
        vt = vt_ref[:, pl.ds(pl.multiple_of(kb * tk, tk), tk)]
        vt_ops = jnp.concatenate([vt, ones_rows], axis=0)
        return m_new, alpha * acc_old + _dot(vt_ops, p.astype(BF16))

    def consume(t, src_ref, mx, state):
        kb, _ = off_tile(t)
        return update(src_ref[...], mx, offset(kb), kb, state)

    DEPTH = len(s_refs)
    max_trips = (n_off - 1) // DEPTH

    @pl.when(qi == 0)
    def _():
        def tile_norm(j, best):
            kt = k_ref[pl.ds(pl.multiple_of(j * tk, tk), tk), :].astype(F32)
            return jnp.maximum(best, jnp.max(jnp.sum(kt * kt, axis=1, keepdims=True), axis=0, keepdims=True))
        knorm_ref[...] = jnp.broadcast_to(lax.fori_loop(0, nk, tile_norm, jnp.zeros((1, 1), F32)),
                                          knorm_ref.shape)

    c_d = offset(kd)
    s_d = jnp.minimum(raw_scores(kd, 0) - c_d, raw_scores(kd, 1) + c_d)
    state = (jnp.full((1, 2 * tq), -jnp.inf, F32), jnp.zeros((V_DIM + SUM_ROWS, 2 * tq), F32))
    state = update(s_d, jnp.max(s_d, axis=0, keepdims=True), 0.0, kd, state)

    qf = q.astype(F32)
    qn2 = jnp.max(jnp.sum(qf * qf, axis=1, keepdims=True), axis=0, keepdims=True)
    bound = jnp.sqrt(qn2 * knorm_ref[0:1, 0:1]) * 1.001 + 1.0
    m_min = jnp.min(state[0], axis=1, keepdims=True)
    reach = jnp.minimum((bound + UNDERFLOW - m_min) / slope2, 1e9)
    reach = (jnp.ceil(reach).astype(jnp.int32) + 1)[0, 0]
    tk_shift = int(math.log2(tk))
    kb_lo = jnp.minimum(lax.shift_right_logical(jnp.maximum(q0pos + 1 - reach, 0), tk_shift), kd)
    kb_hi = jnp.maximum(jnp.minimum(lax.shift_right_logical(reach + q0pos + tq - 2, tk_shift), nk - 1), kd)
    wanted = jnp.maximum(kb_hi - kb_lo - 1, 0)
    trips = jnp.clip(lax.shift_right_logical((wanted + DEPTH - 1) * 43, 7), 1, max_trips)
    first = jnp.minimum(kb_lo, n_off - 1 - DEPTH * trips)

    mx = tuple(produce(first + i, s_refs[i]) for i in range(DEPTH))

    def trip(r, carry, n_produce):
        state, mx = carry[:2], list(carry[2])
        for i in range(DEPTH):
            t = first + DEPTH * r + i
            state = consume(t, s_refs[i], mx[i], state)
            if i < n_produce:
                mx[i] = produce(t + DEPTH, s_refs[i])
        return state + (tuple(mx),)

    carry = lax.fori_loop(0, trips - 1, functools.partial(trip, n_produce=DEPTH), state + (mx,))
    carry = trip(trips - 1, carry, n_produce=1)
    _, acc_fin = consume(first + DEPTH * trips, s_refs[0], carry[2][0], carry[:2])

    lam_init = lam_ref[0]
    one_minus = lam_ref[1]
    lam = (jnp.exp(jnp.sum(lq1_ref[...] * lk1_ref[...], axis=-1, keepdims=True))
           - jnp.exp(jnp.sum(lq2_ref[...] * lk2_ref[...], axis=-1, keepdims=True)) + lam_init)
    o_both = acc_fin[:V_DIM, :] / acc_fin[V_DIM:V_DIM + 1, :]
    o = o_both[:, :tq] - lam * o_both[:, tq:]
    ms = jnp.mean(o * o, axis=0, keepdims=True)
    y = o * lax.rsqrt(ms + EPS) * subln_ref[...] * one_minus
    o_ref[...] = y.T.astype(o_ref.dtype)


def diff_attention(qk, vt, slopes, lam_consts, lq1, lk1, lq2, lk2, subln_col, b, s, tq, tk, depth):
    t = b * s
    nq = s // tq
    nk = s // tk
    assert depth == 3 and nk >= depth + 2 and (nk - 2) % depth == 0, (s, tk)
    assert tk % tq == 0 and tq <= 256
    hh = ATTN_HEADS
    smem = pl.BlockSpec(memory_space=pltpu.SMEM)
    vec = pl.BlockSpec((1, HEAD_DIM), lambda bi, h, qi: (0, 0))
    return pl.pallas_call(
        functools.partial(_attn_kernel, tq=tq, tk=tk, nk=nk),
        grid=(b, hh, nq),
        in_specs=[
            smem, smem,
            pl.BlockSpec((tq, V_DIM), lambda bi, h, qi: (bi * nq + qi, h)),
            pl.BlockSpec((s, V_DIM), lambda bi, h, qi: (bi, hh + h)),
            pl.BlockSpec((V_DIM, s), lambda bi, h, qi: (h, bi)),
            vec, vec, vec, vec,
            pl.BlockSpec((V_DIM, 1), lambda bi, h, qi: (0, 0)),
        ],
        out_specs=pl.BlockSpec((tq, V_DIM), lambda bi, h, qi: (bi * nq + qi, h)),
        out_shape=jax.ShapeDtypeStruct((t, hh * V_DIM), BF16),
        scratch_shapes=([pltpu.VMEM((2, tk, LANES), BF16), pltpu.VMEM((8, LANES), F32)]
                        + [pltpu.VMEM((tk, 2 * tq), F32)] * depth),
        compiler_params=_params(("parallel", "parallel", "arbitrary")),
        name="diff_attention",
    )(slopes, lam_consts, qk, qk, vt, lq1, lk1, lq2, lk2, subln_col)


def _ssd_kernel(*refs, rev, final, nc):
    if final:
        (xbc_ref, prev_ref, next_ref, dt_ref, dtt_ref, cw_ref, cb_ref, bias_p_ref, bias_c_ref,
         alog_p_ref, alog_c_ref, z_ref, yf_ref, dskip_ref, nw_ref, y_ref, st_ref) = refs
    else:
        (xbc_ref, prev_ref, next_ref, dt_ref, dtt_ref, cw_ref, cb_ref, bias_p_ref, bias_c_ref,
         alog_p_ref, alog_c_ref, y_ref, st_ref) = refs
    direction = 1 if rev else 0
    c = pl.program_id(1)
    cc = (nc - 1 - c) if rev else c
    ll = CHUNK

    @pl.when(c == 0)
    def _():
        st_ref[...] = jnp.zeros_like(st_ref)

    x = xbc_ref[...]
    before = jnp.where(cc == 0, 0.0, prev_ref[...])
    after = jnp.where(cc == nc - 1, 0.0, next_ref[...])
    x_ext = jnp.concatenate([x, after, before], axis=0)
    n_ext = x_ext.shape[0]
    xp = pltpu.roll(x_ext, 1, axis=0)[0:ll, :]
    xn = pltpu.roll(x_ext, n_ext - 1, axis=0)[0:ll, :]
    cw = cw_ref[...]
    u = _silu(cb_ref[...] + xp * cw[0:1, :] + x * cw[1:2, :] + xn * cw[2:3, :])
    xs = u[:, :D_INNER]
    bm = u[:, D_INNER:D_INNER + SSD_GROUPS * D_STATE]
    cm = u[:, D_INNER + SSD_GROUPS * D_STATE:]

    jj = lax.broadcasted_iota(jnp.int32, (DT_PAD, D_INNER), 0)
    col = lax.broadcasted_iota(jnp.int32, (DT_PAD, D_INNER), 1)
    head_of_col = lax.shift_right_logical(col, int(math.log2(SSD_HEAD_DIM)))
    expand = jnp.where(jj == direction * SSD_HEADS + head_of_col, 1.0, 0.0).astype(BF16)
    dt_cols = _softplus(dt_ref[...] + bias_p_ref[...])
    a_cols = dt_cols * (-jnp.exp(alog_p_ref[...]))
    dt_rows = _softplus(dtt_ref[direction * SSD_HEADS:(direction + 1) * SSD_HEADS, :] + bias_c_ref[...])
    a_rows = dt_rows * (-jnp.exp(alog_c_ref[...]))

    ri = lax.broadcasted_iota(jnp.int32, (ll, ll), 0)
    ci = lax.broadcasted_iota(jnp.int32, (ll, ll), 1)
    if rev:
        keep = ci >= ri
        edge = 0
    else:
        keep = ci <= ri
        edge = ll - 1
    tri = jnp.where(keep, 1.0, 0.0).astype(BF16)
    tri_t = jnp.where((ri >= ci) if rev else (ri <= ci), 1.0, 0.0).astype(BF16)
    cum_cols = _dot_01_by_f32(tri, a_cols)
    dt_full = _dot_f32_by_01(dt_cols, expand)
    cum_full = _dot_f32_by_01(cum_cols, expand)
    cum_rows = _dot_f32_by_01(a_rows, tri_t)

    xd = xs * dt_full
    xd_b = xd.astype(BF16)
    cum_edge = cum_full[edge:edge + 1, :]
    xdw = (xd * jnp.exp(cum_edge - cum_full)).astype(BF16)
    grow = jnp.exp(cum_full)
    lane = lax.broadcasted_iota(jnp.int32, (ll, LANES), 1)

    y_parts = []
    for g in range(SSD_GROUPS):
        bg = bm[:, g * D_STATE:(g + 1) * D_STATE]
        cg = cm[:, g * D_STATE:(g + 1) * D_STATE].astype(BF16)
        cb = _dot_nt(cg, bg.astype(BF16))
        gs = slice(g * GROUP_COLS, (g + 1) * GROUP_COLS)
        st_in = st_ref[:, gs]
        y_off = _dot(cg, st_in.astype(BF16)) * grow[:, gs]
        heads_per_group = SSD_HEADS // SSD_GROUPS
        for pair in range(heads_per_group // 2):
            lo = g * GROUP_COLS + pair * LANES
            xd_pair = xd_b[:, lo:lo + LANES]
            outs = []
            for sub in range(2):
                hd = g * heads_per_group + pair * 2 + sub
                seg = cum_full[:, hd * SSD_HEAD_DIM:hd * SSD_HEAD_DIM + 1] - cum_rows[hd:hd + 1, :]
                dec = jnp.exp(jnp.where(keep, seg, -jnp.inf))
                outs.append(_dot((cb * dec).astype(BF16), xd_pair))
            y_diag = jnp.where(lane < SSD_HEAD_DIM, outs[0], outs[1])
            y_parts.append(y_diag + y_off[:, pair * LANES:(pair + 1) * LANES])
        st_chunk = _dot(bg.T.astype(BF16), xdw[:, gs])
        st_ref[:, gs] = st_in * jnp.exp(cum_edge[:, gs]) + st_chunk
    y = jnp.concatenate(y_parts, axis=1)

    if final:
        y = yf_ref[...] + y + xs * dskip_ref[...]
        y = y * _silu(z_ref[...])
        nw = nw_ref[...]
        normed = []
        for g in range(SSD_GROUPS):
            gs = slice(g * GROUP_COLS, (g + 1) * GROUP_COLS)
            normed.append(_rms(y[:, gs], nw[:, gs]))
        y = jnp.concatenate(normed, axis=1)
    y_ref[...] = y.astype(y_ref.dtype)


def ssd_pass(xbc, dt, dtt, cw, cb, bias_p, bias_c, alog_p, alog_c, extras, b, s, rev):
    final = extras is not None
    t = b * s
    nc = s // CHUNK
    rows8 = CHUNK // 8
    last8 = t // 8 - 1

    def cidx(bi, c):
        return bi * nc + ((nc - 1 - c) if rev else c)

    def full(shape):
        return pl.BlockSpec(shape, lambda bi, c: (0, 0))

    chunk_rows = lambda w: pl.BlockSpec((CHUNK, w), lambda bi, c: (cidx(bi, c), 0))
    in_specs = [
        chunk_rows(CONV_DIM),
        pl.BlockSpec((8, CONV_DIM), lambda bi, c: (jnp.maximum(cidx(bi, c) * rows8 - 1, 0), 0)),
        pl.BlockSpec((8, CONV_DIM), lambda bi, c: (jnp.minimum((cidx(bi, c) + 1) * rows8, last8), 0)),
        chunk_rows(DT_PAD),
        pl.BlockSpec((2 * SSD_HEADS, CHUNK), lambda bi, c: (0, cidx(bi, c))),
        full((3, CONV_DIM)), full((1, CONV_DIM)),
        full((1, DT_PAD)), full((SSD_HEADS, 1)), full((1, DT_PAD)), full((SSD_HEADS, 1)),
    ]
    args = [xbc, xbc, xbc, dt, dtt, cw, cb, bias_p, bias_c, alog_p, alog_c]
    if final:
        z, yf, dskip_f, nw = extras
        in_specs += [chunk_rows(D_INNER), chunk_rows(D_INNER), full((1, D_INNER)), full((1, D_INNER))]
        args += [z, yf, dskip_f, nw]
    return pl.pallas_call(
        functools.partial(_ssd_kernel, rev=rev, final=final, nc=nc),
        grid=(b, nc),
        in_specs=in_specs,
        out_specs=chunk_rows(D_INNER),
        out_shape=jax.ShapeDtypeStruct((t, D_INNER), BF16 if final else F32),
        scratch_shapes=[pltpu.VMEM((D_STATE, D_INNER), F32)],
        compiler_params=_params(("parallel", "arbitrary")),
        name="ssd_bwd_final" if final else "ssd_fwd",
    )(*args)


def _merge_kernel(attn_ref, ssd_ref, gates_ref, x_ref, woa_ref, wos_ref, wout_ref, nw_ref, o_ref):
    a = _dot(attn_ref[...], woa_ref[...])
    s = _dot(ssd_ref[...], wos_ref[...])
    gates = gates_ref[...]
    merged = jax.nn.sigmoid(gates[:, :D_MODEL]) * a + jax.nn.sigmoid(gates[:, D_MODEL:]) * s
    mo = _dot(merged.astype(BF16), wout_ref[...])
    o_ref[...] = x_ref[...] + _rms(mo, nw_ref[...])


def merge_out(attn, ssd, gates, x, woa, wos, wout, nw, tm):
    t, d = x.shape
    rows = lambda w: pl.BlockSpec((tm, w), lambda i: (i, 0))
    full = lambda shape: pl.BlockSpec(shape, lambda i: (0, 0))
    return pl.pallas_call(
        _merge_kernel,
        grid=(t // tm,),
        in_specs=[rows(d), rows(d), rows(2 * d), rows(d), full((d, d)), full((d, d)), full((d, d)),
                  full((1, d))],
        out_specs=rows(d),
        out_shape=jax.ShapeDtypeStruct((t, d), F32),
        compiler_params=_params(("parallel",)),
        name="merge_out",
    )(attn, ssd, gates, x, woa, wos, wout, nw)


def _ffn_kernel(x_ref, xp_ref, xn_ref, gpre_ref, wa_ref, wg_ref, cwa_ref, cwg_ref, cba_ref, cbg_ref,
                wd_ref, gpost_ref, o_ref, h_ref, acc_ref, *, tm, tiles_per_seq, nf):
    i = pl.program_id(0)
    f = pl.program_id(1)

    @pl.when(f == 0)
    def _():
        gpre = gpre_ref[...]
        pos = i % tiles_per_seq
        h_ref[0:tm, :] = _rms(x_ref[...], gpre).astype(BF16)
        hp = jnp.where(pos == 0, 0.0, _rms(xp_ref[...], gpre))
        hn = jnp.where(pos == tiles_per_seq - 1, 0.0, _rms(xn_ref[...], gpre))
        h_ref[tm:tm + HALO, :] = hn.astype(BF16)
        h_ref[tm + HALO:tm + 2 * HALO, :] = hp.astype(BF16)
        acc_ref[...] = jnp.zeros_like(acc_ref)

    h = h_ref[...]
    n_ext = tm + 2 * HALO

    def conv_branch(w_ref, cw_ref, cb_ref):
        u = _dot(h, w_ref[...])
        um = u[0:tm, :]
        up = pltpu.roll(u, 1, axis=0)[0:tm, :]
        un = pltpu.roll(u, n_ext - 1, axis=0)[0:tm, :]
        cw = cw_ref[...]
        return cb_ref[...] + up * cw[0:1, :] + um * cw[1:2, :] + un * cw[2:3, :]

    a = conv_branch(wa_ref, cwa_ref, cba_ref)
    g = conv_branch(wg_ref, cwg_ref, cbg_ref)
    act = (_silu(g) * a).astype(BF16)
    acc_ref[...] += _dot(act, wd_ref[...])

    @pl.when(f == nf - 1)
    def _():
        o_ref[...] = x_ref[...] + _rms(acc_ref[...], gpost_ref[...])


def ffn(x, gpre, wa, wg, cwa, cwg, cba, cbg, wd, gpost, s, tm, tf):
    t, d = x.shape
    nf = D_FF_PAD // tf
    tiles_per_seq = s // tm
    blocks = tm // HALO
    last = t // HALO - 1
    full = lambda shape: pl.BlockSpec(shape, lambda i, f: (0, 0))
    colblk = lambda r: pl.BlockSpec((r, tf), lambda i, f: (0, f))
    return pl.pallas_call(
        functools.partial(_ffn_kernel, tm=tm, tiles_per_seq=tiles_per_seq, nf=nf),
        grid=(t // tm, nf),
        in_specs=[
            pl.BlockSpec((tm, d), lambda i, f: (i, 0)),
            pl.BlockSpec((HALO, d), lambda i, f: (jnp.maximum(i * blocks - 1, 0), 0)),
            pl.BlockSpec((HALO, d), lambda i, f: (jnp.minimum((i + 1) * blocks, last), 0)),
            full((1, d)),
            colblk(d), colblk(d), colblk(3), colblk(3), colblk(1), colblk(1),
            pl.BlockSpec((tf, d), lambda i, f: (f, 0)),
            full((1, d)),
        ],
        out_specs=pl.BlockSpec((tm, d), lambda i, f: (i, 0)),
        out_shape=jax.ShapeDtypeStruct((t, d), F32),
        scratch_shapes=[pltpu.VMEM((tm + 2 * HALO, d), BF16), pltpu.VMEM((tm, d), F32)],
        compiler_params=_params(("parallel", "arbitrary")),
        name="ffn",
    )(x, x, x, gpre, wa, wg, cwa, cwg, cba, cbg, wd, gpost)


def _tiles(s):
    return dict(
        tm_proj=min(256, s),
        tq=min(256, s), tk=min(512, s), depth=3,
        tm_merge=min(256, s),
        tm_ffn=min(1024, s), tf=256,
    )


def _layer(x, b, s, w, cfg):
    g_pre = w["norm_mix_pre"]
    qk, vt, z, xbc, gates, dt, dtt = in_proj(x, g_pre, w, cfg["tm_proj"])

    attn = diff_attention(qk, vt, w["slopes"], w["lam_consts"], w["lam_q1"], w["lam_k1"], w["lam_q2"],
                          w["lam_k2"], w["subln_col"], b, s, cfg["tq"], cfg["tk"], cfg["depth"])

    yf = ssd_pass(xbc, dt, dtt, w["conv_ssd_w"], w["conv_ssd_b"], w["bias_p"], w["bias_c"][0],
                  w["alog_p"], w["alog_c"][0], None, b, s, rev=False)
    ssd = ssd_pass(xbc, dt, dtt, w["conv_ssd_w"], w["conv_ssd_b"], w["bias_p"], w["bias_c"][1],
                   w["alog_p"], w["alog_c"][1], (z, yf, w["dskip_f"], w["ssd_norm"]), b, s, rev=True)

    x = merge_out(attn, ssd, gates, x, w["w_o_attn"], w["w_o_ssd"], w["w_out"], w["norm_mix_post"],
                  cfg["tm_merge"])
    x = ffn(x, w["norm_ffn_pre"], w["w_up_a"], w["w_up_g"], w["cw_a"], w["cw_g"], w["cb_a"], w["cb_g"],
            w["w_down"], w["norm_ffn_post"], s, cfg["tm_ffn"], cfg["tf"])
    return x


def _prepare_weights(norm_mix_pre, norm_mix_post, norm_ffn_pre, norm_ffn_post, w_in, lam_q1, lam_k1,
                     lam_q2, lam_k2, attn_subln, conv_ssd_w, conv_ssd_b, dt_bias, a_log, d_skip, ssd_norm,
                     w_o_attn, w_o_ssd, w_out, w_up, conv_ffn_w, conv_ffn_b, w_down):
    depth = w_in.shape[0]
    qk_cols = ATTN_HEADS * 2 * HEAD_DIM
    attn_w = ATTN_HEADS * V_DIM
    cuts = [0, 2 * qk_cols]
    for width in (attn_w, D_INNER, CONV_DIM, 2 * SSD_HEADS, 2 * D_MODEL):
        cuts.append(cuts[-1] + width)
    seg = lambda i: w_in[:, :, cuts[i]:cuts[i + 1]]
    w_dt = seg(4)
    row = lambda a: a[:, None, :]
    rep = lambda a: jnp.repeat(a, SSD_HEAD_DIM, axis=-1)
    pad_ff = lambda a: jnp.pad(a, [(0, 0)] * (a.ndim - 1) + [(0, D_FF_PAD - D_FF)])
    pad_dt = lambda a: jnp.pad(a.reshape(depth, 1, 2 * SSD_HEADS), ((0, 0), (0, 0), (0, DT_PAD - 2 * SSD_HEADS)))
    lam_init = [0.8 - 0.6 * math.exp(-0.3 * l) for l in range(depth)]
    return dict(
        norm_mix_pre=row(norm_mix_pre), norm_mix_post=row(norm_mix_post),
        norm_ffn_pre=row(norm_ffn_pre), norm_ffn_post=row(norm_ffn_post),
        w_qk=seg(0).astype(BF16),
        w_vt=jnp.swapaxes(seg(1), 1, 2).astype(BF16),
        w_z=seg(2).astype(BF16),
        w_xbc=seg(3).astype(BF16),
        w_dt=jnp.pad(w_dt, ((0, 0), (0, 0), (0, DT_PAD - 2 * SSD_HEADS))).astype(BF16),
        w_dtt=jnp.swapaxes(w_dt, 1, 2).astype(BF16),
        w_gates=seg(5).astype(BF16),
        slopes=jnp.tile(jnp.asarray([2.0 ** (-8.0 * (i + 1) / ATTN_HEADS) for i in range(ATTN_HEADS)],
                                    F32)[None], (depth, 1)),
        lam_consts=jnp.asarray([[li, 1.0 - li] for li in lam_init], F32),
        lam_q1=row(lam_q1), lam_k1=row(lam_k1), lam_q2=row(lam_q2), lam_k2=row(lam_k2),
        subln_col=attn_subln[:, :, None],
        conv_ssd_w=conv_ssd_w, conv_ssd_b=row(conv_ssd_b),
        bias_p=pad_dt(dt_bias), bias_c=dt_bias[..., None],
        alog_p=pad_dt(a_log), alog_c=a_log[..., None],
        dskip_f=row(rep(d_skip)), ssd_norm=row(ssd_norm),
        w_o_attn=w_o_attn.astype(BF16), w_o_ssd=w_o_ssd.astype(BF16), w_out=w_out.astype(BF16),
        w_up_a=pad_ff(w_up[:, :, :D_FF]).astype(BF16), w_up_g=pad_ff(w_up[:, :, D_FF:]).astype(BF16),
        cw_a=pad_ff(conv_ffn_w[:, :, :D_FF]), cw_g=pad_ff(conv_ffn_w[:, :, D_FF:]),
        cb_a=row(pad_ff(conv_ffn_b[:, :D_FF])), cb_g=row(pad_ff(conv_ffn_b[:, D_FF:])),
        w_down=jnp.pad(w_down, ((0, 0), (0, D_FF_PAD - D_FF), (0, 0))).astype(BF16),
    )


def kernel(x_prompt, x_sample, norm_mix_pre, norm_mix_post, norm_ffn_pre, norm_ffn_post, w_in, lam_q1, lam_k1, lam_q2, lam_k2, attn_subln, conv_ssd_w, conv_ssd_b, dt_bias, a_log, d_skip, ssd_norm, w_o_attn, w_o_ssd, w_out, w_up, conv_ffn_w, conv_ffn_b, w_down):
    weights = _prepare_weights(norm_mix_pre, norm_mix_post, norm_ffn_pre, norm_ffn_post, w_in, lam_q1,
                               lam_k1, lam_q2, lam_k2, attn_subln, conv_ssd_w, conv_ssd_b, dt_bias, a_log,
                               d_skip, ssd_norm, w_o_attn, w_o_ssd, w_out, w_up, conv_ffn_w, conv_ffn_b,
                               w_down)
    groups = []
    for xg in (x_prompt, x_sample):
        b, s, d = xg.shape
        groups.append((b, s, _tiles(s)))

    def step(carry, w):
        out = tuple(_layer(x, b, s, w, cfg) for x, (b, s, cfg) in zip(carry, groups))
        return out, None

    init = tuple(xg.reshape(-1, xg.shape[-1]) for xg in (x_prompt, x_sample))
    out, _ = lax.scan(step, init, weights)
    return tuple(o.reshape(xg.shape) for o, xg in zip(out, (x_prompt, x_sample)))
```

```python
import functools
import math

import jax
import jax.numpy as jnp
from jax import lax
from jax.experimental import pallas as pl
from jax.experimental.pallas import tpu as pltpu

F32 = jnp.float32
BF16 = jnp.bfloat16

D_MODEL = 1024
ATTN_HEADS = 8
HEAD_DIM = 64
V_DIM = 2 * HEAD_DIM
SSD_HEADS = 16
SSD_HEAD_DIM = 64
D_INNER = SSD_HEADS * SSD_HEAD_DIM
SSD_GROUPS = 2
GROUP_COLS = D_INNER // SSD_GROUPS
D_STATE = 128
CHUNK = 128
CONV_DIM = D_INNER + 2 * SSD_GROUPS * D_STATE
D_FF = 2752
EPS = 1e-6

LANES = 128
D_FF_PAD = 2816
DT_PAD = LANES
HALO = 16
PROJ_CHUNK = 512

UNDERFLOW = 150.0
SUM_ROWS = 16
LOG2E = math.log2(math.e)
Q_PRESCALE = LOG2E / math.sqrt(HEAD_DIM)
VMEM_LIMIT = 56 * 1024 * 1024


def _params(sem):
    return pltpu.CompilerParams(dimension_semantics=sem, vmem_limit_bytes=VMEM_LIMIT)


def _rms(x, g):
    ms = jnp.mean(x * x, axis=-1, keepdims=True)
    return x * lax.rsqrt(ms + EPS) * g


def _dot(a, b):
    return jnp.dot(a, b, preferred_element_type=F32)


def _dot_nt(a, b):
    return lax.dot_general(a, b, (((1,), (1,)), ((), ())), preferred_element_type=F32)


def _bf16_terms(x):
    def top(v):
        bits = lax.bitcast_convert_type(v, jnp.uint32) & jnp.uint32(0xFFFF0000)
        return lax.bitcast_convert_type(bits, F32)

    hi = top(x)
    mid = top(x - hi)
    lo = x - hi - mid
    return hi, mid, lo


def _dot_f32_by_01(x, m01):
    return sum(_dot(term.astype(BF16), m01) for term in _bf16_terms(x))


def _dot_01_by_f32(m01, x):
    return sum(_dot(m01, term.astype(BF16)) for term in _bf16_terms(x))


def _softplus(x):
    return jnp.maximum(x, 0.0) + jnp.log1p(jnp.exp(-jnp.abs(x)))


def _silu(x):
    return x * jax.nn.sigmoid(x)


def _in_proj_kernel(x_ref, g_ref, wqk_ref, wvt_ref, wz_ref, wxbc_ref, wg_ref, wdt_ref, wdtt_ref,
                    qk_ref, vt_ref, z_ref, xbc_ref, gates_ref, dt_ref, dtt_ref):
    h = _rms(x_ref[...], g_ref[...]).astype(BF16)
    qk_cols = ATTN_HEADS * 2 * HEAD_DIM

    def project(w_ref, o_ref, scale_upto=0):
        n = w_ref.shape[1]
        for c0 in range(0, n, PROJ_CHUNK):
            c1 = min(c0 + PROJ_CHUNK, n)
            y = _dot(h, w_ref[:, c0:c1])
            if c1 <= scale_upto:
                y = y * Q_PRESCALE
            o_ref[:, c0:c1] = y.astype(o_ref.dtype)

    project(wqk_ref, qk_ref, scale_upto=qk_cols)
    for r0 in range(0, wvt_ref.shape[0], PROJ_CHUNK):
        vt_ref[r0:r0 + PROJ_CHUNK, :] = _dot_nt(wvt_ref[r0:r0 + PROJ_CHUNK, :], h).astype(vt_ref.dtype)
    project(wz_ref, z_ref)
    project(wxbc_ref, xbc_ref)
    project(wg_ref, gates_ref)
    project(wdt_ref, dt_ref)
    dtt_ref[...] = _dot_nt(wdtt_ref[...], h)


def in_proj(x, g, w, tm):
    t, d = x.shape
    weights = [w["w_qk"], w["w_vt"], w["w_z"], w["w_xbc"], w["w_gates"], w["w_dt"], w["w_dtt"]]
    rows = lambda n: pl.BlockSpec((tm, n), lambda i: (i, 0))
    cols = lambda n: pl.BlockSpec((n, tm), lambda i: (0, i))
    resident = lambda a: pl.BlockSpec(a.shape, lambda i: (0, 0), pipeline_mode=pl.Buffered(1))
    n_qk, n_v, n_z, n_xbc, n_g = (w["w_qk"].shape[1], w["w_vt"].shape[0], w["w_z"].shape[1],
                                  w["w_xbc"].shape[1], w["w_gates"].shape[1])
    n_dtt = w["w_dtt"].shape[0]
    return pl.pallas_call(
        _in_proj_kernel,
        grid=(t // tm,),
        in_specs=[rows(d), resident(g)] + [resident(a) for a in weights],
        out_specs=[rows(n_qk), cols(n_v), rows(n_z), rows(n_xbc), rows(n_g), rows(DT_PAD), cols(n_dtt)],
        out_shape=[
            jax.ShapeDtypeStruct((t, n_qk), BF16), jax.ShapeDtypeStruct((n_v, t), BF16),
            jax.ShapeDtypeStruct((t, n_z), F32), jax.ShapeDtypeStruct((t, n_xbc), F32),
            jax.ShapeDtypeStruct((t, n_g), F32), jax.ShapeDtypeStruct((t, DT_PAD), F32),
            jax.ShapeDtypeStruct((n_dtt, t), F32),
        ],
        compiler_params=_params(("parallel",)),
        name="in_proj",
    )(x, g, *weights)


def _attn_kernel(slopes_ref, lam_ref, q_ref, k_ref, vt_ref, lq1_ref, lk1_ref, lq2_ref, lk2_ref,
                 subln_ref, o_ref, kfeat_ref, knorm_ref, *s_refs, tq, tk, nk):
    h = pl.program_id(1)
    qi = pl.program_id(2)
    slope2 = slopes_ref[h] * LOG2E
    q0pos = qi * tq
    n_off = nk - 1
    kd = lax.shift_right_logical(qi, int(math.log2(tk // tq)))

    q = q_ref[...]
    lane = lax.broadcasted_iota(jnp.int32, q.shape, 1)
    zero = jnp.zeros_like(q)
    ii = lax.broadcasted_iota(jnp.int32, (tq, LANES), 0).astype(F32)
    fq = lax.broadcasted_iota(jnp.int32, (tq, LANES), 1)
    row_terms = _bf16_terms(-slope2 * ii)
    slope_terms = _bf16_terms(jnp.full((tq, LANES), slope2, F32))
    q_feat = jnp.zeros((tq, LANES), F32)
    for n in range(3):
        q_feat = jnp.where(fq == n, row_terms[n], q_feat)
        q_feat = jnp.where((fq == 3 + n) | (fq == 6 + n), slope_terms[n], q_feat)
    q_feat = q_feat.astype(BF16)
    q_ops = jnp.concatenate([
        jnp.concatenate([jnp.where(lane < HEAD_DIM, q, zero), q_feat], axis=1),
        jnp.concatenate([jnp.where(lane >= HEAD_DIM, q, zero), q_feat], axis=1)], axis=0)

    jj = lax.broadcasted_iota(jnp.int32, (tk, LANES), 0)
    fk = lax.broadcasted_iota(jnp.int32, (tk, LANES), 1)
    jj_lo = jnp.bitwise_and(jj, 255)
    jj_hi = (jj - jj_lo).astype(F32)
    jj_lo = jj_lo.astype(F32)
    k_feat = jnp.where(fk < 3, 1.0, jnp.where(fk < 6, jj_lo, jnp.where(fk < 9, jj_hi, 0.0)))
    kfeat_ref[0] = k_feat.astype(BF16)
    kfeat_ref[1] = (-k_feat).astype(BF16)
    ones_rows = jnp.ones((SUM_ROWS, tk), BF16)

    def raw_scores(kb, side):
        start = pl.multiple_of(kb * tk, tk)
        k_ops = jnp.concatenate([k_ref[pl.ds(start, tk), :], kfeat_ref[side]], axis=1)
        return _dot_nt(k_ops, q_ops)

    def offset(kb):
        return slope2 * jnp.abs(q0pos - kb * tk).astype(F32)

    def off_tile(t):
        side = (t >= kd).astype(jnp.int32)
        return t + side, side

    def produce(t, dst_ref):
        kb, side = off_tile(t)
        s = raw_scores(kb, side)
        dst_ref[...] = s
        return jnp.max(s, axis=0, keepdims=True)

    def update(s, mx, c, kb, state):
        m_old, acc_old = state
        m_new = jnp.maximum(m_old, mx - c)
        p = jnp.exp2(s - (m_new + c))
        alpha = jnp.exp2(m_old - m_new)
        vt = vt_ref[:, pl.ds(pl.multiple_of(kb * tk, tk), tk)]
        vt_ops = jnp.concatenate([vt, ones_rows], axis=0)
        return m_new, alpha * acc_old + _dot(vt_ops, p.astype(BF16))

    def consume(t, src_ref, mx, state):
        kb, _ = off_tile(t)
        return update(src_ref[...], mx, offset(kb), kb, state)

    DEPTH = len(s_refs)
    max_trips = (n_off - 1) // DEPTH

    @pl.when(qi == 0)
    def _():
        def tile_norm(j, best):
            kt = k_ref[pl.ds(pl.multiple_of(j * tk, tk), tk), :].astype(F32)
            return jnp.maximum(best, jnp.max(jnp.sum(kt * kt, axis=1, keepdims=True), axis=0, keepdims=True))
        knorm_ref[...] = jnp.broadcast_to(lax.fori_loop(0, nk, tile_norm, jnp.zeros((1, 1), F32)),
                                          knorm_ref.shape)

    c_d = offset(kd)
    s_d = jnp.minimum(raw_scores(kd, 0) - c_d, raw_scores(kd, 1) + c_d)
    state = (jnp.full((1, 2 * tq), -jnp.inf, F32), jnp.zeros((V_DIM + SUM_ROWS, 2 * tq), F32))
    state = update(s_d, jnp.max(s_d, axis=0, keepdims=True), 0.0, kd, state)

    qf = q.astype(F32)
    qn2 = jnp.max(jnp.sum(qf * qf, axis=1, keepdims=True), axis=0, keepdims=True)
    bound = jnp.sqrt(qn2 * knorm_ref[0:1, 0:1]) * 1.001 + 1.0
    m_min = jnp.min(state[0], axis=1, keepdims=True)
    reach = jnp.minimum((bound + UNDERFLOW - m_min) / slope2, 1e9)
    reach = (jnp.ceil(reach).astype(jnp.int32) + 1)[0, 0]
    tk_shift = int(math.log2(tk))
    kb_lo = jnp.minimum(lax.shift_right_logical(jnp.maximum(q0pos + 1 - reach, 0), tk_shift), kd)
    kb_hi = jnp.maximum(jnp.minimum(lax.shift_right_logical(reach + q0pos + tq - 2, tk_shift), nk - 1), kd)
    needed = kb_hi - kb_lo

    def one_tile(state):
        t = jnp.minimum(kb_lo, n_off - 1)
        return consume(t, s_refs[0], produce(t, s_refs[0]), state)

    def pipelined(state):
        trips = jnp.clip(lax.shift_right_logical((needed - 1 + DEPTH - 1) * 43, 7), 1, max_trips)
        first = jnp.minimum(kb_lo, n_off - 1 - DEPTH * trips)
        mx = tuple(produce(first + i, s_refs[i]) for i in range(DEPTH))

        def trip(r, carry, n_produce):
            state, mx = carry[:2], list(carry[2])
            for i in range(DEPTH):
                t = first + DEPTH * r + i
                state = consume(t, s_refs[i], mx[i], state)
                if i < n_produce:
                    mx[i] = produce(t + DEPTH, s_refs[i])
            return state + (tuple(mx),)

        carry = lax.fori_loop(0, trips - 1, functools.partial(trip, n_produce=DEPTH), state + (mx,))
        carry = trip(trips - 1, carry, n_produce=1)
        return consume(first + DEPTH * trips, s_refs[0], carry[2][0], carry[:2])

    _, acc_fin = lax.cond(needed <= 1, one_tile, pipelined, state)

    lam_init = lam_ref[0]
    one_minus = lam_ref[1]
    lam = (jnp.exp(jnp.sum(lq1_ref[...] * lk1_ref[...], axis=-1, keepdims=True))
           - jnp.exp(jnp.sum(lq2_ref[...] * lk2_ref[...], axis=-1, keepdims=True)) + lam_init)
    o_both = acc_fin[:V_DIM, :] / acc_fin[V_DIM:V_DIM + 1, :]
    o = o_both[:, :tq] - lam * o_both[:, tq:]
    ms = jnp.mean(o * o, axis=0, keepdims=True)
    y = o * lax.rsqrt(ms + EPS) * subln_ref[...] * one_minus
    o_ref[...] = y.T.astype(o_ref.dtype)


def diff_attention(qk, vt, slopes, lam_consts, lq1, lk1, lq2, lk2, subln_col, b, s, tq, tk, depth):
    t = b * s
    nq = s // tq
    nk = s // tk
    assert depth == 3 and nk >= depth + 2 and (nk - 2) % depth == 0, (s, tk)
    assert tk % tq == 0 and tq <= 256
    hh = ATTN_HEADS
    smem = pl.BlockSpec(memory_space=pltpu.SMEM)
    vec = pl.BlockSpec((1, HEAD_DIM), lambda bi, h, qi: (0, 0))
    return pl.pallas_call(
        functools.partial(_attn_kernel, tq=tq, tk=tk, nk=nk),
        grid=(b, hh, nq),
        in_specs=[
            smem, smem,
            pl.BlockSpec((tq, V_DIM), lambda bi, h, qi: (bi * nq + qi, h)),
            pl.BlockSpec((s, V_DIM), lambda bi, h, qi: (bi, hh + h)),
            pl.BlockSpec((V_DIM, s), lambda bi, h, qi: (h, bi)),
            vec, vec, vec, vec,
            pl.BlockSpec((V_DIM, 1), lambda bi, h, qi: (0, 0)),
        ],
        out_specs=pl.BlockSpec((tq, V_DIM), lambda bi, h, qi: (bi * nq + qi, h)),
        out_shape=jax.ShapeDtypeStruct((t, hh * V_DIM), BF16),
        scratch_shapes=([pltpu.VMEM((2, tk, LANES), BF16), pltpu.VMEM((8, LANES), F32)]
                        + [pltpu.VMEM((tk, 2 * tq), F32)] * depth),
        compiler_params=_params(("parallel", "parallel", "arbitrary")),
        name="diff_attention",
    )(slopes, lam_consts, qk, qk, vt, lq1, lk1, lq2, lk2, subln_col)


def _ssd_kernel(*refs, rev, final, nc):
    if final:
        (xbc_ref, prev_ref, next_ref, dt_ref, dtt_ref, cw_ref, cb_ref, bias_p_ref, bias_c_ref,
         alog_p_ref, alog_c_ref, z_ref, yf_ref, dskip_ref, nw_ref, y_ref, st_ref) = refs
    else:
        (xbc_ref, prev_ref, next_ref, dt_ref, dtt_ref, cw_ref, cb_ref, bias_p_ref, bias_c_ref,
         alog_p_ref, alog_c_ref, y_ref, st_ref) = refs
    direction = 1 if rev else 0
    c = pl.program_id(1)
    cc = (nc - 1 - c) if rev else c
    ll = CHUNK

    @pl.when(c == 0)
    def _():
        st_ref[...] = jnp.zeros_like(st_ref)

    x = xbc_ref[...]
    before = jnp.where(cc == 0, 0.0, prev_ref[...])
    after = jnp.where(cc == nc - 1, 0.0, next_ref[...])
    x_ext = jnp.concatenate([x, after, before], axis=0)
    n_ext = x_ext.shape[0]
    xp = pltpu.roll(x_ext, 1, axis=0)[0:ll, :]
    xn = pltpu.roll(x_ext, n_ext - 1, axis=0)[0:ll, :]
    cw = cw_ref[...]
    u = _silu(cb_ref[...] + xp * cw[0:1, :] + x * cw[1:2, :] + xn * cw[2:3, :])
    xs = u[:, :D_INNER]
    bm = u[:, D_INNER:D_INNER + SSD_GROUPS * D_STATE]
    cm = u[:, D_INNER + SSD_GROUPS * D_STATE:]

    jj = lax.broadcasted_iota(jnp.int32, (DT_PAD, D_INNER), 0)
    col = lax.broadcasted_iota(jnp.int32, (DT_PAD, D_INNER), 1)
    head_of_col = lax.shift_right_logical(col, int(math.log2(SSD_HEAD_DIM)))
    expand = jnp.where(jj == direction * SSD_HEADS + head_of_col, 1.0, 0.0).astype(BF16)
    dt_cols = _softplus(dt_ref[...] + bias_p_ref[...])
    a_cols = dt_cols * (-jnp.exp(alog_p_ref[...]))
    dt_rows = _softplus(dtt_ref[direction * SSD_HEADS:(direction + 1) * SSD_HEADS, :] + bias_c_ref[...])
    a_rows = dt_rows * (-jnp.exp(alog_c_ref[...]))

    ri = lax.broadcasted_iota(jnp.int32, (ll, ll), 0)
    ci = lax.broadcasted_iota(jnp.int32, (ll, ll), 1)
    if rev:
        keep = ci >= ri
        edge = 0
    else:
        keep = ci <= ri
        edge = ll - 1
    tri = jnp.where(keep, 1.0, 0.0).astype(BF16)
    tri_t = jnp.where((ri >= ci) if rev else (ri <= ci), 1.0, 0.0).astype(BF16)
    cum_cols = _dot_01_by_f32(tri, a_cols)
    dt_full = _dot_f32_by_01(dt_cols, expand)
    cum_full = _dot_f32_by_01(cum_cols, expand)
    cum_rows = _dot_f32_by_01(a_rows, tri_t)

    xd = xs * dt_full
    xd_b = xd.astype(BF16)
    cum_edge = cum_full[edge:edge + 1, :]
    xdw = (xd * jnp.exp(cum_edge - cum_full)).astype(BF16)
    grow = jnp.exp(cum_full)
    lane = lax.broadcasted_iota(jnp.int32, (ll, LANES), 1)

    y_parts = []
    for g in range(SSD_GROUPS):
        bg = bm[:, g * D_STATE:(g + 1) * D_STATE]
        cg = cm[:, g * D_STATE:(g + 1) * D_STATE].astype(BF16)
        cb = _dot_nt(cg, bg.astype(BF16))
        gs = slice(g * GROUP_COLS, (g + 1) * GROUP_COLS)
        st_in = st_ref[:, gs]
        y_off = _dot(cg, st_in.astype(BF16)) * grow[:, gs]
        heads_per_group = SSD_HEADS // SSD_GROUPS
        for pair in range(heads_per_group // 2):
            lo = g * GROUP_COLS + pair * LANES
            xd_pair = xd_b[:, lo:lo + LANES]
            outs = []
            for sub in range(2):
                hd = g * heads_per_group + pair * 2 + sub
                seg = cum_full[:, hd * SSD_HEAD_DIM:hd * SSD_HEAD_DIM + 1] - cum_rows[hd:hd + 1, :]
                dec = jnp.exp(jnp.where(keep, seg, -jnp.inf))
                outs.append(_dot((cb * dec).astype(BF16), xd_pair))
            y_diag = jnp.where(lane < SSD_HEAD_DIM, outs[0], outs[1])
            y_parts.append(y_diag + y_off[:, pair * LANES:(pair + 1) * LANES])
        st_chunk = _dot(bg.T.astype(BF16), xdw[:, gs])
        st_ref[:, gs] = st_in * jnp.exp(cum_edge[:, gs]) + st_chunk
    y = jnp.concatenate(y_parts, axis=1)

    if final:
        y = yf_ref[...] + y + xs * dskip_ref[...]
        y = y * _silu(z_ref[...])
        nw = nw_ref[...]
        normed = []
        for g in range(SSD_GROUPS):
            gs = slice(g * GROUP_COLS, (g + 1) * GROUP_COLS)
            normed.append(_rms(y[:, gs], nw[:, gs]))
        y = jnp.concatenate(normed, axis=1)
    y_ref[...] = y.astype(y_ref.dtype)


def ssd_pass(xbc, dt, dtt, cw, cb, bias_p, bias_c, alog_p, alog_c, extras, b, s, rev):
    final = extras is not None
    t = b * s
    nc = s // CHUNK
    rows8 = CHUNK // 8
    last8 = t // 8 - 1

    def cidx(bi, c):
        return bi * nc + ((nc - 1 - c) if rev else c)

    def full(shape):
        return pl.BlockSpec(shape, lambda bi, c: (0, 0))

    chunk_rows = lambda w: pl.BlockSpec((CHUNK, w), lambda bi, c: (cidx(bi, c), 0))
    in_specs = [
        chunk_rows(CONV_DIM),
        pl.BlockSpec((8, CONV_DIM), lambda bi, c: (jnp.maximum(cidx(bi, c) * rows8 - 1, 0), 0)),
        pl.BlockSpec((8, CONV_DIM), lambda bi, c: (jnp.minimum((cidx(bi, c) + 1) * rows8, last8), 0)),
        chunk_rows(DT_PAD),
        pl.BlockSpec((2 * SSD_HEADS, CHUNK), lambda bi, c: (0, cidx(bi, c))),
        full((3, CONV_DIM)), full((1, CONV_DIM)),
        full((1, DT_PAD)), full((SSD_HEADS, 1)), full((1, DT_PAD)), full((SSD_HEADS, 1)),
    ]
    args = [xbc, xbc, xbc, dt, dtt, cw, cb, bias_p, bias_c, alog_p, alog_c]
    if final:
        z, yf, dskip_f, nw = extras
        in_specs += [chunk_rows(D_INNER), chunk_rows(D_INNER), full((1, D_INNER)), full((1, D_INNER))]
        args += [z, yf, dskip_f, nw]
    return pl.pallas_call(
        functools.partial(_ssd_kernel, rev=rev, final=final, nc=nc),
        grid=(b, nc),
        in_specs=in_specs,
        out_specs=chunk_rows(D_INNER),
        out_shape=jax.ShapeDtypeStruct((t, D_INNER), BF16 if final else F32),
        scratch_shapes=[pltpu.VMEM((D_STATE, D_INNER), F32)],
        compiler_params=_params(("parallel", "arbitrary")),
        name="ssd_bwd_final" if final else "ssd_fwd",
    )(*args)


def _merge_kernel(attn_ref, ssd_ref, gates_ref, x_ref, woa_ref, wos_ref, wout_ref, nw_ref, o_ref):
    a = _dot(attn_ref[...], woa_ref[...])
    s = _dot(ssd_ref[...], wos_ref[...])
    gates = gates_ref[...]
    merged = jax.nn.sigmoid(gates[:, :D_MODEL]) * a + jax.nn.sigmoid(gates[:, D_MODEL:]) * s
    mo = _dot(merged.astype(BF16), wout_ref[...])
    o_ref[...] = x_ref[...] + _rms(mo, nw_ref[...])


def merge_out(attn, ssd, gates, x, woa, wos, wout, nw, tm):
    t, d = x.shape
    rows = lambda w: pl.BlockSpec((tm, w), lambda i: (i, 0))
    full = lambda shape: pl.BlockSpec(shape, lambda i: (0, 0))
    return pl.pallas_call(
        _merge_kernel,
        grid=(t // tm,),
        in_specs=[rows(d), rows(d), rows(2 * d), rows(d), full((d, d)), full((d, d)), full((d, d)),
                  full((1, d))],
        out_specs=rows(d),
        out_shape=jax.ShapeDtypeStruct((t, d), F32),
        compiler_params=_params(("parallel",)),
        name="merge_out",
    )(attn, ssd, gates, x, woa, wos, wout, nw)


def _ffn_kernel(x_ref, xp_ref, xn_ref, gpre_ref, wa_ref, wg_ref, cwa_ref, cwg_ref, cba_ref, cbg_ref,
                wd_ref, gpost_ref, o_ref, h_ref, acc_ref, *, tm, tiles_per_seq, nf):
    i = pl.program_id(0)
    f = pl.program_id(1)

    @pl.when(f == 0)
    def _():
        gpre = gpre_ref[...]
        pos = i % tiles_per_seq
        h_ref[0:tm, :] = _rms(x_ref[...], gpre).astype(BF16)
        hp = jnp.where(pos == 0, 0.0, _rms(xp_ref[...], gpre))
        hn = jnp.where(pos == tiles_per_seq - 1, 0.0, _rms(xn_ref[...], gpre))
        h_ref[tm:tm + HALO, :] = hn.astype(BF16)
        h_ref[tm + HALO:tm + 2 * HALO, :] = hp.astype(BF16)
        acc_ref[...] = jnp.zeros_like(acc_ref)

    h = h_ref[...]
    n_ext = tm + 2 * HALO

    def conv_branch(w_ref, cw_ref, cb_ref):
        u = _dot(h, w_ref[...])
        um = u[0:tm, :]
        up = pltpu.roll(u, 1, axis=0)[0:tm, :]
        un = pltpu.roll(u, n_ext - 1, axis=0)[0:tm, :]
        cw = cw_ref[...]
        return cb_ref[...] + up * cw[0:1, :] + um * cw[1:2, :] + un * cw[2:3, :]

    a = conv_branch(wa_ref, cwa_ref, cba_ref)
    g = conv_branch(wg_ref, cwg_ref, cbg_ref)
    act = (_silu(g) * a).astype(BF16)
    acc_ref[...] += _dot(act, wd_ref[...])

    @pl.when(f == nf - 1)
    def _():
        o_ref[...] = x_ref[...] + _rms(acc_ref[...], gpost_ref[...])


def ffn(x, gpre, wa, wg, cwa, cwg, cba, cbg, wd, gpost, s, tm, tf):
    t, d = x.shape
    nf = D_FF_PAD // tf
    tiles_per_seq = s // tm
    blocks = tm // HALO
    last = t // HALO - 1
    full = lambda shape: pl.BlockSpec(shape, lambda i, f: (0, 0))
    colblk = lambda r: pl.BlockSpec((r, tf), lambda i, f: (0, f))
    return pl.pallas_call(
        functools.partial(_ffn_kernel, tm=tm, tiles_per_seq=tiles_per_seq, nf=nf),
        grid=(t // tm, nf),
        in_specs=[
            pl.BlockSpec((tm, d), lambda i, f: (i, 0)),
            pl.BlockSpec((HALO, d), lambda i, f: (jnp.maximum(i * blocks - 1, 0), 0)),
            pl.BlockSpec((HALO, d), lambda i, f: (jnp.minimum((i + 1) * blocks, last), 0)),
            full((1, d)),
            colblk(d), colblk(d), colblk(3), colblk(3), colblk(1), colblk(1),
            pl.BlockSpec((tf, d), lambda i, f: (f, 0)),
            full((1, d)),
        ],
        out_specs=pl.BlockSpec((tm, d), lambda i, f: (i, 0)),
        out_shape=jax.ShapeDtypeStruct((t, d), F32),
        scratch_shapes=[pltpu.VMEM((tm + 2 * HALO, d), BF16), pltpu.VMEM((tm, d), F32)],
        compiler_params=_params(("parallel", "arbitrary")),
        name="ffn",
    )(x, x, x, gpre, wa, wg, cwa, cwg, cba, cbg, wd, gpost)


def _tiles(s):
    return dict(
        tm_proj=min(256, s),
        tq=min(256, s), tk=min(512, s), depth=3,
        tm_merge=min(256, s),
        tm_ffn=min(1024, s), tf=256,
    )


def _layer(x, b, s, w, cfg):
    g_pre = w["norm_mix_pre"]
    qk, vt, z, xbc, gates, dt, dtt = in_proj(x, g_pre, w, cfg["tm_proj"])

    attn = diff_attention(qk, vt, w["slopes"], w["lam_consts"], w["lam_q1"], w["lam_k1"], w["lam_q2"],
                          w["lam_k2"], w["subln_col"], b, s, cfg["tq"], cfg["tk"], cfg["depth"])

    yf = ssd_pass(xbc, dt, dtt, w["conv_ssd_w"], w["conv_ssd_b"], w["bias_p"], w["bias_c"][0],
                  w["alog_p"], w["alog_c"][0], None, b, s, rev=False)
    ssd = ssd_pass(xbc, dt, dtt, w["conv_ssd_w"], w["conv_ssd_b"], w["bias_p"], w["bias_c"][1],
                   w["alog_p"], w["alog_c"][1], (z, yf, w["dskip_f"], w["ssd_norm"]), b, s, rev=True)

    x = merge_out(attn, ssd, gates, x, w["w_o_attn"], w["w_o_ssd"], w["w_out"], w["norm_mix_post"],
                  cfg["tm_merge"])
    x = ffn(x, w["norm_ffn_pre"], w["w_up_a"], w["w_up_g"], w["cw_a"], w["cw_g"], w["cb_a"], w["cb_g"],
            w["w_down"], w["norm_ffn_post"], s, cfg["tm_ffn"], cfg["tf"])
    return x


def _prepare_weights(norm_mix_pre, norm_mix_post, norm_ffn_pre, norm_ffn_post, w_in, lam_q1, lam_k1,
                     lam_q2, lam_k2, attn_subln, conv_ssd_w, conv_ssd_b, dt_bias, a_log, d_skip, ssd_norm,
                     w_o_attn, w_o_ssd, w_out, w_up, conv_ffn_w, conv_ffn_b, w_down):
    depth = w_in.shape[0]
    qk_cols = ATTN_HEADS * 2 * HEAD_DIM
    attn_w = ATTN_HEADS * V_DIM
    cuts = [0, 2 * qk_cols]
    for width in (attn_w, D_INNER, CONV_DIM, 2 * SSD_HEADS, 2 * D_MODEL):
        cuts.append(cuts[-1] + width)
    seg = lambda i: w_in[:, :, cuts[i]:cuts[i + 1]]
    w_dt = seg(4)
    row = lambda a: a[:, None, :]
    rep = lambda a: jnp.repeat(a, SSD_HEAD_DIM, axis=-1)
    pad_ff = lambda a: jnp.pad(a, [(0, 0)] * (a.ndim - 1) + [(0, D_FF_PAD - D_FF)])
    pad_dt = lambda a: jnp.pad(a.reshape(depth, 1, 2 * SSD_HEADS), ((0, 0), (0, 0), (0, DT_PAD - 2 * SSD_HEADS)))
    lam_init = [0.8 - 0.6 * math.exp(-0.3 * l) for l in range(depth)]
    return dict(
        norm_mix_pre=row(norm_mix_pre), norm_mix_post=row(norm_mix_post),
        norm_ffn_pre=row(norm_ffn_pre), norm_ffn_post=row(norm_ffn_post),
        w_qk=seg(0).astype(BF16),
        w_vt=jnp.swapaxes(seg(1), 1, 2).astype(BF16),
        w_z=seg(2).astype(BF16),
        w_xbc=seg(3).astype(BF16),
        w_dt=jnp.pad(w_dt, ((0, 0), (0, 0), (0, DT_PAD - 2 * SSD_HEADS))).astype(BF16),
        w_dtt=jnp.swapaxes(w_dt, 1, 2).astype(BF16),
        w_gates=seg(5).astype(BF16),
        slopes=jnp.tile(jnp.asarray([2.0 ** (-8.0 * (i + 1) / ATTN_HEADS) for i in range(ATTN_HEADS)],
                                    F32)[None], (depth, 1)),
        lam_consts=jnp.asarray([[li, 1.0 - li] for li in lam_init], F32),
        lam_q1=row(lam_q1), lam_k1=row(lam_k1), lam_q2=row(lam_q2), lam_k2=row(lam_k2),
        subln_col=attn_subln[:, :, None],
        conv_ssd_w=conv_ssd_w, conv_ssd_b=row(conv_ssd_b),
        bias_p=pad_dt(dt_bias), bias_c=dt_bias[..., None],
        alog_p=pad_dt(a_log), alog_c=a_log[..., None],
        dskip_f=row(rep(d_skip)), ssd_norm=row(ssd_norm),
        w_o_attn=w_o_attn.astype(BF16), w_o_ssd=w_o_ssd.astype(BF16), w_out=w_out.astype(BF16),
        w_up_a=pad_ff(w_up[:, :, :D_FF]).astype(BF16), w_up_g=pad_ff(w_up[:, :, D_FF:]).astype(BF16),
        cw_a=pad_ff(conv_ffn_w[:, :, :D_FF]), cw_g=pad_ff(conv_ffn_w[:, :, D_FF:]),
        cb_a=row(pad_ff(conv_ffn_b[:, :D_FF])), cb_g=row(pad_ff(conv_ffn_b[:, D_FF:])),
        w_down=jnp.pad(w_down, ((0, 0), (0, D_FF_PAD - D_FF), (0, 0))).astype(BF16),
    )


def kernel(x_prompt, x_sample, norm_mix_pre, norm_mix_post, norm_ffn_pre, norm_ffn_post, w_in, lam_q1, lam_k1, lam_q2, lam_k2, attn_subln, conv_ssd_w, conv_ssd_b, dt_bias, a_log, d_skip, ssd_norm, w_o_attn, w_o_ssd, w_out, w_up, conv_ffn_w, conv_ffn_b, w_down):
    weights = _prepare_weights(norm_mix_pre, norm_mix_post, norm_ffn_pre, norm_ffn_post, w_in, lam_q1,
                               lam_k1, lam_q2, lam_k2, attn_subln, conv_ssd_w, conv_ssd_b, dt_bias, a_log,
                               d_skip, ssd_norm, w_o_attn, w_o_ssd, w_out, w_up, conv_ffn_w, conv_ffn_b,
                               w_down)
    groups = []
    for xg in (x_prompt, x_sample):
        b, s, d = xg.shape
        groups.append((b, s, _tiles(s)))

    def step(carry, w):
        out = tuple(_layer(x, b, s, w, cfg) for x, (b, s, cfg) in zip(carry, groups))
        return out, None

    init = tuple(xg.reshape(-1, xg.shape[-1]) for xg in (x_prompt, x_sample))
    out, _ = lax.scan(step, init, weights)
    return tuple(o.reshape(xg.shape) for o, xg in zip(out, (x_prompt, x_sample)))
```

```python
import functools
import math

import jax
import jax.numpy as jnp
from jax import lax
from jax.experimental import pallas as pl
from jax.experimental.pallas import tpu as pltpu

F32 = jnp.float32
BF16 = jnp.bfloat16

D_MODEL = 1024
ATTN_HEADS = 8
HEAD_DIM = 64
V_DIM = 2 * HEAD_DIM
SSD_HEADS = 16
SSD_HEAD_DIM = 64
D_INNER = SSD_HEADS * SSD_HEAD_DIM
SSD_GROUPS = 2
GROUP_COLS = D_INNER // SSD_GROUPS
D_STATE = 128
CHUNK = 128
CONV_DIM = D_INNER + 2 * SSD_GROUPS * D_STATE
D_FF = 2752
EPS = 1e-6

LANES = 128
D_FF_PAD = 2816
DT_PAD = LANES
HALO = 16
PROJ_CHUNK = 512

UNDERFLOW = 150.0
SUM_ROWS = 16
LOG2E = math.log2(math.e)
Q_PRESCALE = LOG2E / math.sqrt(HEAD_DIM)
VMEM_LIMIT = 56 * 1024 * 1024


def _params(sem):
    return pltpu.CompilerParams(dimension_semantics=sem, vmem_limit_bytes=VMEM_LIMIT)


def _rms(x, g):
    ms = jnp.mean(x * x, axis=-1, keepdims=True)
    return x * lax.rsqrt(ms + EPS) * g


def _dot(a, b):
    return jnp.dot(a, b, preferred_element_type=F32)


def _dot_nt(a, b):
    return lax.dot_general(a, b, (((1,), (1,)), ((), ())), preferred_element_type=F32)


def _bf16_terms(x):
    def top(v):
        bits = lax.bitcast_convert_type(v, jnp.uint32) & jnp.uint32(0xFFFF0000)
        return lax.bitcast_convert_type(bits, F32)

    hi = top(x)
    mid = top(x - hi)
    lo = x - hi - mid
    return hi, mid, lo


def _dot_f32_by_01(x, m01):
    return sum(_dot(term.astype(BF16), m01) for term in _bf16_terms(x))


def _dot_01_by_f32(m01, x):
    return sum(_dot(m01, term.astype(BF16)) for term in _bf16_terms(x))


def _softplus(x):
    return jnp.maximum(x, 0.0) + jnp.log1p(jnp.exp(-jnp.abs(x)))


def _silu(x):
    return x * jax.nn.sigmoid(x)


def _in_proj_kernel(x_ref, g_ref, wqk_ref, wvt_ref, wz_ref, wxbc_ref, wg_ref, wdt_ref, wdtt_ref,
                    qk_ref, vt_ref, z_ref, xbc_ref, gates_ref, dt_ref, dtt_ref):
    h = _rms(x_ref[...], g_ref[...]).astype(BF16)
    qk_cols = ATTN_HEADS * 2 * HEAD_DIM

    def project(w_ref, o_ref, scale_upto=0):
        n = w_ref.shape[1]
        for c0 in range(0, n, PROJ_CHUNK):
            c1 = min(c0 + PROJ_CHUNK, n)
            y = _dot(h, w_ref[:, c0:c1])
            if c1 <= scale_upto:
                y = y * Q_PRESCALE
            o_ref[:, c0:c1] = y.astype(o_ref.dtype)

    project(wqk_ref, qk_ref, scale_upto=qk_cols)
    for r0 in range(0, wvt_ref.shape[0], PROJ_CHUNK):
        vt_ref[r0:r0 + PROJ_CHUNK, :] = _dot_nt(wvt_ref[r0:r0 + PROJ_CHUNK, :], h).astype(vt_ref.dtype)
    project(wz_ref, z_ref)
    project(wxbc_ref, xbc_ref)
    project(wg_ref, gates_ref)
    project(wdt_ref, dt_ref)
    dtt_ref[...] = _dot_nt(wdtt_ref[...], h)


def in_proj(x, g, w, tm):
    t, d = x.shape
    weights = [w["w_qk"], w["w_vt"], w["w_z"], w["w_xbc"], w["w_gates"], w["w_dt"], w["w_dtt"]]
    rows = lambda n: pl.BlockSpec((tm, n), lambda i: (i, 0))
    cols = lambda n: pl.BlockSpec((n, tm), lambda i: (0, i))
    resident = lambda a: pl.BlockSpec(a.shape, lambda i: (0, 0), pipeline_mode=pl.Buffered(1))
    n_qk, n_v, n_z, n_xbc, n_g = (w["w_qk"].shape[1], w["w_vt"].shape[0], w["w_z"].shape[1],
                                  w["w_xbc"].shape[1], w["w_gates"].shape[1])
    n_dtt = w["w_dtt"].shape[0]
    return pl.pallas_call(
        _in_proj_kernel,
        grid=(t // tm,),
        in_specs=[rows(d), resident(g)] + [resident(a) for a in weights],
        out_specs=[rows(n_qk), cols(n_v), rows(n_z), rows(n_xbc), rows(n_g), rows(DT_PAD), cols(n_dtt)],
        out_shape=[
            jax.ShapeDtypeStruct((t, n_qk), BF16), jax.ShapeDtypeStruct((n_v, t), BF16),
            jax.ShapeDtypeStruct((t, n_z), F32), jax.ShapeDtypeStruct((t, n_xbc), F32),
            jax.ShapeDtypeStruct((t, n_g), F32), jax.ShapeDtypeStruct((t, DT_PAD), F32),
            jax.ShapeDtypeStruct((n_dtt, t), F32),
        ],
        compiler_params=_params(("parallel",)),
        name="in_proj",
    )(x, g, *weights)


def _attn_kernel(slopes_ref, lam_ref, q_ref, k_ref, vt_ref, lq1_ref, lk1_ref, lq2_ref, lk2_ref,
                 subln_ref, o_ref, kfeat_ref, qfeat_ref, knorm_ref, *s_refs, tq, tk, nk):
    h = pl.program_id(1)
    qi = pl.program_id(2)
    slope2 = slopes_ref[h] * LOG2E
    q0pos = qi * tq
    n_off = nk - 1
    kd = lax.shift_right_logical(qi, int(math.log2(tk // tq)))

    @pl.when(qi == 0)
    def _():
        ii = lax.broadcasted_iota(jnp.int32, (tq, LANES), 0).astype(F32)
        fq = lax.broadcasted_iota(jnp.int32, (tq, LANES), 1)
        row_terms = _bf16_terms(-slope2 * ii)
        slope_terms = _bf16_terms(jnp.full((tq, LANES), slope2, F32))
        q_feat = jnp.zeros((tq, LANES), F32)
        for n in range(3):
            q_feat = jnp.where(fq == n, row_terms[n], q_feat)
            q_feat = jnp.where((fq == 3 + n) | (fq == 6 + n), slope_terms[n], q_feat)
        qfeat_ref[...] = q_feat.astype(BF16)
        jj = lax.broadcasted_iota(jnp.int32, (tk, LANES), 0)
        fk = lax.broadcasted_iota(jnp.int32, (tk, LANES), 1)
        jj_lo = jnp.bitwise_and(jj, 255)
        jj_hi = (jj - jj_lo).astype(F32)
        jj_lo = jj_lo.astype(F32)
        k_feat = jnp.where(fk < 3, 1.0, jnp.where(fk < 6, jj_lo, jnp.where(fk < 9, jj_hi, 0.0)))
        kfeat_ref[0] = k_feat.astype(BF16)
        kfeat_ref[1] = (-k_feat).astype(BF16)

    q = q_ref[...]
    lane = lax.broadcasted_iota(jnp.int32, q.shape, 1)
    zero = jnp.zeros_like(q)
    q_feat = qfeat_ref[...]
    q_ops = jnp.concatenate([
        jnp.concatenate([jnp.where(lane < HEAD_DIM, q, zero), q_feat], axis=1),
        jnp.concatenate([jnp.where(lane >= HEAD_DIM, q, zero), q_feat], axis=1)], axis=0)
    ones_rows = jnp.ones((SUM_ROWS, tk), BF16)

    def raw_scores(kb, side):
        start = pl.multiple_of(kb * tk, tk)
        k_ops = jnp.concatenate([k_ref[pl.ds(start, tk), :], kfeat_ref[side]], axis=1)
        return _dot_nt(k_ops, q_ops)

    def offset(kb):
        return slope2 * jnp.abs(q0pos - kb * tk).astype(F32)

    def off_tile(t):
        side = (t >= kd).astype(jnp.int32)
        return t + side, side

    def produce(t, dst_ref):
        kb, side = off_tile(t)
        s = raw_scores(kb, side)
        dst_ref[...] = s
        return jnp.max(s, axis=0, keepdims=True)

    def update(s, mx, c, kb, state):
        m_old, acc_old = state
        m_new = jnp.maximum(m_old, mx - c)
        p = jnp.exp2(s - (m_new + c))
        alpha = jnp.exp2(m_old - m_new)
        vt = vt_ref[:, pl.ds(pl.multiple_of(kb * tk, tk), tk)]
        vt_ops = jnp.concatenate([vt, ones_rows], axis=0)
        return m_new, alpha * acc_old + _dot(vt_ops, p.astype(BF16))

    def consume(t, src_ref, mx, state):
        kb, _ = off_tile(t)
        return update(src_ref[...], mx, offset(kb), kb, state)

    DEPTH = len(s_refs)
    max_trips = (n_off - 1) // DEPTH

    @pl.when(qi == 0)
    def _():
        def tile_norm(j, best):
            kt = k_ref[pl.ds(pl.multiple_of(j * tk, tk), tk), :].astype(F32)
            return jnp.maximum(best, jnp.max(jnp.sum(kt * kt, axis=1, keepdims=True), axis=0, keepdims=True))
        knorm_ref[...] = jnp.broadcast_to(lax.fori_loop(0, nk, tile_norm, jnp.zeros((1, 1), F32)),
                                          knorm_ref.shape)

    c_d = offset(kd)
    s_d = jnp.minimum(raw_scores(kd, 0) - c_d, raw_scores(kd, 1) + c_d)
    state = (jnp.full((1, 2 * tq), -jnp.inf, F32), jnp.zeros((V_DIM + SUM_ROWS, 2 * tq), F32))
    state = update(s_d, jnp.max(s_d, axis=0, keepdims=True), 0.0, kd, state)

    qf = q.astype(F32)
    qn2 = jnp.max(jnp.sum(qf * qf, axis=1, keepdims=True), axis=0, keepdims=True)
    bound = jnp.sqrt(qn2 * knorm_ref[0:1, 0:1]) * 1.001 + 1.0
    m_min = jnp.min(state[0], axis=1, keepdims=True)
    reach = jnp.minimum((bound + UNDERFLOW - m_min) / slope2, 1e9)
    reach = (jnp.ceil(reach).astype(jnp.int32) + 1)[0, 0]
    tk_shift = int(math.log2(tk))
    kb_lo = jnp.minimum(lax.shift_right_logical(jnp.maximum(q0pos + 1 - reach, 0), tk_shift), kd)
    kb_hi = jnp.maximum(jnp.minimum(lax.shift_right_logical(reach + q0pos + tq - 2, tk_shift), nk - 1), kd)
    wanted = jnp.maximum(kb_hi - kb_lo - 1, 0)
    if DEPTH == 2:
        whole = lax.shift_right_logical(wanted + 1, 1)
    else:
        whole = lax.shift_right_logical((wanted + 2) * 43, 7)
    trips = jnp.clip(whole, 1, max_trips)
    first = jnp.minimum(kb_lo, n_off - 1 - DEPTH * trips)

    mx = tuple(produce(first + i, s_refs[i]) for i in range(DEPTH))

    def trip(r, carry, n_produce):
        state, mx = carry[:2], list(carry[2])
        for i in range(DEPTH):
            t = first + DEPTH * r + i
            state = consume(t, s_refs[i], mx[i], state)
            if i < n_produce:
                mx[i] = produce(t + DEPTH, s_refs[i])
        return state + (tuple(mx),)

    carry = lax.fori_loop(0, trips - 1, functools.partial(trip, n_produce=DEPTH), state + (mx,))
    carry = trip(trips - 1, carry, n_produce=1)
    _, acc_fin = consume(first + DEPTH * trips, s_refs[0], carry[2][0], carry[:2])

    lam_init = lam_ref[0]
    one_minus = lam_ref[1]
    lam = (jnp.exp(jnp.sum(lq1_ref[...] * lk1_ref[...], axis=-1, keepdims=True))
           - jnp.exp(jnp.sum(lq2_ref[...] * lk2_ref[...], axis=-1, keepdims=True)) + lam_init)
    o_both = acc_fin[:V_DIM, :] / acc_fin[V_DIM:V_DIM + 1, :]
    o = o_both[:, :tq] - lam * o_both[:, tq:]
    ms = jnp.mean(o * o, axis=0, keepdims=True)
    y = o * lax.rsqrt(ms + EPS) * subln_ref[...] * one_minus
    o_ref[...] = y.T.astype(o_ref.dtype)


def diff_attention(qk, vt, slopes, lam_consts, lq1, lk1, lq2, lk2, subln_col, b, s, tq, tk, depth):
    t = b * s
    nq = s // tq
    nk = s // tk
    assert depth in (2, 3) and depth + 2 <= nk < 126 and (nk - 2) % depth == 0, (s, tk)
    assert tk % tq == 0 and tq <= 256
    hh = ATTN_HEADS
    smem = pl.BlockSpec(memory_space=pltpu.SMEM)
    vec = pl.BlockSpec((1, HEAD_DIM), lambda bi, h, qi: (0, 0))
    return pl.pallas_call(
        functools.partial(_attn_kernel, tq=tq, tk=tk, nk=nk),
        grid=(b, hh, nq),
        in_specs=[
            smem, smem,
            pl.BlockSpec((tq, V_DIM), lambda bi, h, qi: (bi * nq + qi, h)),
            pl.BlockSpec((s, V_DIM), lambda bi, h, qi: (bi, hh + h)),
            pl.BlockSpec((V_DIM, s), lambda bi, h, qi: (h, bi)),
            vec, vec, vec, vec,
            pl.BlockSpec((V_DIM, 1), lambda bi, h, qi: (0, 0)),
        ],
        out_specs=pl.BlockSpec((tq, V_DIM), lambda bi, h, qi: (bi * nq + qi, h)),
        out_shape=jax.ShapeDtypeStruct((t, hh * V_DIM), BF16),
        scratch_shapes=([pltpu.VMEM((2, tk, LANES), BF16), pltpu.VMEM((tq, LANES), BF16),
                         pltpu.VMEM((8, LANES), F32)]
                        + [pltpu.VMEM((tk, 2 * tq), F32)] * depth),
        compiler_params=_params(("parallel", "parallel", "arbitrary")),
        name="diff_attention",
    )(slopes, lam_consts, qk, qk, vt, lq1, lk1, lq2, lk2, subln_col)


def _ssd_kernel(*refs, rev, final, nc):
    if final:
        (xbc_ref, prev_ref, next_ref, dt_ref, dtt_ref, cw_ref, cb_ref, bias_p_ref, bias_c_ref,
         alog_p_ref, alog_c_ref, z_ref, yf_ref, dskip_ref, nw_ref, y_ref, st_ref) = refs
    else:
        (xbc_ref, prev_ref, next_ref, dt_ref, dtt_ref, cw_ref, cb_ref, bias_p_ref, bias_c_ref,
         alog_p_ref, alog_c_ref, y_ref, st_ref) = refs
    direction = 1 if rev else 0
    c = pl.program_id(1)
    cc = (nc - 1 - c) if rev else c
    ll = CHUNK

    @pl.when(c == 0)
    def _():
        st_ref[...] = jnp.zeros_like(st_ref)

    x = xbc_ref[...]
    before = jnp.where(cc == 0, 0.0, prev_ref[...])
    after = jnp.where(cc == nc - 1, 0.0, next_ref[...])
    x_ext = jnp.concatenate([x, after, before], axis=0)
    n_ext = x_ext.shape[0]
    xp = pltpu.roll(x_ext, 1, axis=0)[0:ll, :]
    xn = pltpu.roll(x_ext, n_ext - 1, axis=0)[0:ll, :]
    cw = cw_ref[...]
    u = _silu(cb_ref[...] + xp * cw[0:1, :] + x * cw[1:2, :] + xn * cw[2:3, :])
    xs = u[:, :D_INNER]
    bm = u[:, D_INNER:D_INNER + SSD_GROUPS * D_STATE]
    cm = u[:, D_INNER + SSD_GROUPS * D_STATE:]

    jj = lax.broadcasted_iota(jnp.int32, (DT_PAD, D_INNER), 0)
    col = lax.broadcasted_iota(jnp.int32, (DT_PAD, D_INNER), 1)
    head_of_col = lax.shift_right_logical(col, int(math.log2(SSD_HEAD_DIM)))
    expand = jnp.where(jj == direction * SSD_HEADS + head_of_col, 1.0, 0.0).astype(BF16)
    dt_cols = _softplus(dt_ref[...] + bias_p_ref[...])
    a_cols = dt_cols * (-jnp.exp(alog_p_ref[...]))
    dt_rows = _softplus(dtt_ref[direction * SSD_HEADS:(direction + 1) * SSD_HEADS, :] + bias_c_ref[...])
    a_rows = dt_rows * (-jnp.exp(alog_c_ref[...]))

    ri = lax.broadcasted_iota(jnp.int32, (ll, ll), 0)
    ci = lax.broadcasted_iota(jnp.int32, (ll, ll), 1)
    if rev:
        keep = ci >= ri
        edge = 0
    else:
        keep = ci <= ri
        edge = ll - 1
    tri = jnp.where(keep, 1.0, 0.0).astype(BF16)
    tri_t = jnp.where((ri >= ci) if rev else (ri <= ci), 1.0, 0.0).astype(BF16)
    cum_cols = _dot_01_by_f32(tri, a_cols)
    dt_full = _dot_f32_by_01(dt_cols, expand)
    cum_full = _dot_f32_by_01(cum_cols, expand)
    cum_rows = _dot_f32_by_01(a_rows, tri_t)

    xd = xs * dt_full
    xd_b = xd.astype(BF16)
    cum_edge = cum_full[edge:edge + 1, :]
    xdw = (xd * jnp.exp(cum_edge - cum_full)).astype(BF16)
    grow = jnp.exp(cum_full)
    lane = lax.broadcasted_iota(jnp.int32, (ll, LANES), 1)

    y_parts = []
    for g in range(SSD_GROUPS):
        bg = bm[:, g * D_STATE:(g + 1) * D_STATE]
        cg = cm[:, g * D_STATE:(g + 1) * D_STATE].astype(BF16)
        cb = _dot_nt(cg, bg.astype(BF16))
        gs = slice(g * GROUP_COLS, (g + 1) * GROUP_COLS)
        st_in = st_ref[:, gs]
        y_off = _dot(cg, st_in.astype(BF16)) * grow[:, gs]
        heads_per_group = SSD_HEADS // SSD_GROUPS
        for pair in range(heads_per_group // 2):
            lo = g * GROUP_COLS + pair * LANES
            xd_pair = xd_b[:, lo:lo + LANES]
            outs = []
            for sub in range(2):
                hd = g * heads_per_group + pair * 2 + sub
                seg = cum_full[:, hd * SSD_HEAD_DIM:hd * SSD_HEAD_DIM + 1] - cum_rows[hd:hd + 1, :]
                dec = jnp.exp(jnp.where(keep, seg, -jnp.inf))
                outs.append(_dot((cb * dec).astype(BF16), xd_pair))
            y_diag = jnp.where(lane < SSD_HEAD_DIM, outs[0], outs[1])
            y_parts.append(y_diag + y_off[:, pair * LANES:(pair + 1) * LANES])
        st_chunk = _dot(bg.T.astype(BF16), xdw[:, gs])
        st_ref[:, gs] = st_in * jnp.exp(cum_edge[:, gs]) + st_chunk
    y = jnp.concatenate(y_parts, axis=1)

    if final:
        y = yf_ref[...] + y + xs * dskip_ref[...]
        y = y * _silu(z_ref[...])
        nw = nw_ref[...]
        normed = []
        for g in range(SSD_GROUPS):
            gs = slice(g * GROUP_COLS, (g + 1) * GROUP_COLS)
            normed.append(_rms(y[:, gs], nw[:, gs]))
        y = jnp.concatenate(normed, axis=1)
    y_ref[...] = y.astype(y_ref.dtype)


def ssd_pass(xbc, dt, dtt, cw, cb, bias_p, bias_c, alog_p, alog_c, extras, b, s, rev):
    final = extras is not None
    t = b * s
    nc = s // CHUNK
    rows8 = CHUNK // 8
    last8 = t // 8 - 1

    def cidx(bi, c):
        return bi * nc + ((nc - 1 - c) if rev else c)

    def full(shape):
        return pl.BlockSpec(shape, lambda bi, c: (0, 0))

    chunk_rows = lambda w: pl.BlockSpec((CHUNK, w), lambda bi, c: (cidx(bi, c), 0))
    in_specs = [
        chunk_rows(CONV_DIM),
        pl.BlockSpec((8, CONV_DIM), lambda bi, c: (jnp.maximum(cidx(bi, c) * rows8 - 1, 0), 0)),
        pl.BlockSpec((8, CONV_DIM), lambda bi, c: (jnp.minimum((cidx(bi, c) + 1) * rows8, last8), 0)),
        chunk_rows(DT_PAD),
        pl.BlockSpec((2 * SSD_HEADS, CHUNK), lambda bi, c: (0, cidx(bi, c))),
        full((3, CONV_DIM)), full((1, CONV_DIM)),
        full((1, DT_PAD)), full((SSD_HEADS, 1)), full((1, DT_PAD)), full((SSD_HEADS, 1)),
    ]
    args = [xbc, xbc, xbc, dt, dtt, cw, cb, bias_p, bias_c, alog_p, alog_c]
    if final:
        z, yf, dskip_f, nw = extras
        in_specs += [chunk_rows(D_INNER), chunk_rows(D_INNER), full((1, D_INNER)), full((1, D_INNER))]
        args += [z, yf, dskip_f, nw]
    return pl.pallas_call(
        functools.partial(_ssd_kernel, rev=rev, final=final, nc=nc),
        grid=(b, nc),
        in_specs=in_specs,
        out_specs=chunk_rows(D_INNER),
        out_shape=jax.ShapeDtypeStruct((t, D_INNER), BF16 if final else F32),
        scratch_shapes=[pltpu.VMEM((D_STATE, D_INNER), F32)],
        compiler_params=_params(("parallel", "arbitrary")),
        name="ssd_bwd_final" if final else "ssd_fwd",
    )(*args)


def _merge_kernel(attn_ref, ssd_ref, gates_ref, x_ref, woa_ref, wos_ref, wout_ref, nw_ref, o_ref):
    a = _dot(attn_ref[...], woa_ref[...])
    s = _dot(ssd_ref[...], wos_ref[...])
    gates = gates_ref[...]
    merged = jax.nn.sigmoid(gates[:, :D_MODEL]) * a + jax.nn.sigmoid(gates[:, D_MODEL:]) * s
    mo = _dot(merged.astype(BF16), wout_ref[...])
    o_ref[...] = x_ref[...] + _rms(mo, nw_ref[...])


def merge_out(attn, ssd, gates, x, woa, wos, wout, nw, tm):
    t, d = x.shape
    rows = lambda w: pl.BlockSpec((tm, w), lambda i: (i, 0))
    full = lambda shape: pl.BlockSpec(shape, lambda i: (0, 0))
    return pl.pallas_call(
        _merge_kernel,
        grid=(t // tm,),
        in_specs=[rows(d), rows(d), rows(2 * d), rows(d), full((d, d)), full((d, d)), full((d, d)),
                  full((1, d))],
        out_specs=rows(d),
        out_shape=jax.ShapeDtypeStruct((t, d), F32),
        compiler_params=_params(("parallel",)),
        name="merge_out",
    )(attn, ssd, gates, x, woa, wos, wout, nw)


def _ffn_kernel(x_ref, xp_ref, xn_ref, gpre_ref, wa_ref, wg_ref, cwa_ref, cwg_ref, cba_ref, cbg_ref,
                wd_ref, gpost_ref, o_ref, h_ref, acc_ref, *, tm, tiles_per_seq, nf):
    i = pl.program_id(0)
    f = pl.program_id(1)

    @pl.when(f == 0)
    def _():
        gpre = gpre_ref[...]
        pos = i % tiles_per_seq
        h_ref[0:tm, :] = _rms(x_ref[...], gpre).astype(BF16)
        hp = jnp.where(pos == 0, 0.0, _rms(xp_ref[...], gpre))
        hn = jnp.where(pos == tiles_per_seq - 1, 0.0, _rms(xn_ref[...], gpre))
        h_ref[tm:tm + HALO, :] = hn.astype(BF16)
        h_ref[tm + HALO:tm + 2 * HALO, :] = hp.astype(BF16)
        acc_ref[...] = jnp.zeros_like(acc_ref)

    h = h_ref[...]
    n_ext = tm + 2 * HALO

    def conv_branch(w_ref, cw_ref, cb_ref):
        u = _dot(h, w_ref[...])
        um = u[0:tm, :]
        up = pltpu.roll(u, 1, axis=0)[0:tm, :]
        un = pltpu.roll(u, n_ext - 1, axis=0)[0:tm, :]
        cw = cw_ref[...]
        return cb_ref[...] + up * cw[0:1, :] + um * cw[1:2, :] + un * cw[2:3, :]

    a = conv_branch(wa_ref, cwa_ref, cba_ref)
    g = conv_branch(wg_ref, cwg_ref, cbg_ref)
    act = (_silu(g) * a).astype(BF16)
    acc_ref[...] += _dot(act, wd_ref[...])

    @pl.when(f == nf - 1)
    def _():
        o_ref[...] = x_ref[...] + _rms(acc_ref[...], gpost_ref[...])


def ffn(x, gpre, wa, wg, cwa, cwg, cba, cbg, wd, gpost, s, tm, tf):
    t, d = x.shape
    nf = D_FF_PAD // tf
    tiles_per_seq = s // tm
    blocks = tm // HALO
    last = t // HALO - 1
    full = lambda shape: pl.BlockSpec(shape, lambda i, f: (0, 0))
    colblk = lambda r: pl.BlockSpec((r, tf), lambda i, f: (0, f))
    return pl.pallas_call(
        functools.partial(_ffn_kernel, tm=tm, tiles_per_seq=tiles_per_seq, nf=nf),
        grid=(t // tm, nf),
        in_specs=[
            pl.BlockSpec((tm, d), lambda i, f: (i, 0)),
            pl.BlockSpec((HALO, d), lambda i, f: (jnp.maximum(i * blocks - 1, 0), 0)),
            pl.BlockSpec((HALO, d), lambda i, f: (jnp.minimum((i + 1) * blocks, last), 0)),
            full((1, d)),
            colblk(d), colblk(d), colblk(3), colblk(3), colblk(1), colblk(1),
            pl.BlockSpec((tf, d), lambda i, f: (f, 0)),
            full((1, d)),
        ],
        out_specs=pl.BlockSpec((tm, d), lambda i, f: (i, 0)),
        out_shape=jax.ShapeDtypeStruct((t, d), F32),
        scratch_shapes=[pltpu.VMEM((tm + 2 * HALO, d), BF16), pltpu.VMEM((tm, d), F32)],
        compiler_params=_params(("parallel", "arbitrary")),
        name="ffn",
    )(x, x, x, gpre, wa, wg, cwa, cwg, cba, cbg, wd, gpost)


def _tiles(s):
    return dict(
        tm_proj=min(256, s),
        tq=min(256, s), tk=min(512 if s > 4096 else 256, s), depth=3 if s > 4096 else 2,
        tm_merge=min(256, s),
        tm_ffn=min(1024, s), tf=256,
    )


def _layer(x, b, s, w, cfg):
    g_pre = w["norm_mix_pre"]
    qk, vt, z, xbc, gates, dt, dtt = in_proj(x, g_pre, w, cfg["tm_proj"])

    attn = diff_attention(qk, vt, w["slopes"], w["lam_consts"], w["lam_q1"], w["lam_k1"], w["lam_q2"],
                          w["lam_k2"], w["subln_col"], b, s, cfg["tq"], cfg["tk"], cfg["depth"])

    yf = ssd_pass(xbc, dt, dtt, w["conv_ssd_w"], w["conv_ssd_b"], w["bias_p"], w["bias_c"][0],
                  w["alog_p"], w["alog_c"][0], None, b, s, rev=False)
    ssd = ssd_pass(xbc, dt, dtt, w["conv_ssd_w"], w["conv_ssd_b"], w["bias_p"], w["bias_c"][1],
                   w["alog_p"], w["alog_c"][1], (z, yf, w["dskip_f"], w["ssd_norm"]), b, s, rev=True)

    x = merge_out(attn, ssd, gates, x, w["w_o_attn"], w["w_o_ssd"], w["w_out"], w["norm_mix_post"],
                  cfg["tm_merge"])
    x = ffn(x, w["norm_ffn_pre"], w["w_up_a"], w["w_up_g"], w["cw_a"], w["cw_g"], w["cb_a"], w["cb_g"],
            w["w_down"], w["norm_ffn_post"], s, cfg["tm_ffn"], cfg["tf"])
    return x


def _prepare_weights(norm_mix_pre, norm_mix_post, norm_ffn_pre, norm_ffn_post, w_in, lam_q1, lam_k1,
                     lam_q2, lam_k2, attn_subln, conv_ssd_w, conv_ssd_b, dt_bias, a_log, d_skip, ssd_norm,
                     w_o_attn, w_o_ssd, w_out, w_up, conv_ffn_w, conv_ffn_b, w_down):
    depth = w_in.shape[0]
    qk_cols = ATTN_HEADS * 2 * HEAD_DIM
    attn_w = ATTN_HEADS * V_DIM
    cuts = [0, 2 * qk_cols]
    for width in (attn_w, D_INNER, CONV_DIM, 2 * SSD_HEADS, 2 * D_MODEL):
        cuts.append(cuts[-1] + width)
    seg = lambda i: w_in[:, :, cuts[i]:cuts[i + 1]]
    w_dt = seg(4)
    row = lambda a: a[:, None, :]
    rep = lambda a: jnp.repeat(a, SSD_HEAD_DIM, axis=-1)
    pad_ff = lambda a: jnp.pad(a, [(0, 0)] * (a.ndim - 1) + [(0, D_FF_PAD - D_FF)])
    pad_dt = lambda a: jnp.pad(a.reshape(depth, 1, 2 * SSD_HEADS), ((0, 0), (0, 0), (0, DT_PAD - 2 * SSD_HEADS)))
    lam_init = [0.8 - 0.6 * math.exp(-0.3 * l) for l in range(depth)]
    return dict(
        norm_mix_pre=row(norm_mix_pre), norm_mix_post=row(norm_mix_post),
        norm_ffn_pre=row(norm_ffn_pre), norm_ffn_post=row(norm_ffn_post),
        w_qk=seg(0).astype(BF16),
        w_vt=jnp.swapaxes(seg(1), 1, 2).astype(BF16),
        w_z=seg(2).astype(BF16),
        w_xbc=seg(3).astype(BF16),
        w_dt=jnp.pad(w_dt, ((0, 0), (0, 0), (0, DT_PAD - 2 * SSD_HEADS))).astype(BF16),
        w_dtt=jnp.swapaxes(w_dt, 1, 2).astype(BF16),
        w_gates=seg(5).astype(BF16),
        slopes=jnp.tile(jnp.asarray([2.0 ** (-8.0 * (i + 1) / ATTN_HEADS) for i in range(ATTN_HEADS)],
                                    F32)[None], (depth, 1)),
        lam_consts=jnp.asarray([[li, 1.0 - li] for li in lam_init], F32),
        lam_q1=row(lam_q1), lam_k1=row(lam_k1), lam_q2=row(lam_q2), lam_k2=row(lam_k2),
        subln_col=attn_subln[:, :, None],
        conv_ssd_w=conv_ssd_w, conv_ssd_b=row(conv_ssd_b),
        bias_p=pad_dt(dt_bias), bias_c=dt_bias[..., None],
        alog_p=pad_dt(a_log), alog_c=a_log[..., None],
        dskip_f=row(rep(d_skip)), ssd_norm=row(ssd_norm),
        w_o_attn=w_o_attn.astype(BF16), w_o_ssd=w_o_ssd.astype(BF16), w_out=w_out.astype(BF16),
        w_up_a=pad_ff(w_up[:, :, :D_FF]).astype(BF16), w_up_g=pad_ff(w_up[:, :, D_FF:]).astype(BF16),
        cw_a=pad_ff(conv_ffn_w[:, :, :D_FF]), cw_g=pad_ff(conv_ffn_w[:, :, D_FF:]),
        cb_a=row(pad_ff(conv_ffn_b[:, :D_FF])), cb_g=row(pad_ff(conv_ffn_b[:, D_FF:])),
        w_down=jnp.pad(w_down, ((0, 0), (0, D_FF_PAD - D_FF), (0, 0))).astype(BF16),
    )


def kernel(x_prompt, x_sample, norm_mix_pre, norm_mix_post, norm_ffn_pre, norm_ffn_post, w_in, lam_q1, lam_k1, lam_q2, lam_k2, attn_subln, conv_ssd_w, conv_ssd_b, dt_bias, a_log, d_skip, ssd_norm, w_o_attn, w_o_ssd, w_out, w_up, conv_ffn_w, conv_ffn_b, w_down):
    weights = _prepare_weights(norm_mix_pre, norm_mix_post, norm_ffn_pre, norm_ffn_post, w_in, lam_q1,
                               lam_k1, lam_q2, lam_k2, attn_subln, conv_ssd_w, conv_ssd_b, dt_bias, a_log,
                               d_skip, ssd_norm, w_o_attn, w_o_ssd, w_out, w_up, conv_ffn_w, conv_ffn_b,
                               w_down)
    groups = []
    for xg in (x_prompt, x_sample):
        b, s, d = xg.shape
        groups.append((b, s, _tiles(s)))

    def step(carry, w):
        out = tuple(_layer(x, b, s, w, cfg) for x, (b, s, cfg) in zip(carry, groups))
        return out, None

    init = tuple(xg.reshape(-1, xg.shape[-1]) for xg in (x_prompt, x_sample))
    out, _ = lax.scan(step, init, weights)
    return tuple(o.reshape(xg.shape) for o, xg in zip(out, (x_prompt, x_sample)))
```

```python
import functools
import math

import jax
import jax.numpy as jnp
from jax import lax
from jax.experimental import pallas as pl
from jax.experimental.pallas import tpu as pltpu

F32 = jnp.float32
BF16 = jnp.bfloat16

D_MODEL = 1024
ATTN_HEADS = 8
HEAD_DIM = 64
V_DIM = 2 * HEAD_DIM
SSD_HEADS = 16
SSD_HEAD_DIM = 64
D_INNER = SSD_HEADS * SSD_HEAD_DIM
SSD_GROUPS = 2
GROUP_COLS = D_INNER // SSD_GROUPS
D_STATE = 128
CHUNK = 128
CONV_DIM = D_INNER + 2 * SSD_GROUPS * D_STATE
D_FF = 2752
EPS = 1e-6

LANES = 128
D_FF_PAD = 2816
DT_PAD = LANES
HALO = 16
PROJ_CHUNK = 512

UNDERFLOW = 150.0
SUM_ROWS = 16
LOG2E = math.log2(math.e)
Q_PRESCALE = LOG2E / math.sqrt(HEAD_DIM)
VMEM_LIMIT = 56 * 1024 * 1024


def _params(sem):
    return pltpu.CompilerParams(dimension_semantics=sem, vmem_limit_bytes=VMEM_LIMIT)


def _rms(x, g):
    ms = jnp.mean(x * x, axis=-1, keepdims=True)
    return x * lax.rsqrt(ms + EPS) * g


def _dot(a, b):
    return jnp.dot(a, b, preferred_element_type=F32)


def _dot_nt(a, b):
    return lax.dot_general(a, b, (((1,), (1,)), ((), ())), preferred_element_type=F32)


def _bf16_terms(x):
    def top(v):
        bits = lax.bitcast_convert_type(v, jnp.uint32) & jnp.uint32(0xFFFF0000)
        return lax.bitcast_convert_type(bits, F32)

    hi = top(x)
    mid = top(x - hi)
    lo = x - hi - mid
    return hi, mid, lo


def _dot_f32_by_01(x, m01):
    return sum(_dot(term.astype(BF16), m01) for term in _bf16_terms(x))


def _dot_01_by_f32(m01, x):
    return sum(_dot(m01, term.astype(BF16)) for term in _bf16_terms(x))


def _softplus(x):
    return jnp.maximum(x, 0.0) + jnp.log1p(jnp.exp(-jnp.abs(x)))


def _silu(x):
    return x * jax.nn.sigmoid(x)


def _in_proj_kernel(x_ref, g_ref, wqk_ref, wvt_ref, wz_ref, wxbc_ref, wg_ref, wdt_ref, wdtt_ref,
                    qk_ref, vt_ref, z_ref, xbc_ref, gates_ref, dt_ref, dtt_ref):
    h = _rms(x_ref[...], g_ref[...]).astype(BF16)
    qk_cols = ATTN_HEADS * 2 * HEAD_DIM

    def project(w_ref, o_ref, scale_upto=0):
        n = w_ref.shape[1]
        for c0 in range(0, n, PROJ_CHUNK):
            c1 = min(c0 + PROJ_CHUNK, n)
            y = _dot(h, w_ref[:, c0:c1])
            if c1 <= scale_upto:
                y = y * Q_PRESCALE
            o_ref[:, c0:c1] = y.astype(o_ref.dtype)

    project(wqk_ref, qk_ref, scale_upto=qk_cols)
    for r0 in range(0, wvt_ref.shape[0], PROJ_CHUNK):
        vt_ref[r0:r0 + PROJ_CHUNK, :] = _dot_nt(wvt_ref[r0:r0 + PROJ_CHUNK, :], h).astype(vt_ref.dtype)
    project(wz_ref, z_ref)
    project(wxbc_ref, xbc_ref)
    project(wg_ref, gates_ref)
    project(wdt_ref, dt_ref)
    dtt_ref[...] = _dot_nt(wdtt_ref[...], h)


def in_proj(x, g, w, tm):
    t, d = x.shape
    weights = [w["w_qk"], w["w_vt"], w["w_z"], w["w_xbc"], w["w_gates"], w["w_dt"], w["w_dtt"]]
    rows = lambda n: pl.BlockSpec((tm, n), lambda i: (i, 0))
    cols = lambda n: pl.BlockSpec((n, tm), lambda i: (0, i))
    resident = lambda a: pl.BlockSpec(a.shape, lambda i: (0, 0), pipeline_mode=pl.Buffered(1))
    n_qk, n_v, n_z, n_xbc, n_g = (w["w_qk"].shape[1], w["w_vt"].shape[0], w["w_z"].shape[1],
                                  w["w_xbc"].shape[1], w["w_gates"].shape[1])
    n_dtt = w["w_dtt"].shape[0]
    return pl.pallas_call(
        _in_proj_kernel,
        grid=(t // tm,),
        in_specs=[rows(d), resident(g)] + [resident(a) for a in weights],
        out_specs=[rows(n_qk), cols(n_v), rows(n_z), rows(n_xbc), rows(n_g), rows(DT_PAD), cols(n_dtt)],
        out_shape=[
            jax.ShapeDtypeStruct((t, n_qk), BF16), jax.ShapeDtypeStruct((n_v, t), BF16),
            jax.ShapeDtypeStruct((t, n_z), F32), jax.ShapeDtypeStruct((t, n_xbc), F32),
            jax.ShapeDtypeStruct((t, n_g), F32), jax.ShapeDtypeStruct((t, DT_PAD), F32),
            jax.ShapeDtypeStruct((n_dtt, t), F32),
        ],
        compiler_params=_params(("parallel",)),
        name="in_proj",
    )(x, g, *weights)


def _attn_kernel(slopes_ref, lam_ref, q_ref, k_ref, vt_ref, lq1_ref, lk1_ref, lq2_ref, lk2_ref,
                 subln_ref, o_ref, kfeat_ref, qfeat_ref, knorm_ref, *s_refs, tq, tk, nk):
    h = pl.program_id(1)
    qi = pl.program_id(2)
    slope2 = slopes_ref[h] * LOG2E
    q0pos = qi * tq
    n_off = nk - 1
    kd = lax.shift_right_logical(qi, int(math.log2(tk // tq)))

    @pl.when(qi == 0)
    def _():
        ii = lax.broadcasted_iota(jnp.int32, (tq, LANES), 0).astype(F32)
        fq = lax.broadcasted_iota(jnp.int32, (tq, LANES), 1)
        row_terms = _bf16_terms(-slope2 * ii)
        slope_terms = _bf16_terms(jnp.full((tq, LANES), slope2, F32))
        q_feat = jnp.zeros((tq, LANES), F32)
        for n in range(3):
            q_feat = jnp.where(fq == n, row_terms[n], q_feat)
            q_feat = jnp.where((fq == 3 + n) | (fq == 6 + n), slope_terms[n], q_feat)
        qfeat_ref[...] = q_feat.astype(BF16)
        jj = lax.broadcasted_iota(jnp.int32, (tk, LANES), 0)
        fk = lax.broadcasted_iota(jnp.int32, (tk, LANES), 1)
        jj_lo = jnp.bitwise_and(jj, 255)
        jj_hi = (jj - jj_lo).astype(F32)
        jj_lo = jj_lo.astype(F32)
        k_feat = jnp.where(fk < 3, 1.0, jnp.where(fk < 6, jj_lo, jnp.where(fk < 9, jj_hi, 0.0)))
        kfeat_ref[0] = k_feat.astype(BF16)
        kfeat_ref[1] = (-k_feat).astype(BF16)

    q = q_ref[...]
    lane = lax.broadcasted_iota(jnp.int32, q.shape, 1)
    zero = jnp.zeros_like(q)
    q_feat = qfeat_ref[...]
    q_ops = jnp.concatenate([
        jnp.concatenate([jnp.where(lane < HEAD_DIM, q, zero), q_feat], axis=1),
        jnp.concatenate([jnp.where(lane >= HEAD_DIM, q, zero), q_feat], axis=1)], axis=0)
    ones_rows = jnp.ones((SUM_ROWS, tk), BF16)

    def raw_scores(kb, side):
        start = pl.multiple_of(kb * tk, tk)
        k_ops = jnp.concatenate([k_ref[pl.ds(start, tk), :], kfeat_ref[side]], axis=1)
        return _dot_nt(k_ops, q_ops)

    def offset(kb):
        return slope2 * jnp.abs(q0pos - kb * tk).astype(F32)

    def off_tile(t):
        side = (t >= kd).astype(jnp.int32)
        return t + side, side

    def produce(t, dst_ref):
        kb, side = off_tile(t)
        s = raw_scores(kb, side)
        dst_ref[...] = s
        return jnp.max(s, axis=0, keepdims=True)

    def update(s, mx, c, kb, state):
        m_old, acc_old = state
        m_new = jnp.maximum(m_old, mx - c)
        p = jnp.exp2(s - (m_new + c))
        alpha = jnp.exp2(m_old - m_new)
        vt = vt_ref[:, pl.ds(pl.multiple_of(kb * tk, tk), tk)]
        vt_ops = jnp.concatenate([vt, ones_rows], axis=0)
        return m_new, alpha * acc_old + _dot(vt_ops, p.astype(BF16))

    def consume(t, src_ref, mx, state):
        kb, _ = off_tile(t)
        return update(src_ref[...], mx, offset(kb), kb, state)

    DEPTH = len(s_refs)
    max_trips = (n_off - 1) // DEPTH

    @pl.when(qi == 0)
    def _():
        def tile_norm(j, best):
            kt = k_ref[pl.ds(pl.multiple_of(j * tk, tk), tk), :].astype(F32)
            return jnp.maximum(best, jnp.max(jnp.sum(kt * kt, axis=1, keepdims=True), axis=0, keepdims=True))
        knorm_ref[...] = jnp.broadcast_to(lax.fori_loop(0, nk, tile_norm, jnp.zeros((1, 1), F32)),
                                          knorm_ref.shape)

    c_d = offset(kd)
    s_d = jnp.minimum(raw_scores(kd, 0) - c_d, raw_scores(kd, 1) + c_d)
    state = (jnp.full((1, 2 * tq), -jnp.inf, F32), jnp.zeros((V_DIM + SUM_ROWS, 2 * tq), F32))
    state = update(s_d, jnp.max(s_d, axis=0, keepdims=True), 0.0, kd, state)

    qf = q.astype(F32)
    qn2 = jnp.max(jnp.sum(qf * qf, axis=1, keepdims=True), axis=0, keepdims=True)
    bound = jnp.sqrt(qn2 * knorm_ref[0:1, 0:1]) * 1.001 + 1.0
    m_min = jnp.min(state[0], axis=1, keepdims=True)
    reach = jnp.minimum((bound + UNDERFLOW - m_min) / slope2, 1e9)
    reach = (jnp.ceil(reach).astype(jnp.int32) + 1)[0, 0]
    tk_shift = int(math.log2(tk))
    kb_lo = jnp.minimum(lax.shift_right_logical(jnp.maximum(q0pos + 1 - reach, 0), tk_shift), kd)
    kb_hi = jnp.maximum(jnp.minimum(lax.shift_right_logical(reach + q0pos + tq - 2, tk_shift), nk - 1), kd)
    wanted = jnp.maximum(kb_hi - kb_lo - 1, 0)
    if DEPTH == 2:
        whole = lax.shift_right_logical(wanted + 1, 1)
    else:
        whole = lax.shift_right_logical((wanted + 2) * 43, 7)
    trips = jnp.clip(whole, 1, max_trips)
    first = jnp.minimum(kb_lo, n_off - 1 - DEPTH * trips)

    mx = tuple(produce(first + i, s_refs[i]) for i in range(DEPTH))

    def trip(r, carry, n_produce):
        state, mx = carry[:2], list(carry[2])
        for i in range(DEPTH):
            t = first + DEPTH * r + i
            state = consume(t, s_refs[i], mx[i], state)
            if i < n_produce:
                mx[i] = produce(t + DEPTH, s_refs[i])
        return state + (tuple(mx),)

    full_trips = trips - 1
    pairs = lax.shift_right_logical(full_trips, 1)
    carry = lax.fori_loop(0, pairs, lambda r2, c: trip(2 * r2 + 1, trip(2 * r2, c, DEPTH), DEPTH),
                          state + (mx,))
    carry = lax.fori_loop(2 * pairs, full_trips, functools.partial(trip, n_produce=DEPTH), carry)
    carry = trip(trips - 1, carry, n_produce=1)
    _, acc_fin = consume(first + DEPTH * trips, s_refs[0], carry[2][0], carry[:2])

    lam_init = lam_ref[0]
    one_minus = lam_ref[1]
    lam = (jnp.exp(jnp.sum(lq1_ref[...] * lk1_ref[...], axis=-1, keepdims=True))
           - jnp.exp(jnp.sum(lq2_ref[...] * lk2_ref[...], axis=-1, keepdims=True)) + lam_init)
    o_both = acc_fin[:V_DIM, :] / acc_fin[V_DIM:V_DIM + 1, :]
    o = o_both[:, :tq] - lam * o_both[:, tq:]
    ms = jnp.mean(o * o, axis=0, keepdims=True)
    y = o * lax.rsqrt(ms + EPS) * subln_ref[...] * one_minus
    o_ref[...] = y.T.astype(o_ref.dtype)


def diff_attention(qk, vt, slopes, lam_consts, lq1, lk1, lq2, lk2, subln_col, b, s, tq, tk, depth):
    t = b * s
    nq = s // tq
    nk = s // tk
    assert depth in (2, 3) and depth + 2 <= nk < 126 and (nk - 2) % depth == 0, (s, tk)
    assert tk % tq == 0 and tq <= 256
    hh = ATTN_HEADS
    smem = pl.BlockSpec(memory_space=pltpu.SMEM)
    vec = pl.BlockSpec((1, HEAD_DIM), lambda bi, h, qi: (0, 0))
    return pl.pallas_call(
        functools.partial(_attn_kernel, tq=tq, tk=tk, nk=nk),
        grid=(b, hh, nq),
        in_specs=[
            smem, smem,
            pl.BlockSpec((tq, V_DIM), lambda bi, h, qi: (bi * nq + qi, h)),
            pl.BlockSpec((s, V_DIM), lambda bi, h, qi: (bi, hh + h)),
            pl.BlockSpec((V_DIM, s), lambda bi, h, qi: (h, bi)),
            vec, vec, vec, vec,
            pl.BlockSpec((V_DIM, 1), lambda bi, h, qi: (0, 0)),
        ],
        out_specs=pl.BlockSpec((tq, V_DIM), lambda bi, h, qi: (bi * nq + qi, h)),
        out_shape=jax.ShapeDtypeStruct((t, hh * V_DIM), BF16),
        scratch_shapes=([pltpu.VMEM((2, tk, LANES), BF16), pltpu.VMEM((tq, LANES), BF16),
                         pltpu.VMEM((8, LANES), F32)]
                        + [pltpu.VMEM((tk, 2 * tq), F32)] * depth),
        compiler_params=_params(("parallel", "parallel", "arbitrary")),
        name="diff_attention",
    )(slopes, lam_consts, qk, qk, vt, lq1, lk1, lq2, lk2, subln_col)


def _ssd_kernel(*refs, rev, final, nc):
    if final:
        (xbc_ref, prev_ref, next_ref, dt_ref, dtt_ref, cw_ref, cb_ref, bias_p_ref, bias_c_ref,
         alog_p_ref, alog_c_ref, z_ref, yf_ref, dskip_ref, nw_ref, y_ref, st_ref) = refs
    else:
        (xbc_ref, prev_ref, next_ref, dt_ref, dtt_ref, cw_ref, cb_ref, bias_p_ref, bias_c_ref,
         alog_p_ref, alog_c_ref, y_ref, st_ref) = refs
    direction = 1 if rev else 0
    c = pl.program_id(1)
    cc = (nc - 1 - c) if rev else c
    ll = CHUNK

    @pl.when(c == 0)
    def _():
        st_ref[...] = jnp.zeros_like(st_ref)

    x = xbc_ref[...]
    before = jnp.where(cc == 0, 0.0, prev_ref[...])
    after = jnp.where(cc == nc - 1, 0.0, next_ref[...])
    x_ext = jnp.concatenate([x, after, before], axis=0)
    n_ext = x_ext.shape[0]
    xp = pltpu.roll(x_ext, 1, axis=0)[0:ll, :]
    xn = pltpu.roll(x_ext, n_ext - 1, axis=0)[0:ll, :]
    cw = cw_ref[...]
    u = _silu(cb_ref[...] + xp * cw[0:1, :] + x * cw[1:2, :] + xn * cw[2:3, :])
    xs = u[:, :D_INNER]
    bm = u[:, D_INNER:D_INNER + SSD_GROUPS * D_STATE]
    cm = u[:, D_INNER + SSD_GROUPS * D_STATE:]

    jj = lax.broadcasted_iota(jnp.int32, (DT_PAD, D_INNER), 0)
    col = lax.broadcasted_iota(jnp.int32, (DT_PAD, D_INNER), 1)
    head_of_col = lax.shift_right_logical(col, int(math.log2(SSD_HEAD_DIM)))
    expand = jnp.where(jj == direction * SSD_HEADS + head_of_col, 1.0, 0.0).astype(BF16)
    dt_cols = _softplus(dt_ref[...] + bias_p_ref[...])
    a_cols = dt_cols * (-jnp.exp(alog_p_ref[...]))
    dt_rows = _softplus(dtt_ref[direction * SSD_HEADS:(direction + 1) * SSD_HEADS, :] + bias_c_ref[...])
    a_rows = dt_rows * (-jnp.exp(alog_c_ref[...]))

    ri = lax.broadcasted_iota(jnp.int32, (ll, ll), 0)
    ci = lax.broadcasted_iota(jnp.int32, (ll, ll), 1)
    if rev:
        keep = ci >= ri
        edge = 0
    else:
        keep = ci <= ri
        edge = ll - 1
    tri = jnp.where(keep, 1.0, 0.0).astype(BF16)
    tri_t = jnp.where((ri >= ci) if rev else (ri <= ci), 1.0, 0.0).astype(BF16)
    cum_cols = _dot_01_by_f32(tri, a_cols)
    dt_full = _dot_f32_by_01(dt_cols, expand)
    cum_full = _dot_f32_by_01(cum_cols, expand)
    cum_rows = _dot_f32_by_01(a_rows, tri_t)

    xd = xs * dt_full
    xd_b = xd.astype(BF16)
    cum_edge = cum_full[edge:edge + 1, :]
    xdw = (xd * jnp.exp(cum_edge - cum_full)).astype(BF16)
    grow = jnp.exp(cum_full)
    lane = lax.broadcasted_iota(jnp.int32, (ll, LANES), 1)

    y_parts = []
    for g in range(SSD_GROUPS):
        bg = bm[:, g * D_STATE:(g + 1) * D_STATE]
        cg = cm[:, g * D_STATE:(g + 1) * D_STATE].astype(BF16)
        cb = _dot_nt(cg, bg.astype(BF16))
        gs = slice(g * GROUP_COLS, (g + 1) * GROUP_COLS)
        st_in = st_ref[:, gs]
        y_off = _dot(cg, st_in.astype(BF16)) * grow[:, gs]
        heads_per_group = SSD_HEADS // SSD_GROUPS
        for pair in range(heads_per_group // 2):
            lo = g * GROUP_COLS + pair * LANES
            xd_pair = xd_b[:, lo:lo + LANES]
            outs = []
            for sub in range(2):
                hd = g * heads_per_group + pair * 2 + sub
                seg = cum_full[:, hd * SSD_HEAD_DIM:hd * SSD_HEAD_DIM + 1] - cum_rows[hd:hd + 1, :]
                dec = jnp.exp(jnp.where(keep, seg, -jnp.inf))
                outs.append(_dot((cb * dec).astype(BF16), xd_pair))
            y_diag = jnp.where(lane < SSD_HEAD_DIM, outs[0], outs[1])
            y_parts.append(y_diag + y_off[:, pair * LANES:(pair + 1) * LANES])
        st_chunk = _dot(bg.T.astype(BF16), xdw[:, gs])
        st_ref[:, gs] = st_in * jnp.exp(cum_edge[:, gs]) + st_chunk
    y = jnp.concatenate(y_parts, axis=1)

    if final:
        y = yf_ref[...] + y + xs * dskip_ref[...]
        y = y * _silu(z_ref[...])
        nw = nw_ref[...]
        normed = []
        for g in range(SSD_GROUPS):
            gs = slice(g * GROUP_COLS, (g + 1) * GROUP_COLS)
            normed.append(_rms(y[:, gs], nw[:, gs]))
        y = jnp.concatenate(normed, axis=1)
    y_ref[...] = y.astype(y_ref.dtype)


def ssd_pass(xbc, dt, dtt, cw, cb, bias_p, bias_c, alog_p, alog_c, extras, b, s, rev):
    final = extras is not None
    t = b * s
    nc = s // CHUNK
    rows8 = CHUNK // 8
    last8 = t // 8 - 1

    def cidx(bi, c):
        return bi * nc + ((nc - 1 - c) if rev else c)

    def full(shape):
        return pl.BlockSpec(shape, lambda bi, c: (0, 0))

    chunk_rows = lambda w: pl.BlockSpec((CHUNK, w), lambda bi, c: (cidx(bi, c), 0))
    in_specs = [
        chunk_rows(CONV_DIM),
        pl.BlockSpec((8, CONV_DIM), lambda bi, c: (jnp.maximum(cidx(bi, c) * rows8 - 1, 0), 0)),
        pl.BlockSpec((8, CONV_DIM), lambda bi, c: (jnp.minimum((cidx(bi, c) + 1) * rows8, last8), 0)),
        chunk_rows(DT_PAD),
        pl.BlockSpec((2 * SSD_HEADS, CHUNK), lambda bi, c: (0, cidx(bi, c))),
        full((3, CONV_DIM)), full((1, CONV_DIM)),
        full((1, DT_PAD)), full((SSD_HEADS, 1)), full((1, DT_PAD)), full((SSD_HEADS, 1)),
    ]
    args = [xbc, xbc, xbc, dt, dtt, cw, cb, bias_p, bias_c, alog_p, alog_c]
    if final:
        z, yf, dskip_f, nw = extras
        in_specs += [chunk_rows(D_INNER), chunk_rows(D_INNER), full((1, D_INNER)), full((1, D_INNER))]
        args += [z, yf, dskip_f, nw]
    return pl.pallas_call(
        functools.partial(_ssd_kernel, rev=rev, final=final, nc=nc),
        grid=(b, nc),
        in_specs=in_specs,
        out_specs=chunk_rows(D_INNER),
        out_shape=jax.ShapeDtypeStruct((t, D_INNER), BF16 if final else F32),
        scratch_shapes=[pltpu.VMEM((D_STATE, D_INNER), F32)],
        compiler_params=_params(("parallel", "arbitrary")),
        name="ssd_bwd_final" if final else "ssd_fwd",
    )(*args)


def _merge_kernel(attn_ref, ssd_ref, gates_ref, x_ref, woa_ref, wos_ref, wout_ref, nw_ref, o_ref):
    a = _dot(attn_ref[...], woa_ref[...])
    s = _dot(ssd_ref[...], wos_ref[...])
    gates = gates_ref[...]
    merged = jax.nn.sigmoid(gates[:, :D_MODEL]) * a + jax.nn.sigmoid(gates[:, D_MODEL:]) * s
    mo = _dot(merged.astype(BF16), wout_ref[...])
    o_ref[...] = x_ref[...] + _rms(mo, nw_ref[...])


def merge_out(attn, ssd, gates, x, woa, wos, wout, nw, tm):
    t, d = x.shape
    rows = lambda w: pl.BlockSpec((tm, w), lambda i: (i, 0))
    full = lambda shape: pl.BlockSpec(shape, lambda i: (0, 0))
    return pl.pallas_call(
        _merge_kernel,
        grid=(t // tm,),
        in_specs=[rows(d), rows(d), rows(2 * d), rows(d), full((d, d)), full((d, d)), full((d, d)),
                  full((1, d))],
        out_specs=rows(d),
        out_shape=jax.ShapeDtypeStruct((t, d), F32),
        compiler_params=_params(("parallel",)),
        name="merge_out",
    )(attn, ssd, gates, x, woa, wos, wout, nw)


def _ffn_kernel(x_ref, xp_ref, xn_ref, gpre_ref, wa_ref, wg_ref, cwa_ref, cwg_ref, cba_ref, cbg_ref,
                wd_ref, gpost_ref, o_ref, h_ref, acc_ref, *, tm, tiles_per_seq, nf):
    i = pl.program_id(0)
    f = pl.program_id(1)

    @pl.when(f == 0)
    def _():
        gpre = gpre_ref[...]
        pos = i % tiles_per_seq
        h_ref[0:tm, :] = _rms(x_ref[...], gpre).astype(BF16)
        hp = jnp.where(pos == 0, 0.0, _rms(xp_ref[...], gpre))
        hn = jnp.where(pos == tiles_per_seq - 1, 0.0, _rms(xn_ref[...], gpre))
        h_ref[tm:tm + HALO, :] = hn.astype(BF16)
        h_ref[tm + HALO:tm + 2 * HALO, :] = hp.astype(BF16)
        acc_ref[...] = jnp.zeros_like(acc_ref)

    h = h_ref[...]
    n_ext = tm + 2 * HALO

    def conv_branch(w_ref, cw_ref, cb_ref):
        u = _dot(h, w_ref[...])
        um = u[0:tm, :]
        up = pltpu.roll(u, 1, axis=0)[0:tm, :]
        un = pltpu.roll(u, n_ext - 1, axis=0)[0:tm, :]
        cw = cw_ref[...]
        return cb_ref[...] + up * cw[0:1, :] + um * cw[1:2, :] + un * cw[2:3, :]

    a = conv_branch(wa_ref, cwa_ref, cba_ref)
    g = conv_branch(wg_ref, cwg_ref, cbg_ref)
    act = (_silu(g) * a).astype(BF16)
    acc_ref[...] += _dot(act, wd_ref[...])

    @pl.when(f == nf - 1)
    def _():
        o_ref[...] = x_ref[...] + _rms(acc_ref[...], gpost_ref[...])


def ffn(x, gpre, wa, wg, cwa, cwg, cba, cbg, wd, gpost, s, tm, tf):
    t, d = x.shape
    nf = D_FF_PAD // tf
    tiles_per_seq = s // tm
    blocks = tm // HALO
    last = t // HALO - 1
    full = lambda shape: pl.BlockSpec(shape, lambda i, f: (0, 0))
    colblk = lambda r: pl.BlockSpec((r, tf), lambda i, f: (0, f))
    return pl.pallas_call(
        functools.partial(_ffn_kernel, tm=tm, tiles_per_seq=tiles_per_seq, nf=nf),
        grid=(t // tm, nf),
        in_specs=[
            pl.BlockSpec((tm, d), lambda i, f: (i, 0)),
            pl.BlockSpec((HALO, d), lambda i, f: (jnp.maximum(i * blocks - 1, 0), 0)),
            pl.BlockSpec((HALO, d), lambda i, f: (jnp.minimum((i + 1) * blocks, last), 0)),
            full((1, d)),
            colblk(d), colblk(d), colblk(3), colblk(3), colblk(1), colblk(1),
            pl.BlockSpec((tf, d), lambda i, f: (f, 0)),
            full((1, d)),
        ],
        out_specs=pl.BlockSpec((tm, d), lambda i, f: (i, 0)),
        out_shape=jax.ShapeDtypeStruct((t, d), F32),
        scratch_shapes=[pltpu.VMEM((tm + 2 * HALO, d), BF16), pltpu.VMEM((tm, d), F32)],
        compiler_params=_params(("parallel", "arbitrary")),
        name="ffn",
    )(x, x, x, gpre, wa, wg, cwa, cwg, cba, cbg, wd, gpost)


def _tiles(s):
    return dict(
        tm_proj=min(256, s),
        tq=min(256, s), tk=min(512, s), depth=3,
        tm_merge=min(256, s),
        tm_ffn=min(1024, s), tf=256,
    )


def _layer(x, b, s, w, cfg):
    g_pre = w["norm_mix_pre"]
    qk, vt, z, xbc, gates, dt, dtt = in_proj(x, g_pre, w, cfg["tm_proj"])

    attn = diff_attention(qk, vt, w["slopes"], w["lam_consts"], w["lam_q1"], w["lam_k1"], w["lam_q2"],
                          w["lam_k2"], w["subln_col"], b, s, cfg["tq"], cfg["tk"], cfg["depth"])

    yf = ssd_pass(xbc, dt, dtt, w["conv_ssd_w"], w["conv_ssd_b"], w["bias_p"], w["bias_c"][0],
                  w["alog_p"], w["alog_c"][0], None, b, s, rev=False)
    ssd = ssd_pass(xbc, dt, dtt, w["conv_ssd_w"], w["conv_ssd_b"], w["bias_p"], w["bias_c"][1],
                   w["alog_p"], w["alog_c"][1], (z, yf, w["dskip_f"], w["ssd_norm"]), b, s, rev=True)

    x = merge_out(attn, ssd, gates, x, w["w_o_attn"], w["w_o_ssd"], w["w_out"], w["norm_mix_post"],
                  cfg["tm_merge"])
    x = ffn(x, w["norm_ffn_pre"], w["w_up_a"], w["w_up_g"], w["cw_a"], w["cw_g"], w["cb_a"], w["cb_g"],
            w["w_down"], w["norm_ffn_post"], s, cfg["tm_ffn"], cfg["tf"])
    return x


def _prepare_weights(norm_mix_pre, norm_mix_post, norm_ffn_pre, norm_ffn_post, w_in, lam_q1, lam_k1,
                     lam_q2, lam_k2, attn_subln, conv_ssd_w, conv_ssd_b, dt_bias, a_log, d_skip, ssd_norm,
                     w_o_attn, w_o_ssd, w_out, w_up, conv_ffn_w, conv_ffn_b, w_down):
    depth = w_in.shape[0]
    qk_cols = ATTN_HEADS * 2 * HEAD_DIM
    attn_w = ATTN_HEADS * V_DIM
    cuts = [0, 2 * qk_cols]
    for width in (attn_w, D_INNER, CONV_DIM, 2 * SSD_HEADS, 2 * D_MODEL):
        cuts.append(cuts[-1] + width)
    seg = lambda i: w_in[:, :, cuts[i]:cuts[i + 1]]
    w_dt = seg(4)
    row = lambda a: a[:, None, :]
    rep = lambda a: jnp.repeat(a, SSD_HEAD_DIM, axis=-1)
    pad_ff = lambda a: jnp.pad(a, [(0, 0)] * (a.ndim - 1) + [(0, D_FF_PAD - D_FF)])
    pad_dt = lambda a: jnp.pad(a.reshape(depth, 1, 2 * SSD_HEADS), ((0, 0), (0, 0), (0, DT_PAD - 2 * SSD_HEADS)))
    lam_init = [0.8 - 0.6 * math.exp(-0.3 * l) for l in range(depth)]
    return dict(
        norm_mix_pre=row(norm_mix_pre), norm_mix_post=row(norm_mix_post),
        norm_ffn_pre=row(norm_ffn_pre), norm_ffn_post=row(norm_ffn_post),
        w_qk=seg(0).astype(BF16),
        w_vt=jnp.swapaxes(seg(1), 1, 2).astype(BF16),
        w_z=seg(2).astype(BF16),
        w_xbc=seg(3).astype(BF16),
        w_dt=jnp.pad(w_dt, ((0, 0), (0, 0), (0, DT_PAD - 2 * SSD_HEADS))).astype(BF16),
        w_dtt=jnp.swapaxes(w_dt, 1, 2).astype(BF16),
        w_gates=seg(5).astype(BF16),
        slopes=jnp.tile(jnp.asarray([2.0 ** (-8.0 * (i + 1) / ATTN_HEADS) for i in range(ATTN_HEADS)],
                                    F32)[None], (depth, 1)),
        lam_consts=jnp.asarray([[li, 1.0 - li] for li in lam_init], F32),
        lam_q1=row(lam_q1), lam_k1=row(lam_k1), lam_q2=row(lam_q2), lam_k2=row(lam_k2),
        subln_col=attn_subln[:, :, None],
        conv_ssd_w=conv_ssd_w, conv_ssd_b=row(conv_ssd_b),
        bias_p=pad_dt(dt_bias), bias_c=dt_bias[..., None],
        alog_p=pad_dt(a_log), alog_c=a_log[..., None],
        dskip_f=row(rep(d_skip)), ssd_norm=row(ssd_norm),
        w_o_attn=w_o_attn.astype(BF16), w_o_ssd=w_o_ssd.astype(BF16), w_out=w_out.astype(BF16),
        w_up_a=pad_ff(w_up[:, :, :D_FF]).astype(BF16), w_up_g=pad_ff(w_up[:, :, D_FF:]).astype(BF16),
        cw_a=pad_ff(conv_ffn_w[:, :, :D_FF]), cw_g=pad_ff(conv_ffn_w[:, :, D_FF:]),
        cb_a=row(pad_ff(conv_ffn_b[:, :D_FF])), cb_g=row(pad_ff(conv_ffn_b[:, D_FF:])),
        w_down=jnp.pad(w_down, ((0, 0), (0, D_FF_PAD - D_FF), (0, 0))).astype(BF16),
    )


def kernel(x_prompt, x_sample, norm_mix_pre, norm_mix_post, norm_ffn_pre, norm_ffn_post, w_in, lam_q1, lam_k1, lam_q2, lam_k2, attn_subln, conv_ssd_w, conv_ssd_b, dt_bias, a_log, d_skip, ssd_norm, w_o_attn, w_o_ssd, w_out, w_up, conv_ffn_w, conv_ffn_b, w_down):
    weights = _prepare_weights(norm_mix_pre, norm_mix_post, norm_ffn_pre, norm_ffn_post, w_in, lam_q1,
                               lam_k1, lam_q2, lam_k2, attn_subln, conv_ssd_w, conv_ssd_b, dt_bias, a_log,
                               d_skip, ssd_norm, w_o_attn, w_o_ssd, w_out, w_up, conv_ffn_w, conv_ffn_b,
                               w_down)
    groups = []
    for xg in (x_prompt, x_sample):
        b, s, d = xg.shape
        groups.append((b, s, _tiles(s)))

    def step(carry, w):
        out = tuple(_layer(x, b, s, w, cfg) for x, (b, s, cfg) in zip(carry, groups))
        return out, None

    init = tuple(xg.reshape(-1, xg.shape[-1]) for xg in (x_prompt, x_sample))
    out, _ = lax.scan(step, init, weights)
    return tuple(o.reshape(xg.shape) for o, xg in zip(out, (x_prompt, x_sample)))
```

```python
import functools
import math

import jax
import jax.numpy as jnp
from jax import lax
from jax.experimental import pallas as pl
from jax.experimental.pallas import tpu as pltpu

F32 = jnp.float32
BF16 = jnp.bfloat16

D_MODEL = 1024
ATTN_HEADS = 8
HEAD_DIM = 64
V_DIM = 2 * HEAD_DIM
SSD_HEADS = 16
SSD_HEAD_DIM = 64
D_INNER = SSD_HEADS * SSD_HEAD_DIM
SSD_GROUPS = 2
GROUP_COLS = D_INNER // SSD_GROUPS
D_STATE = 128
CHUNK = 128
CONV_DIM = D_INNER + 2 * SSD_GROUPS * D_STATE
D_FF = 2752
EPS = 1e-6

LANES = 128
D_FF_PAD = 2816
DT_PAD = LANES
HALO = 16
PROJ_CHUNK = 512

UNDERFLOW = 136.0
SUM_ROWS = 16
LOG2E = math.log2(math.e)
Q_PRESCALE = LOG2E / math.sqrt(HEAD_DIM)
VMEM_LIMIT = 56 * 1024 * 1024


def _params(sem):
    return pltpu.CompilerParams(dimension_semantics=sem, vmem_limit_bytes=VMEM_LIMIT)


def _rms(x, g):
    ms = jnp.mean(x * x, axis=-1, keepdims=True)
    return x * lax.rsqrt(ms + EPS) * g


def _dot(a, b):
    return jnp.dot(a, b, preferred_element_type=F32)


def _dot_nt(a, b):
    return lax.dot_general(a, b, (((1,), (1,)), ((), ())), preferred_element_type=F32)


def _bf16_terms(x):
    def top(v):
        bits = lax.bitcast_convert_type(v, jnp.uint32) & jnp.uint32(0xFFFF0000)
        return lax.bitcast_convert_type(bits, F32)

    hi = top(x)
    mid = top(x - hi)
    lo = x - hi - mid
    return hi, mid, lo


def _dot_f32_by_01(x, m01):
    return sum(_dot(term.astype(BF16), m01) for term in _bf16_terms(x))


def _dot_01_by_f32(m01, x):
    return sum(_dot(m01, term.astype(BF16)) for term in _bf16_terms(x))


def _softplus(x):
    return jnp.maximum(x, 0.0) + jnp.log1p(jnp.exp(-jnp.abs(x)))


def _silu(x):
    return x * jax.nn.sigmoid(x)


def _in_proj_kernel(x_ref, g_ref, wqk_ref, wvt_ref, wz_ref, wxbc_ref, wg_ref, wdt_ref, wdtt_ref,
                    qk_ref, vt_ref, z_ref, xbc_ref, gates_ref, dt_ref, dtt_ref):
    h = _rms(x_ref[...], g_ref[...]).astype(BF16)
    qk_cols = ATTN_HEADS * 2 * HEAD_DIM

    def project(w_ref, o_ref, scale_upto=0):
        n = w_ref.shape[1]
        for c0 in range(0, n, PROJ_CHUNK):
            c1 = min(c0 + PROJ_CHUNK, n)
            y = _dot(h, w_ref[:, c0:c1])
            if c1 <= scale_upto:
                y = y * Q_PRESCALE
            o_ref[:, c0:c1] = y.astype(o_ref.dtype)

    project(wqk_ref, qk_ref, scale_upto=qk_cols)
    for r0 in range(0, wvt_ref.shape[0], PROJ_CHUNK):
        vt_ref[r0:r0 + PROJ_CHUNK, :] = _dot_nt(wvt_ref[r0:r0 + PROJ_CHUNK, :], h).astype(vt_ref.dtype)
    project(wz_ref, z_ref)
    project(wxbc_ref, xbc_ref)
    project(wg_ref, gates_ref)
    project(wdt_ref, dt_ref)
    dtt_ref[...] = _dot_nt(wdtt_ref[...], h)


def in_proj(x, g, w, tm):
    t, d = x.shape
    weights = [w["w_qk"], w["w_vt"], w["w_z"], w["w_xbc"], w["w_gates"], w["w_dt"], w["w_dtt"]]
    rows = lambda n: pl.BlockSpec((tm, n), lambda i: (i, 0))
    cols = lambda n: pl.BlockSpec((n, tm), lambda i: (0, i))
    resident = lambda a: pl.BlockSpec(a.shape, lambda i: (0, 0), pipeline_mode=pl.Buffered(1))
    n_qk, n_v, n_z, n_xbc, n_g = (w["w_qk"].shape[1], w["w_vt"].shape[0], w["w_z"].shape[1],
                                  w["w_xbc"].shape[1], w["w_gates"].shape[1])
    n_dtt = w["w_dtt"].shape[0]
    return pl.pallas_call(
        _in_proj_kernel,
        grid=(t // tm,),
        in_specs=[rows(d), resident(g)] + [resident(a) for a in weights],
        out_specs=[rows(n_qk), cols(n_v), rows(n_z), rows(n_xbc), rows(n_g), rows(DT_PAD), cols(n_dtt)],
        out_shape=[
            jax.ShapeDtypeStruct((t, n_qk), BF16), jax.ShapeDtypeStruct((n_v, t), BF16),
            jax.ShapeDtypeStruct((t, n_z), F32), jax.ShapeDtypeStruct((t, n_xbc), F32),
            jax.ShapeDtypeStruct((t, n_g), F32), jax.ShapeDtypeStruct((t, DT_PAD), F32),
            jax.ShapeDtypeStruct((n_dtt, t), F32),
        ],
        compiler_params=_params(("parallel",)),
        name="in_proj",
    )(x, g, *weights)


def _attn_kernel(slopes_ref, lam_ref, q_ref, k_ref, vt_ref, lq1_ref, lk1_ref, lq2_ref, lk2_ref,
                 subln_ref, o_ref, kfeat_ref, qfeat_ref, knorm_ref, *s_refs, tq, tk, nk):
    h = pl.program_id(1)
    qi = pl.program_id(2)
    slope2 = slopes_ref[h] * LOG2E
    q0pos = qi * tq
    n_off = nk - 1
    kd = lax.shift_right_logical(qi, int(math.log2(tk // tq)))

    @pl.when(qi == 0)
    def _():
        ii = lax.broadcasted_iota(jnp.int32, (tq, LANES), 0).astype(F32)
        fq = lax.broadcasted_iota(jnp.int32, (tq, LANES), 1)
        row_terms = _bf16_terms(-slope2 * ii)
        slope_terms = _bf16_terms(jnp.full((tq, LANES), slope2, F32))
        q_feat = jnp.zeros((tq, LANES), F32)
        for n in range(3):
            q_feat = jnp.where(fq == n, row_terms[n], q_feat)
            q_feat = jnp.where((fq == 3 + n) | (fq == 6 + n), slope_terms[n], q_feat)
        qfeat_ref[...] = q_feat.astype(BF16)
        jj = lax.broadcasted_iota(jnp.int32, (tk, LANES), 0)
        fk = lax.broadcasted_iota(jnp.int32, (tk, LANES), 1)
        jj_lo = jnp.bitwise_and(jj, 255)
        jj_hi = (jj - jj_lo).astype(F32)
        jj_lo = jj_lo.astype(F32)
        k_feat = jnp.where(fk < 3, 1.0, jnp.where(fk < 6, jj_lo, jnp.where(fk < 9, jj_hi, 0.0)))
        kfeat_ref[0] = k_feat.astype(BF16)
        kfeat_ref[1] = (-k_feat).astype(BF16)

    q = q_ref[...]
    lane = lax.broadcasted_iota(jnp.int32, q.shape, 1)
    zero = jnp.zeros_like(q)
    q_feat = qfeat_ref[...]
    q_ops = jnp.concatenate([
        jnp.concatenate([jnp.where(lane < HEAD_DIM, q, zero), q_feat], axis=1),
        jnp.concatenate([jnp.where(lane >= HEAD_DIM, q, zero), q_feat], axis=1)], axis=0)
    ones_rows = jnp.ones((SUM_ROWS, tk), BF16)

    def raw_scores(kb, side):
        start = pl.multiple_of(kb * tk, tk)
        k_ops = jnp.concatenate([k_ref[pl.ds(start, tk), :], kfeat_ref[side]], axis=1)
        return _dot_nt(k_ops, q_ops)

    def offset(kb):
        return slope2 * jnp.abs(q0pos - kb * tk).astype(F32)

    def off_tile(t):
        side = (t >= kd).astype(jnp.int32)
        return t + side, side

    def produce(t, dst_ref):
        kb, side = off_tile(t)
        s = raw_scores(kb, side)
        dst_ref[...] = s
        return jnp.max(s, axis=0, keepdims=True)

    def update(s, mx, c, kb, state):
        m_old, acc_old = state
        m_new = jnp.maximum(m_old, mx - c)
        p = jnp.exp2(s - (m_new + c))
        alpha = jnp.exp2(m_old - m_new)
        vt = vt_ref[:, pl.ds(pl.multiple_of(kb * tk, tk), tk)]
        vt_ops = jnp.concatenate([vt, ones_rows], axis=0)
        return m_new, alpha * acc_old + _dot(vt_ops, p.astype(BF16))

    def consume(t, src_ref, mx, state):
        kb, _ = off_tile(t)
        return update(src_ref[...], mx, offset(kb), kb, state)

    DEPTH = len(s_refs)
    max_trips = (n_off - 1) // DEPTH

    @pl.when(qi == 0)
    def _():
        def tile_norm(j, best):
            kt = k_ref[pl.ds(pl.multiple_of(j * tk, tk), tk), :].astype(F32)
            return jnp.maximum(best, jnp.max(jnp.sum(kt * kt, axis=1, keepdims=True), axis=0, keepdims=True))
        knorm_ref[...] = jnp.broadcast_to(lax.fori_loop(0, nk, tile_norm, jnp.zeros((1, 1), F32)),
                                          knorm_ref.shape)

    c_d = offset(kd)
    s_d = jnp.minimum(raw_scores(kd, 0) - c_d, raw_scores(kd, 1) + c_d)
    state = (jnp.full((1, 2 * tq), -jnp.inf, F32), jnp.zeros((V_DIM + SUM_ROWS, 2 * tq), F32))
    state = update(s_d, jnp.max(s_d, axis=0, keepdims=True), 0.0, kd, state)

    qf = q.astype(F32)
    qn2 = jnp.max(jnp.sum(qf * qf, axis=1, keepdims=True), axis=0, keepdims=True)
    bound = jnp.sqrt(qn2 * knorm_ref[0:1, 0:1]) * 1.001 + 1.0
    m_min = jnp.min(state[0], axis=1, keepdims=True)
    reach = jnp.minimum((bound + UNDERFLOW - m_min) / slope2, 1e9)
    reach = (jnp.ceil(reach).astype(jnp.int32) + 1)[0, 0]
    tk_shift = int(math.log2(tk))
    kb_lo = jnp.minimum(lax.shift_right_logical(jnp.maximum(q0pos + 1 - reach, 0), tk_shift), kd)
    kb_hi = jnp.maximum(jnp.minimum(lax.shift_right_logical(reach + q0pos + tq - 2, tk_shift), nk - 1), kd)
    wanted = jnp.maximum(kb_hi - kb_lo - 1, 0)
    if DEPTH == 2:
        whole = lax.shift_right_logical(wanted + 1, 1)
    else:
        whole = lax.shift_right_logical((wanted + 2) * 43, 7)
    trips = jnp.clip(whole, 1, max_trips)
    first = jnp.minimum(kb_lo, n_off - 1 - DEPTH * trips)

    mx = tuple(produce(first + i, s_refs[i]) for i in range(DEPTH))

    def trip(r, carry, n_produce):
        state, mx = carry[:2], list(carry[2])
        for i in range(DEPTH):
            t = first + DEPTH * r + i
            state = consume(t, s_refs[i], mx[i], state)
            if i < n_produce:
                mx[i] = produce(t + DEPTH, s_refs[i])
        return state + (tuple(mx),)

    full_trips = trips - 1
    pairs = lax.shift_right_logical(full_trips, 1)
    carry = lax.fori_loop(0, pairs, lambda r2, c: trip(2 * r2 + 1, trip(2 * r2, c, DEPTH), DEPTH),
                          state + (mx,))
    carry = lax.fori_loop(2 * pairs, full_trips, functools.partial(trip, n_produce=DEPTH), carry)
    carry = trip(trips - 1, carry, n_produce=1)
    _, acc_fin = consume(first + DEPTH * trips, s_refs[0], carry[2][0], carry[:2])

    lam_init = lam_ref[0]
    one_minus = lam_ref[1]
    lam = (jnp.exp(jnp.sum(lq1_ref[...] * lk1_ref[...], axis=-1, keepdims=True))
           - jnp.exp(jnp.sum(lq2_ref[...] * lk2_ref[...], axis=-1, keepdims=True)) + lam_init)
    o_both = acc_fin[:V_DIM, :] / acc_fin[V_DIM:V_DIM + 1, :]
    o = o_both[:, :tq] - lam * o_both[:, tq:]
    ms = jnp.mean(o * o, axis=0, keepdims=True)
    y = o * lax.rsqrt(ms + EPS) * subln_ref[...] * one_minus
    o_ref[...] = y.T.astype(o_ref.dtype)


def diff_attention(qk, vt, slopes, lam_consts, lq1, lk1, lq2, lk2, subln_col, b, s, tq, tk, depth):
    t = b * s
    nq = s // tq
    nk = s // tk
    assert depth in (2, 3) and depth + 2 <= nk < 126 and (nk - 2) % depth == 0, (s, tk)
    assert tk % tq == 0 and tq <= 256
    hh = ATTN_HEADS
    smem = pl.BlockSpec(memory_space=pltpu.SMEM)
    vec = pl.BlockSpec((1, HEAD_DIM), lambda bi, h, qi: (0, 0))
    return pl.pallas_call(
        functools.partial(_attn_kernel, tq=tq, tk=tk, nk=nk),
        grid=(b, hh, nq),
        in_specs=[
            smem, smem,
            pl.BlockSpec((tq, V_DIM), lambda bi, h, qi: (bi * nq + qi, h)),
            pl.BlockSpec((s, V_DIM), lambda bi, h, qi: (bi, hh + h)),
            pl.BlockSpec((V_DIM, s), lambda bi, h, qi: (h, bi)),
            vec, vec, vec, vec,
            pl.BlockSpec((V_DIM, 1), lambda bi, h, qi: (0, 0)),
        ],
        out_specs=pl.BlockSpec((tq, V_DIM), lambda bi, h, qi: (bi * nq + qi, h)),
        out_shape=jax.ShapeDtypeStruct((t, hh * V_DIM), BF16),
        scratch_shapes=([pltpu.VMEM((2, tk, LANES), BF16), pltpu.VMEM((tq, LANES), BF16),
                         pltpu.VMEM((8, LANES), F32)]
                        + [pltpu.VMEM((tk, 2 * tq), F32)] * depth),
        compiler_params=_params(("parallel", "parallel", "arbitrary")),
        name="diff_attention",
    )(slopes, lam_consts, qk, qk, vt, lq1, lk1, lq2, lk2, subln_col)


def _ssd_kernel(*refs, rev, final, nc):
    if final:
        (xbc_ref, prev_ref, next_ref, dt_ref, dtt_ref, cw_ref, cb_ref, bias_p_ref, bias_c_ref,
         alog_p_ref, alog_c_ref, z_ref, yf_ref, dskip_ref, nw_ref, y_ref, st_ref) = refs
    else:
        (xbc_ref, prev_ref, next_ref, dt_ref, dtt_ref, cw_ref, cb_ref, bias_p_ref, bias_c_ref,
         alog_p_ref, alog_c_ref, y_ref, st_ref) = refs
    direction = 1 if rev else 0
    c = pl.program_id(1)
    cc = (nc - 1 - c) if rev else c
    ll = CHUNK

    @pl.when(c == 0)
    def _():
        st_ref[...] = jnp.zeros_like(st_ref)

    x = xbc_ref[...]
    before = jnp.where(cc == 0, 0.0, prev_ref[...])
    after = jnp.where(cc == nc - 1, 0.0, next_ref[...])
    x_ext = jnp.concatenate([x, after, before], axis=0)
    n_ext = x_ext.shape[0]
    xp = pltpu.roll(x_ext, 1, axis=0)[0:ll, :]
    xn = pltpu.roll(x_ext, n_ext - 1, axis=0)[0:ll, :]
    cw = cw_ref[...]
    u = _silu(cb_ref[...] + xp * cw[0:1, :] + x * cw[1:2, :] + xn * cw[2:3, :])
    xs = u[:, :D_INNER]
    bm = u[:, D_INNER:D_INNER + SSD_GROUPS * D_STATE]
    cm = u[:, D_INNER + SSD_GROUPS * D_STATE:]

    jj = lax.broadcasted_iota(jnp.int32, (DT_PAD, D_INNER), 0)
    col = lax.broadcasted_iota(jnp.int32, (DT_PAD, D_INNER), 1)
    head_of_col = lax.shift_right_logical(col, int(math.log2(SSD_HEAD_DIM)))
    expand = jnp.where(jj == direction * SSD_HEADS + head_of_col, 1.0, 0.0).astype(BF16)
    dt_cols = _softplus(dt_ref[...] + bias_p_ref[...])
    a_cols = dt_cols * (-jnp.exp(alog_p_ref[...]))
    dt_rows = _softplus(dtt_ref[direction * SSD_HEADS:(direction + 1) * SSD_HEADS, :] + bias_c_ref[...])
    a_rows = dt_rows * (-jnp.exp(alog_c_ref[...]))

    ri = lax.broadcasted_iota(jnp.int32, (ll, ll), 0)
    ci = lax.broadcasted_iota(jnp.int32, (ll, ll), 1)
    if rev:
        keep = ci >= ri
        edge = 0
    else:
        keep = ci <= ri
        edge = ll - 1
    tri = jnp.where(keep, 1.0, 0.0).astype(BF16)
    tri_t = jnp.where((ri >= ci) if rev else (ri <= ci), 1.0, 0.0).astype(BF16)
    cum_cols = _dot_01_by_f32(tri, a_cols)
    dt_full = _dot_f32_by_01(dt_cols, expand)
    cum_full = _dot_f32_by_01(cum_cols, expand)
    cum_rows = _dot_f32_by_01(a_rows, tri_t)

    xd = xs * dt_full
    xd_b = xd.astype(BF16)
    cum_edge = cum_full[edge:edge + 1, :]
    xdw = (xd * jnp.exp(cum_edge - cum_full)).astype(BF16)
    grow = jnp.exp(cum_full)
    lane = lax.broadcasted_iota(jnp.int32, (ll, LANES), 1)

    y_parts = []
    for g in range(SSD_GROUPS):
        bg = bm[:, g * D_STATE:(g + 1) * D_STATE]
        cg = cm[:, g * D_STATE:(g + 1) * D_STATE].astype(BF16)
        cb = _dot_nt(cg, bg.astype(BF16))
        gs = slice(g * GROUP_COLS, (g + 1) * GROUP_COLS)
        st_in = st_ref[:, gs]
        y_off = _dot(cg, st_in.astype(BF16)) * grow[:, gs]
        heads_per_group = SSD_HEADS // SSD_GROUPS
        for pair in range(heads_per_group // 2):
            lo = g * GROUP_COLS + pair * LANES
            xd_pair = xd_b[:, lo:lo + LANES]
            outs = []
            for sub in range(2):
                hd = g * heads_per_group + pair * 2 + sub
                seg = cum_full[:, hd * SSD_HEAD_DIM:hd * SSD_HEAD_DIM + 1] - cum_rows[hd:hd + 1, :]
                dec = jnp.exp(jnp.where(keep, seg, -jnp.inf))
                outs.append(_dot((cb * dec).astype(BF16), xd_pair))
            y_diag = jnp.where(lane < SSD_HEAD_DIM, outs[0], outs[1])
            y_parts.append(y_diag + y_off[:, pair * LANES:(pair + 1) * LANES])
        st_chunk = _dot(bg.T.astype(BF16), xdw[:, gs])
        st_ref[:, gs] = st_in * jnp.exp(cum_edge[:, gs]) + st_chunk
    y = jnp.concatenate(y_parts, axis=1)

    if final:
        y = yf_ref[...] + y + xs * dskip_ref[...]
        y = y * _silu(z_ref[...])
        nw = nw_ref[...]
        normed = []
        for g in range(SSD_GROUPS):
            gs = slice(g * GROUP_COLS, (g + 1) * GROUP_COLS)
            normed.append(_rms(y[:, gs], nw[:, gs]))
        y = jnp.concatenate(normed, axis=1)
    y_ref[...] = y.astype(y_ref.dtype)


def ssd_pass(xbc, dt, dtt, cw, cb, bias_p, bias_c, alog_p, alog_c, extras, b, s, rev):
    final = extras is not None
    t = b * s
    nc = s // CHUNK
    rows8 = CHUNK // 8
    last8 = t // 8 - 1

    def cidx(bi, c):
        return bi * nc + ((nc - 1 - c) if rev else c)

    def full(shape):
        return pl.BlockSpec(shape, lambda bi, c: (0, 0))

    chunk_rows = lambda w: pl.BlockSpec((CHUNK, w), lambda bi, c: (cidx(bi, c), 0))
    in_specs = [
        chunk_rows(CONV_DIM),
        pl.BlockSpec((8, CONV_DIM), lambda bi, c: (jnp.maximum(cidx(bi, c) * rows8 - 1, 0), 0)),
        pl.BlockSpec((8, CONV_DIM), lambda bi, c: (jnp.minimum((cidx(bi, c) + 1) * rows8, last8), 0)),
        chunk_rows(DT_PAD),
        pl.BlockSpec((2 * SSD_HEADS, CHUNK), lambda bi, c: (0, cidx(bi, c))),
        full((3, CONV_DIM)), full((1, CONV_DIM)),
        full((1, DT_PAD)), full((SSD_HEADS, 1)), full((1, DT_PAD)), full((SSD_HEADS, 1)),
    ]
    args = [xbc, xbc, xbc, dt, dtt, cw, cb, bias_p, bias_c, alog_p, alog_c]
    if final:
        z, yf, dskip_f, nw = extras
        in_specs += [chunk_rows(D_INNER), chunk_rows(D_INNER), full((1, D_INNER)), full((1, D_INNER))]
        args += [z, yf, dskip_f, nw]
    return pl.pallas_call(
        functools.partial(_ssd_kernel, rev=rev, final=final, nc=nc),
        grid=(b, nc),
        in_specs=in_specs,
        out_specs=chunk_rows(D_INNER),
        out_shape=jax.ShapeDtypeStruct((t, D_INNER), BF16 if final else F32),
        scratch_shapes=[pltpu.VMEM((D_STATE, D_INNER), F32)],
        compiler_params=_params(("parallel", "arbitrary")),
        name="ssd_bwd_final" if final else "ssd_fwd",
    )(*args)


def _merge_kernel(attn_ref, ssd_ref, gates_ref, x_ref, woa_ref, wos_ref, wout_ref, nw_ref, o_ref):
    a = _dot(attn_ref[...], woa_ref[...])
    s = _dot(ssd_ref[...], wos_ref[...])
    gates = gates_ref[...]
    merged = jax.nn.sigmoid(gates[:, :D_MODEL]) * a + jax.nn.sigmoid(gates[:, D_MODEL:]) * s
    mo = _dot(merged.astype(BF16), wout_ref[...])
    o_ref[...] = x_ref[...] + _rms(mo, nw_ref[...])


def merge_out(attn, ssd, gates, x, woa, wos, wout, nw, tm):
    t, d = x.shape
    rows = lambda w: pl.BlockSpec((tm, w), lambda i: (i, 0))
    full = lambda shape: pl.BlockSpec(shape, lambda i: (0, 0))
    return pl.pallas_call(
        _merge_kernel,
        grid=(t // tm,),
        in_specs=[rows(d), rows(d), rows(2 * d), rows(d), full((d, d)), full((d, d)), full((d, d)),
                  full((1, d))],
        out_specs=rows(d),
        out_shape=jax.ShapeDtypeStruct((t, d), F32),
        compiler_params=_params(("parallel",)),
        name="merge_out",
    )(attn, ssd, gates, x, woa, wos, wout, nw)


def _ffn_kernel(x_ref, xp_ref, xn_ref, gpre_ref, wa_ref, wg_ref, cwa_ref, cwg_ref, cba_ref, cbg_ref,
                wd_ref, gpost_ref, o_ref, h_ref, acc_ref, *, tm, tiles_per_seq, nf):
    i = pl.program_id(0)
    f = pl.program_id(1)

    @pl.when(f == 0)
    def _():
        gpre = gpre_ref[...]
        pos = i % tiles_per_seq
        h_ref[0:tm, :] = _rms(x_ref[...], gpre).astype(BF16)
        hp = jnp.where(pos == 0, 0.0, _rms(xp_ref[...], gpre))
        hn = jnp.where(pos == tiles_per_seq - 1, 0.0, _rms(xn_ref[...], gpre))
        h_ref[tm:tm + HALO, :] = hn.astype(BF16)
        h_ref[tm + HALO:tm + 2 * HALO, :] = hp.astype(BF16)
        acc_ref[...] = jnp.zeros_like(acc_ref)

    h = h_ref[...]
    n_ext = tm + 2 * HALO

    def conv_branch(w_ref, cw_ref, cb_ref):
        u = _dot(h, w_ref[...])
        um = u[0:tm, :]
        up = pltpu.roll(u, 1, axis=0)[0:tm, :]
        un = pltpu.roll(u, n_ext - 1, axis=0)[0:tm, :]
        cw = cw_ref[...]
        return cb_ref[...] + up * cw[0:1, :] + um * cw[1:2, :] + un * cw[2:3, :]

    a = conv_branch(wa_ref, cwa_ref, cba_ref)
    g = conv_branch(wg_ref, cwg_ref, cbg_ref)
    act = (_silu(g) * a).astype(BF16)
    acc_ref[...] += _dot(act, wd_ref[...])

    @pl.when(f == nf - 1)
    def _():
        o_ref[...] = x_ref[...] + _rms(acc_ref[...], gpost_ref[...])


def ffn(x, gpre, wa, wg, cwa, cwg, cba, cbg, wd, gpost, s, tm, tf):
    t, d = x.shape
    nf = D_FF_PAD // tf
    tiles_per_seq = s // tm
    blocks = tm // HALO
    last = t // HALO - 1
    full = lambda shape: pl.BlockSpec(shape, lambda i, f: (0, 0))
    colblk = lambda r: pl.BlockSpec((r, tf), lambda i, f: (0, f))
    return pl.pallas_call(
        functools.partial(_ffn_kernel, tm=tm, tiles_per_seq=tiles_per_seq, nf=nf),
        grid=(t // tm, nf),
        in_specs=[
            pl.BlockSpec((tm, d), lambda i, f: (i, 0)),
            pl.BlockSpec((HALO, d), lambda i, f: (jnp.maximum(i * blocks - 1, 0), 0)),
            pl.BlockSpec((HALO, d), lambda i, f: (jnp.minimum((i + 1) * blocks, last), 0)),
            full((1, d)),
            colblk(d), colblk(d), colblk(3), colblk(3), colblk(1), colblk(1),
            pl.BlockSpec((tf, d), lambda i, f: (f, 0)),
            full((1, d)),
        ],
        out_specs=pl.BlockSpec((tm, d), lambda i, f: (i, 0)),
        out_shape=jax.ShapeDtypeStruct((t, d), F32),
        scratch_shapes=[pltpu.VMEM((tm + 2 * HALO, d), BF16), pltpu.VMEM((tm, d), F32)],
        compiler_params=_params(("parallel", "arbitrary")),
        name="ffn",
    )(x, x, x, gpre, wa, wg, cwa, cwg, cba, cbg, wd, gpost)


def _tiles(s):
    return dict(
        tm_proj=min(256, s),
        tq=min(256, s), tk=min(512, s), depth=3,
        tm_merge=min(256, s),
        tm_ffn=min(1024, s), tf=256,
    )


def _layer(x, b, s, w, cfg):
    g_pre = w["norm_mix_pre"]
    qk, vt, z, xbc, gates, dt, dtt = in_proj(x, g_pre, w, cfg["tm_proj"])

    attn = diff_attention(qk, vt, w["slopes"], w["lam_consts"], w["lam_q1"], w["lam_k1"], w["lam_q2"],
                          w["lam_k2"], w["subln_col"], b, s, cfg["tq"], cfg["tk"], cfg["depth"])

    yf = ssd_pass(xbc, dt, dtt, w["conv_ssd_w"], w["conv_ssd_b"], w["bias_p"], w["bias_c"][0],
                  w["alog_p"], w["alog_c"][0], None, b, s, rev=False)
    ssd = ssd_pass(xbc, dt, dtt, w["conv_ssd_w"], w["conv_ssd_b"], w["bias_p"], w["bias_c"][1],
                   w["alog_p"], w["alog_c"][1], (z, yf, w["dskip_f"], w["ssd_norm"]), b, s, rev=True)

    x = merge_out(attn, ssd, gates, x, w["w_o_attn"], w["w_o_ssd"], w["w_out"], w["norm_mix_post"],
                  cfg["tm_merge"])
    x = ffn(x, w["norm_ffn_pre"], w["w_up_a"], w["w_up_g"], w["cw_a"], w["cw_g"], w["cb_a"], w["cb_g"],
            w["w_down"], w["norm_ffn_post"], s, cfg["tm_ffn"], cfg["tf"])
    return x


def _prepare_weights(norm_mix_pre, norm_mix_post, norm_ffn_pre, norm_ffn_post, w_in, lam_q1, lam_k1,
                     lam_q2, lam_k2, attn_subln, conv_ssd_w, conv_ssd_b, dt_bias, a_log, d_skip, ssd_norm,
                     w_o_attn, w_o_ssd, w_out, w_up, conv_ffn_w, conv_ffn_b, w_down):
    depth = w_in.shape[0]
    qk_cols = ATTN_HEADS * 2 * HEAD_DIM
    attn_w = ATTN_HEADS * V_DIM
    cuts = [0, 2 * qk_cols]
    for width in (attn_w, D_INNER, CONV_DIM, 2 * SSD_HEADS, 2 * D_MODEL):
        cuts.append(cuts[-1] + width)
    seg = lambda i: w_in[:, :, cuts[i]:cuts[i + 1]]
    w_dt = seg(4)
    row = lambda a: a[:, None, :]
    rep = lambda a: jnp.repeat(a, SSD_HEAD_DIM, axis=-1)
    pad_ff = lambda a: jnp.pad(a, [(0, 0)] * (a.ndim - 1) + [(0, D_FF_PAD - D_FF)])
    pad_dt = lambda a: jnp.pad(a.reshape(depth, 1, 2 * SSD_HEADS), ((0, 0), (0, 0), (0, DT_PAD - 2 * SSD_HEADS)))
    lam_init = [0.8 - 0.6 * math.exp(-0.3 * l) for l in range(depth)]
    return dict(
        norm_mix_pre=row(norm_mix_pre), norm_mix_post=row(norm_mix_post),
        norm_ffn_pre=row(norm_ffn_pre), norm_ffn_post=row(norm_ffn_post),
        w_qk=seg(0).astype(BF16),
        w_vt=jnp.swapaxes(seg(1), 1, 2).astype(BF16),
        w_z=seg(2).astype(BF16),
        w_xbc=seg(3).astype(BF16),
        w_dt=jnp.pad(w_dt, ((0, 0), (0, 0), (0, DT_PAD - 2 * SSD_HEADS))).astype(BF16),
        w_dtt=jnp.swapaxes(w_dt, 1, 2).astype(BF16),
        w_gates=seg(5).astype(BF16),
        slopes=jnp.tile(jnp.asarray([2.0 ** (-8.0 * (i + 1) / ATTN_HEADS) for i in range(ATTN_HEADS)],
                                    F32)[None], (depth, 1)),
        lam_consts=jnp.asarray([[li, 1.0 - li] for li in lam_init], F32),
        lam_q1=row(lam_q1), lam_k1=row(lam_k1), lam_q2=row(lam_q2), lam_k2=row(lam_k2),
        subln_col=attn_subln[:, :, None],
        conv_ssd_w=conv_ssd_w, conv_ssd_b=row(conv_ssd_b),
        bias_p=pad_dt(dt_bias), bias_c=dt_bias[..., None],
        alog_p=pad_dt(a_log), alog_c=a_log[..., None],
        dskip_f=row(rep(d_skip)), ssd_norm=row(ssd_norm),
        w_o_attn=w_o_attn.astype(BF16), w_o_ssd=w_o_ssd.astype(BF16), w_out=w_out.astype(BF16),
        w_up_a=pad_ff(w_up[:, :, :D_FF]).astype(BF16), w_up_g=pad_ff(w_up[:, :, D_FF:]).astype(BF16),
        cw_a=pad_ff(conv_ffn_w[:, :, :D_FF]), cw_g=pad_ff(conv_ffn_w[:, :, D_FF:]),
        cb_a=row(pad_ff(conv_ffn_b[:, :D_FF])), cb_g=row(pad_ff(conv_ffn_b[:, D_FF:])),
        w_down=jnp.pad(w_down, ((0, 0), (0, D_FF_PAD - D_FF), (0, 0))).astype(BF16),
    )


def kernel(x_prompt, x_sample, norm_mix_pre, norm_mix_post, norm_ffn_pre, norm_ffn_post, w_in, lam_q1, lam_k1, lam_q2, lam_k2, attn_subln, conv_ssd_w, conv_ssd_b, dt_bias, a_log, d_skip, ssd_norm, w_o_attn, w_o_ssd, w_out, w_up, conv_ffn_w, conv_ffn_b, w_down):
    weights = _prepare_weights(norm_mix_pre, norm_mix_post, norm_ffn_pre, norm_ffn_post, w_in, lam_q1,
                               lam_k1, lam_q2, lam_k2, attn_subln, conv_ssd_w, conv_ssd_b, dt_bias, a_log,
                               d_skip, ssd_norm, w_o_attn, w_o_ssd, w_out, w_up, conv_ffn_w, conv_ffn_b,
                               w_down)
    groups = []
    for xg in (x_prompt, x_sample):
        b, s, d = xg.shape
        groups.append((b, s, _tiles(s)))

    def step(carry, w):
        out = tuple(_layer(x, b, s, w, cfg) for x, (b, s, cfg) in zip(carry, groups))
        return out, None

    init = tuple(xg.reshape(-1, xg.shape[-1]) for xg in (x_prompt, x_sample))
    out, _ = lax.scan(step, init, weights)
    return tuple(o.reshape(xg.shape) for o, xg in zip(out, (x_prompt, x_sample)))
```

```python
import functools
import math

import jax
import jax.numpy as jnp
from jax import lax
from jax.experimental import pallas as pl
from jax.experimental.pallas import tpu as pltpu

F32 = jnp.float32
BF16 = jnp.bfloat16

D_MODEL = 1024
ATTN_HEADS = 8
HEAD_DIM = 64
V_DIM = 2 * HEAD_DIM
SSD_HEADS = 16
SSD_HEAD_DIM = 64
D_INNER = SSD_HEADS * SSD_HEAD_DIM
SSD_GROUPS = 2
GROUP_COLS = D_INNER // SSD_GROUPS
D_STATE = 128
CHUNK = 128
CONV_DIM = D_INNER + 2 * SSD_GROUPS * D_STATE
D_FF = 2752
EPS = 1e-6

LANES = 128
D_FF_PAD = 2816
DT_PAD = LANES
HALO = 16
PROJ_CHUNK = 512

UNDERFLOW = 136.0
BF16_EXACT_INT = 256
BOUND_REL_SLACK = 1.001
BOUND_ABS_SLACK = 1.0
REACH_CAP = 1e9
SUM_ROWS = 16
LOG2E = math.log2(math.e)
Q_PRESCALE = LOG2E / math.sqrt(HEAD_DIM)
VMEM_LIMIT = 56 * 1024 * 1024


def _params(sem):
    return pltpu.CompilerParams(dimension_semantics=sem, vmem_limit_bytes=VMEM_LIMIT)


def _rms(x, g):
    ms = jnp.mean(x * x, axis=-1, keepdims=True)
    return x * lax.rsqrt(ms + EPS) * g


def _dot(a, b):
    return jnp.dot(a, b, preferred_element_type=F32)


def _dot_nt(a, b):
    return lax.dot_general(a, b, (((1,), (1,)), ((), ())), preferred_element_type=F32)


def _bf16_terms(x):
    def top(v):
        bits = lax.bitcast_convert_type(v, jnp.uint32) & jnp.uint32(0xFFFF0000)
        return lax.bitcast_convert_type(bits, F32)

    hi = top(x)
    mid = top(x - hi)
    lo = x - hi - mid
    return hi, mid, lo


def _dot_f32_by_01(x, m01):
    return sum(_dot(term.astype(BF16), m01) for term in _bf16_terms(x))


def _dot_01_by_f32(m01, x):
    return sum(_dot(m01, term.astype(BF16)) for term in _bf16_terms(x))


def _softplus(x):
    return jnp.maximum(x, 0.0) + jnp.log1p(jnp.exp(-jnp.abs(x)))


def _silu(x):
    return x * jax.nn.sigmoid(x)


def _in_proj_kernel(x_ref, g_ref, wqk_ref, wvt_ref, wz_ref, wxbc_ref, wg_ref, wdt_ref, wdtt_ref,
                    qk_ref, vt_ref, z_ref, xbc_ref, gates_ref, dt_ref, dtt_ref):
    h = _rms(x_ref[...], g_ref[...]).astype(BF16)
    qk_cols = ATTN_HEADS * 2 * HEAD_DIM

    def project(w_ref, o_ref, scale_upto=0):
        n = w_ref.shape[1]
        for c0 in range(0, n, PROJ_CHUNK):
            c1 = min(c0 + PROJ_CHUNK, n)
            y = _dot(h, w_ref[:, c0:c1])
            if c1 <= scale_upto:
                y = y * Q_PRESCALE
            o_ref[:, c0:c1] = y.astype(o_ref.dtype)

    project(wqk_ref, qk_ref, scale_upto=qk_cols)
    for r0 in range(0, wvt_ref.shape[0], PROJ_CHUNK):
        vt_ref[r0:r0 + PROJ_CHUNK, :] = _dot_nt(wvt_ref[r0:r0 + PROJ_CHUNK, :], h).astype(vt_ref.dtype)
    project(wz_ref, z_ref)
    project(wxbc_ref, xbc_ref)
    project(wg_ref, gates_ref)
    project(wdt_ref, dt_ref)
    dtt_ref[...] = _dot_nt(wdtt_ref[...], h)


def in_proj(x, g, w, tm):
    t, d = x.shape
    weights = [w["w_qk"], w["w_vt"], w["w_z"], w["w_xbc"], w["w_gates"], w["w_dt"], w["w_dtt"]]
    rows = lambda n: pl.BlockSpec((tm, n), lambda i: (i, 0))
    cols = lambda n: pl.BlockSpec((n, tm), lambda i: (0, i))
    resident = lambda a: pl.BlockSpec(a.shape, lambda i: (0, 0), pipeline_mode=pl.Buffered(1))
    n_qk, n_v, n_z, n_xbc, n_g = (w["w_qk"].shape[1], w["w_vt"].shape[0], w["w_z"].shape[1],
                                  w["w_xbc"].shape[1], w["w_gates"].shape[1])
    n_dtt = w["w_dtt"].shape[0]
    return pl.pallas_call(
        _in_proj_kernel,
        grid=(t // tm,),
        in_specs=[rows(d), resident(g)] + [resident(a) for a in weights],
        out_specs=[rows(n_qk), cols(n_v), rows(n_z), rows(n_xbc), rows(n_g), rows(DT_PAD), cols(n_dtt)],
        out_shape=[
            jax.ShapeDtypeStruct((t, n_qk), BF16), jax.ShapeDtypeStruct((n_v, t), BF16),
            jax.ShapeDtypeStruct((t, n_z), F32), jax.ShapeDtypeStruct((t, n_xbc), F32),
            jax.ShapeDtypeStruct((t, n_g), F32), jax.ShapeDtypeStruct((t, DT_PAD), F32),
            jax.ShapeDtypeStruct((n_dtt, t), F32),
        ],
        compiler_params=_params(("parallel",)),
        name="in_proj",
    )(x, g, *weights)


def _attn_kernel(slopes_ref, lam_ref, q_ref, k_ref, vt_ref, lq1_ref, lk1_ref, lq2_ref, lk2_ref,
                 subln_ref, o_ref, kfeat_ref, qfeat_ref, knorm_ref, *s_refs, tq, tk, nk):
    h = pl.program_id(1)
    qi = pl.program_id(2)
    slope2 = slopes_ref[h] * LOG2E
    q0pos = qi * tq
    n_off = nk - 1
    kd = lax.shift_right_logical(qi, int(math.log2(tk // tq)))

    @pl.when(qi == 0)
    def _():
        ii = lax.broadcasted_iota(jnp.int32, (tq, LANES), 0).astype(F32)
        fq = lax.broadcasted_iota(jnp.int32, (tq, LANES), 1)
        row_terms = _bf16_terms(-slope2 * ii)
        slope_terms = _bf16_terms(jnp.full((tq, LANES), slope2, F32))
        q_feat = jnp.zeros((tq, LANES), F32)
        for n in range(3):
            q_feat = jnp.where(fq == n, row_terms[n], q_feat)
            q_feat = jnp.where((fq == 3 + n) | (fq == 6 + n), slope_terms[n], q_feat)
        qfeat_ref[...] = q_feat.astype(BF16)
        jj = lax.broadcasted_iota(jnp.int32, (tk, LANES), 0)
        fk = lax.broadcasted_iota(jnp.int32, (tk, LANES), 1)
        jj_lo = jnp.bitwise_and(jj, BF16_EXACT_INT - 1)
        jj_hi = (jj - jj_lo).astype(F32)
        jj_lo = jj_lo.astype(F32)
        k_feat = jnp.where(fk < 3, 1.0, jnp.where(fk < 6, jj_lo, jnp.where(fk < 9, jj_hi, 0.0)))
        kfeat_ref[0] = k_feat.astype(BF16)
        kfeat_ref[1] = (-k_feat).astype(BF16)

    q = q_ref[...]
    lane = lax.broadcasted_iota(jnp.int32, q.shape, 1)
    zero = jnp.zeros_like(q)
    q_feat = qfeat_ref[...]
    q_ops = jnp.concatenate([
        jnp.concatenate([jnp.where(lane < HEAD_DIM, q, zero), q_feat], axis=1),
        jnp.concatenate([jnp.where(lane >= HEAD_DIM, q, zero), q_feat], axis=1)], axis=0)
    ones_rows = jnp.ones((SUM_ROWS, tk), BF16)

    def raw_scores(kb, side):
        start = pl.multiple_of(kb * tk, tk)
        k_ops = jnp.concatenate([k_ref[pl.ds(start, tk), :], kfeat_ref[side]], axis=1)
        return _dot_nt(k_ops, q_ops)

    def offset(kb):
        return slope2 * jnp.abs(q0pos - kb * tk).astype(F32)

    def off_tile(t):
        side = (t >= kd).astype(jnp.int32)
        return t + side, side

    def produce(t, dst_ref):
        kb, side = off_tile(t)
        s = raw_scores(kb, side)
        dst_ref[...] = s
        return jnp.max(s, axis=0, keepdims=True)

    def update(s, mx, c, kb, state):
        m_old, acc_old = state
        m_new = jnp.maximum(m_old, mx - c)
        p = jnp.exp2(s - (m_new + c))
        alpha = jnp.exp2(m_old - m_new)
        vt = vt_ref[:, pl.ds(pl.multiple_of(kb * tk, tk), tk)]
        vt_ops = jnp.concatenate([vt, ones_rows], axis=0)
        return m_new, alpha * acc_old + _dot(vt_ops, p.astype(BF16))

    def consume(t, src_ref, mx, state):
        kb, _ = off_tile(t)
        return update(src_ref[...], mx, offset(kb), kb, state)

    DEPTH = len(s_refs)
    max_trips = (n_off - 1) // DEPTH

    @pl.when(qi == 0)
    def _():
        def tile_norm(j, best):
            kt = k_ref[pl.ds(pl.multiple_of(j * tk, tk), tk), :].astype(F32)
            return jnp.maximum(best, jnp.max(jnp.sum(kt * kt, axis=1, keepdims=True), axis=0, keepdims=True))
        knorm_ref[...] = jnp.broadcast_to(lax.fori_loop(0, nk, tile_norm, jnp.zeros((1, 1), F32)),
                                          knorm_ref.shape)

    c_d = offset(kd)
    s_d = jnp.minimum(raw_scores(kd, 0) - c_d, raw_scores(kd, 1) + c_d)
    state = (jnp.full((1, 2 * tq), -jnp.inf, F32), jnp.zeros((V_DIM + SUM_ROWS, 2 * tq), F32))
    state = update(s_d, jnp.max(s_d, axis=0, keepdims=True), 0.0, kd, state)

    qf = q.astype(F32)
    qn2 = jnp.max(jnp.sum(qf * qf, axis=1, keepdims=True), axis=0, keepdims=True)
    bound = jnp.sqrt(qn2 * knorm_ref[0:1, 0:1]) * BOUND_REL_SLACK + BOUND_ABS_SLACK
    m_min = jnp.min(state[0], axis=1, keepdims=True)
    reach = jnp.minimum((bound + UNDERFLOW - m_min) / slope2, REACH_CAP)
    reach = (jnp.ceil(reach).astype(jnp.int32) + 1)[0, 0]
    tk_shift = int(math.log2(tk))
    kb_lo = jnp.minimum(lax.shift_right_logical(jnp.maximum(q0pos + 1 - reach, 0), tk_shift), kd)
    kb_hi = jnp.maximum(jnp.minimum(lax.shift_right_logical(reach + q0pos + tq - 2, tk_shift), nk - 1), kd)
    wanted = jnp.maximum(kb_hi - kb_lo - 1, 0)
    if DEPTH == 2:
        whole = lax.shift_right_logical(wanted + 1, 1)
    else:
        whole = lax.shift_right_logical((wanted + 2) * 43, 7)
    trips = jnp.clip(whole, 1, max_trips)
    first = jnp.minimum(kb_lo, n_off - 1 - DEPTH * trips)

    mx = tuple(produce(first + i, s_refs[i]) for i in range(DEPTH))

    def trip(r, carry, n_produce):
        state, mx = carry[:2], list(carry[2])
        for i in range(DEPTH):
            t = first + DEPTH * r + i
            state = consume(t, s_refs[i], mx[i], state)
            if i < n_produce:
                mx[i] = produce(t + DEPTH, s_refs[i])
        return state + (tuple(mx),)

    full_trips = trips - 1
    pairs = lax.shift_right_logical(full_trips, 1)
    carry = lax.fori_loop(0, pairs, lambda r2, c: trip(2 * r2 + 1, trip(2 * r2, c, DEPTH), DEPTH),
                          state + (mx,))
    carry = lax.fori_loop(2 * pairs, full_trips, functools.partial(trip, n_produce=DEPTH), carry)
    carry = trip(trips - 1, carry, n_produce=1)
    _, acc_fin = consume(first + DEPTH * trips, s_refs[0], carry[2][0], carry[:2])

    lam_init = lam_ref[0]
    one_minus = lam_ref[1]
    lam = (jnp.exp(jnp.sum(lq1_ref[...] * lk1_ref[...], axis=-1, keepdims=True))
           - jnp.exp(jnp.sum(lq2_ref[...] * lk2_ref[...], axis=-1, keepdims=True)) + lam_init)
    o_both = acc_fin[:V_DIM, :] / acc_fin[V_DIM:V_DIM + 1, :]
    o = o_both[:, :tq] - lam * o_both[:, tq:]
    ms = jnp.mean(o * o, axis=0, keepdims=True)
    y = o * lax.rsqrt(ms + EPS) * subln_ref[...] * one_minus
    o_ref[...] = y.T.astype(o_ref.dtype)


def diff_attention(qk, vt, slopes, lam_consts, lq1, lk1, lq2, lk2, subln_col, b, s, tq, tk, depth):
    t = b * s
    nq = s // tq
    nk = s // tk
    assert depth in (2, 3) and depth + 2 <= nk < 126 and (nk - 2) % depth == 0, (s, tk)
    assert tk % tq == 0 and tk <= BF16_EXACT_INT ** 2
    hh = ATTN_HEADS
    smem = pl.BlockSpec(memory_space=pltpu.SMEM)
    vec = pl.BlockSpec((1, HEAD_DIM), lambda bi, h, qi: (0, 0))
    return pl.pallas_call(
        functools.partial(_attn_kernel, tq=tq, tk=tk, nk=nk),
        grid=(b, hh, nq),
        in_specs=[
            smem, smem,
            pl.BlockSpec((tq, V_DIM), lambda bi, h, qi: (bi * nq + qi, h)),
            pl.BlockSpec((s, V_DIM), lambda bi, h, qi: (bi, hh + h)),
            pl.BlockSpec((V_DIM, s), lambda bi, h, qi: (h, bi)),
            vec, vec, vec, vec,
            pl.BlockSpec((V_DIM, 1), lambda bi, h, qi: (0, 0)),
        ],
        out_specs=pl.BlockSpec((tq, V_DIM), lambda bi, h, qi: (bi * nq + qi, h)),
        out_shape=jax.ShapeDtypeStruct((t, hh * V_DIM), BF16),
        scratch_shapes=([pltpu.VMEM((2, tk, LANES), BF16), pltpu.VMEM((tq, LANES), BF16),
                         pltpu.VMEM((8, LANES), F32)]
                        + [pltpu.VMEM((tk, 2 * tq), F32)] * depth),
        compiler_params=_params(("parallel", "parallel", "arbitrary")),
        name="diff_attention",
    )(slopes, lam_consts, qk, qk, vt, lq1, lk1, lq2, lk2, subln_col)


def _ssd_kernel(*refs, rev, final, nc):
    if final:
        (xbc_ref, prev_ref, next_ref, dt_ref, dtt_ref, cw_ref, cb_ref, bias_p_ref, bias_c_ref,
         alog_p_ref, alog_c_ref, z_ref, yf_ref, dskip_ref, nw_ref, y_ref, st_ref) = refs
    else:
        (xbc_ref, prev_ref, next_ref, dt_ref, dtt_ref, cw_ref, cb_ref, bias_p_ref, bias_c_ref,
         alog_p_ref, alog_c_ref, y_ref, st_ref) = refs
    direction = 1 if rev else 0
    c = pl.program_id(1)
    cc = (nc - 1 - c) if rev else c
    ll = CHUNK

    @pl.when(c == 0)
    def _():
        st_ref[...] = jnp.zeros_like(st_ref)

    x = xbc_ref[...]
    before = jnp.where(cc == 0, 0.0, prev_ref[...])
    after = jnp.where(cc == nc - 1, 0.0, next_ref[...])
    x_ext = jnp.concatenate([x, after, before], axis=0)
    n_ext = x_ext.shape[0]
    xp = pltpu.roll(x_ext, 1, axis=0)[0:ll, :]
    xn = pltpu.roll(x_ext, n_ext - 1, axis=0)[0:ll, :]
    cw = cw_ref[...]
    u = _silu(cb_ref[...] + xp * cw[0:1, :] + x * cw[1:2, :] + xn * cw[2:3, :])
    xs = u[:, :D_INNER]
    bm = u[:, D_INNER:D_INNER + SSD_GROUPS * D_STATE]
    cm = u[:, D_INNER + SSD_GROUPS * D_STATE:]

    jj = lax.broadcasted_iota(jnp.int32, (DT_PAD, D_INNER), 0)
    col = lax.broadcasted_iota(jnp.int32, (DT_PAD, D_INNER), 1)
    head_of_col = lax.shift_right_logical(col, int(math.log2(SSD_HEAD_DIM)))
    expand = jnp.where(jj == direction * SSD_HEADS + head_of_col, 1.0, 0.0).astype(BF16)
    dt_cols = _softplus(dt_ref[...] + bias_p_ref[...])
    a_cols = dt_cols * (-jnp.exp(alog_p_ref[...]))
    dt_rows = _softplus(dtt_ref[direction * SSD_HEADS:(direction + 1) * SSD_HEADS, :] + bias_c_ref[...])
    a_rows = dt_rows * (-jnp.exp(alog_c_ref[...]))

    ri = lax.broadcasted_iota(jnp.int32, (ll, ll), 0)
    ci = lax.broadcasted_iota(jnp.int32, (ll, ll), 1)
    if rev:
        keep = ci >= ri
        edge = 0
    else:
        keep = ci <= ri
        edge = ll - 1
    tri = jnp.where(keep, 1.0, 0.0).astype(BF16)
    tri_t = jnp.where((ri >= ci) if rev else (ri <= ci), 1.0, 0.0).astype(BF16)
    cum_cols = _dot_01_by_f32(tri, a_cols)
    dt_full = _dot_f32_by_01(dt_cols, expand)
    cum_full = _dot_f32_by_01(cum_cols, expand)
    cum_rows = _dot_f32_by_01(a_rows, tri_t)

    xd = xs * dt_full
    xd_b = xd.astype(BF16)
    cum_edge = cum_full[edge:edge + 1, :]
    xdw = (xd * jnp.exp(cum_edge - cum_full)).astype(BF16)
    grow = jnp.exp(cum_full)
    lane = lax.broadcasted_iota(jnp.int32, (ll, LANES), 1)

    y_parts = []
    for g in range(SSD_GROUPS):
        bg = bm[:, g * D_STATE:(g + 1) * D_STATE]
        cg = cm[:, g * D_STATE:(g + 1) * D_STATE].astype(BF16)
        cb = _dot_nt(cg, bg.astype(BF16))
        gs = slice(g * GROUP_COLS, (g + 1) * GROUP_COLS)
        st_in = st_ref[:, gs]
        y_off = _dot(cg, st_in.astype(BF16)) * grow[:, gs]
        heads_per_group = SSD_HEADS // SSD_GROUPS
        for pair in range(heads_per_group // 2):
            lo = g * GROUP_COLS + pair * LANES
            xd_pair = xd_b[:, lo:lo + LANES]
            outs = []
            for sub in range(2):
                hd = g * heads_per_group + pair * 2 + sub
                seg = cum_full[:, hd * SSD_HEAD_DIM:hd * SSD_HEAD_DIM + 1] - cum_rows[hd:hd + 1, :]
                dec = jnp.exp(jnp.where(keep, seg, -jnp.inf))
                outs.append(_dot((cb * dec).astype(BF16), xd_pair))
            y_diag = jnp.where(lane < SSD_HEAD_DIM, outs[0], outs[1])
            y_parts.append(y_diag + y_off[:, pair * LANES:(pair + 1) * LANES])
        st_chunk = _dot(bg.T.astype(BF16), xdw[:, gs])
        st_ref[:, gs] = st_in * jnp.exp(cum_edge[:, gs]) + st_chunk
    y = jnp.concatenate(y_parts, axis=1)

    if final:
        y = yf_ref[...] + y + xs * dskip_ref[...]
        y = y * _silu(z_ref[...])
        nw = nw_ref[...]
        normed = []
        for g in range(SSD_GROUPS):
            gs = slice(g * GROUP_COLS, (g + 1) * GROUP_COLS)
            normed.append(_rms(y[:, gs], nw[:, gs]))
        y = jnp.concatenate(normed, axis=1)
    y_ref[...] = y.astype(y_ref.dtype)


def ssd_pass(xbc, dt, dtt, cw, cb, bias_p, bias_c, alog_p, alog_c, extras, b, s, rev):
    final = extras is not None
    t = b * s
    nc = s // CHUNK
    rows8 = CHUNK // 8
    last8 = t // 8 - 1

    def cidx(bi, c):
        return bi * nc + ((nc - 1 - c) if rev else c)

    def full(shape):
        return pl.BlockSpec(shape, lambda bi, c: (0, 0))

    chunk_rows = lambda w: pl.BlockSpec((CHUNK, w), lambda bi, c: (cidx(bi, c), 0))
    in_specs = [
        chunk_rows(CONV_DIM),
        pl.BlockSpec((8, CONV_DIM), lambda bi, c: (jnp.maximum(cidx(bi, c) * rows8 - 1, 0), 0)),
        pl.BlockSpec((8, CONV_DIM), lambda bi, c: (jnp.minimum((cidx(bi, c) + 1) * rows8, last8), 0)),
        chunk_rows(DT_PAD),
        pl.BlockSpec((2 * SSD_HEADS, CHUNK), lambda bi, c: (0, cidx(bi, c))),
        full((3, CONV_DIM)), full((1, CONV_DIM)),
        full((1, DT_PAD)), full((SSD_HEADS, 1)), full((1, DT_PAD)), full((SSD_HEADS, 1)),
    ]
    args = [xbc, xbc, xbc, dt, dtt, cw, cb, bias_p, bias_c, alog_p, alog_c]
    if final:
        z, yf, dskip_f, nw = extras
        in_specs += [chunk_rows(D_INNER), chunk_rows(D_INNER), full((1, D_INNER)), full((1, D_INNER))]
        args += [z, yf, dskip_f, nw]
    return pl.pallas_call(
        functools.partial(_ssd_kernel, rev=rev, final=final, nc=nc),
        grid=(b, nc),
        in_specs=in_specs,
        out_specs=chunk_rows(D_INNER),
        out_shape=jax.ShapeDtypeStruct((t, D_INNER), BF16 if final else F32),
        scratch_shapes=[pltpu.VMEM((D_STATE, D_INNER), F32)],
        compiler_params=_params(("parallel", "arbitrary")),
        name="ssd_bwd_final" if final else "ssd_fwd",
    )(*args)


def _merge_kernel(attn_ref, ssd_ref, gates_ref, x_ref, woa_ref, wos_ref, wout_ref, nw_ref, o_ref):
    a = _dot(attn_ref[...], woa_ref[...])
    s = _dot(ssd_ref[...], wos_ref[...])
    gates = gates_ref[...]
    merged = jax.nn.sigmoid(gates[:, :D_MODEL]) * a + jax.nn.sigmoid(gates[:, D_MODEL:]) * s
    mo = _dot(merged.astype(BF16), wout_ref[...])
    o_ref[...] = x_ref[...] + _rms(mo, nw_ref[...])


def merge_out(attn, ssd, gates, x, woa, wos, wout, nw, tm):
    t, d = x.shape
    rows = lambda w: pl.BlockSpec((tm, w), lambda i: (i, 0))
    full = lambda shape: pl.BlockSpec(shape, lambda i: (0, 0))
    return pl.pallas_call(
        _merge_kernel,
        grid=(t // tm,),
        in_specs=[rows(d), rows(d), rows(2 * d), rows(d), full((d, d)), full((d, d)), full((d, d)),
                  full((1, d))],
        out_specs=rows(d),
        out_shape=jax.ShapeDtypeStruct((t, d), F32),
        compiler_params=_params(("parallel",)),
        name="merge_out",
    )(attn, ssd, gates, x, woa, wos, wout, nw)


def _ffn_kernel(x_ref, xp_ref, xn_ref, gpre_ref, wa_ref, wg_ref, cwa_ref, cwg_ref, cba_ref, cbg_ref,
                wd_ref, gpost_ref, o_ref, h_ref, acc_ref, *, tm, tiles_per_seq, nf):
    i = pl.program_id(0)
    f = pl.program_id(1)

    @pl.when(f == 0)
    def _():
        gpre = gpre_ref[...]
        pos = i % tiles_per_seq
        h_ref[0:tm, :] = _rms(x_ref[...], gpre).astype(BF16)
        hp = jnp.where(pos == 0, 0.0, _rms(xp_ref[...], gpre))
        hn = jnp.where(pos == tiles_per_seq - 1, 0.0, _rms(xn_ref[...], gpre))
        h_ref[tm:tm + HALO, :] = hn.astype(BF16)
        h_ref[tm + HALO:tm + 2 * HALO, :] = hp.astype(BF16)
        acc_ref[...] = jnp.zeros_like(acc_ref)

    h = h_ref[...]
    n_ext = tm + 2 * HALO

    def conv_branch(w_ref, cw_ref, cb_ref):
        u = _dot(h, w_ref[...])
        um = u[0:tm, :]
        up = pltpu.roll(u, 1, axis=0)[0:tm, :]
        un = pltpu.roll(u, n_ext - 1, axis=0)[0:tm, :]
        cw = cw_ref[...]
        return cb_ref[...] + up * cw[0:1, :] + um * cw[1:2, :] + un * cw[2:3, :]

    a = conv_branch(wa_ref, cwa_ref, cba_ref)
    g = conv_branch(wg_ref, cwg_ref, cbg_ref)
    act = (_silu(g) * a).astype(BF16)
    acc_ref[...] += _dot(act, wd_ref[...])

    @pl.when(f == nf - 1)
    def _():
        o_ref[...] = x_ref[...] + _rms(acc_ref[...], gpost_ref[...])


def ffn(x, gpre, wa, wg, cwa, cwg, cba, cbg, wd, gpost, s, tm, tf):
    t, d = x.shape
    nf = D_FF_PAD // tf
    tiles_per_seq = s // tm
    blocks = tm // HALO
    last = t // HALO - 1
    full = lambda shape: pl.BlockSpec(shape, lambda i, f: (0, 0))
    colblk = lambda r: pl.BlockSpec((r, tf), lambda i, f: (0, f))
    return pl.pallas_call(
        functools.partial(_ffn_kernel, tm=tm, tiles_per_seq=tiles_per_seq, nf=nf),
        grid=(t // tm, nf),
        in_specs=[
            pl.BlockSpec((tm, d), lambda i, f: (i, 0)),
            pl.BlockSpec((HALO, d), lambda i, f: (jnp.maximum(i * blocks - 1, 0), 0)),
            pl.BlockSpec((HALO, d), lambda i, f: (jnp.minimum((i + 1) * blocks, last), 0)),
            full((1, d)),
            colblk(d), colblk(d), colblk(3), colblk(3), colblk(1), colblk(1),
            pl.BlockSpec((tf, d), lambda i, f: (f, 0)),
            full((1, d)),
        ],
        out_specs=pl.BlockSpec((tm, d), lambda i, f: (i, 0)),
        out_shape=jax.ShapeDtypeStruct((t, d), F32),
        scratch_shapes=[pltpu.VMEM((tm + 2 * HALO, d), BF16), pltpu.VMEM((tm, d), F32)],
        compiler_params=_params(("parallel", "arbitrary")),
        name="ffn",
    )(x, x, x, gpre, wa, wg, cwa, cwg, cba, cbg, wd, gpost)


def _tiles(s):
    return dict(
        tm_proj=min(256, s),
        tq=min(256, s), tk=min(512, s), depth=3,
        tm_merge=min(256, s),
        tm_ffn=min(1024, s), tf=256,
    )


def _layer(x, b, s, w, cfg):
    g_pre = w["norm_mix_pre"]
    qk, vt, z, xbc, gates, dt, dtt = in_proj(x, g_pre, w, cfg["tm_proj"])

    attn = diff_attention(qk, vt, w["slopes"], w["lam_consts"], w["lam_q1"], w["lam_k1"], w["lam_q2"],
                          w["lam_k2"], w["subln_col"], b, s, cfg["tq"], cfg["tk"], cfg["depth"])

    yf = ssd_pass(xbc, dt, dtt, w["conv_ssd_w"], w["conv_ssd_b"], w["bias_p"], w["bias_c"][0],
                  w["alog_p"], w["alog_c"][0], None, b, s, rev=False)
    ssd = ssd_pass(xbc, dt, dtt, w["conv_ssd_w"], w["conv_ssd_b"], w["bias_p"], w["bias_c"][1],
                   w["alog_p"], w["alog_c"][1], (z, yf, w["dskip_f"], w["ssd_norm"]), b, s, rev=True)

    x = merge_out(attn, ssd, gates, x, w["w_o_attn"], w["w_o_ssd"], w["w_out"], w["norm_mix_post"],
                  cfg["tm_merge"])
    x = ffn(x, w["norm_ffn_pre"], w["w_up_a"], w["w_up_g"], w["cw_a"], w["cw_g"], w["cb_a"], w["cb_g"],
            w["w_down"], w["norm_ffn_post"], s, cfg["tm_ffn"], cfg["tf"])
    return x


def _prepare_weights(norm_mix_pre, norm_mix_post, norm_ffn_pre, norm_ffn_post, w_in, lam_q1, lam_k1,
                     lam_q2, lam_k2, attn_subln, conv_ssd_w, conv_ssd_b, dt_bias, a_log, d_skip, ssd_norm,
                     w_o_attn, w_o_ssd, w_out, w_up, conv_ffn_w, conv_ffn_b, w_down):
    depth = w_in.shape[0]
    qk_cols = ATTN_HEADS * 2 * HEAD_DIM
    attn_w = ATTN_HEADS * V_DIM
    cuts = [0, 2 * qk_cols]
    for width in (attn_w, D_INNER, CONV_DIM, 2 * SSD_HEADS, 2 * D_MODEL):
        cuts.append(cuts[-1] + width)
    seg = lambda i: w_in[:, :, cuts[i]:cuts[i + 1]]
    w_dt = seg(4)
    row = lambda a: a[:, None, :]
    rep = lambda a: jnp.repeat(a, SSD_HEAD_DIM, axis=-1)
    pad_ff = lambda a: jnp.pad(a, [(0, 0)] * (a.ndim - 1) + [(0, D_FF_PAD - D_FF)])
    pad_dt = lambda a: jnp.pad(a.reshape(depth, 1, 2 * SSD_HEADS), ((0, 0), (0, 0), (0, DT_PAD - 2 * SSD_HEADS)))
    lam_init = [0.8 - 0.6 * math.exp(-0.3 * l) for l in range(depth)]
    return dict(
        norm_mix_pre=row(norm_mix_pre), norm_mix_post=row(norm_mix_post),
        norm_ffn_pre=row(norm_ffn_pre), norm_ffn_post=row(norm_ffn_post),
        w_qk=seg(0).astype(BF16),
        w_vt=jnp.swapaxes(seg(1), 1, 2).astype(BF16),
        w_z=seg(2).astype(BF16),
        w_xbc=seg(3).astype(BF16),
        w_dt=jnp.pad(w_dt, ((0, 0), (0, 0), (0, DT_PAD - 2 * SSD_HEADS))).astype(BF16),
        w_dtt=jnp.swapaxes(w_dt, 1, 2).astype(BF16),
        w_gates=seg(5).astype(BF16),
        slopes=jnp.tile(jnp.asarray([2.0 ** (-8.0 * (i + 1) / ATTN_HEADS) for i in range(ATTN_HEADS)],
                                    F32)[None], (depth, 1)),
        lam_consts=jnp.asarray([[li, 1.0 - li] for li in lam_init], F32),
        lam_q1=row(lam_q1), lam_k1=row(lam_k1), lam_q2=row(lam_q2), lam_k2=row(lam_k2),
        subln_col=attn_subln[:, :, None],
        conv_ssd_w=conv_ssd_w, conv_ssd_b=row(conv_ssd_b),
        bias_p=pad_dt(dt_bias), bias_c=dt_bias[..., None],
        alog_p=pad_dt(a_log), alog_c=a_log[..., None],
        dskip_f=row(rep(d_skip)), ssd_norm=row(ssd_norm),
        w_o_attn=w_o_attn.astype(BF16), w_o_ssd=w_o_ssd.astype(BF16), w_out=w_out.astype(BF16),
        w_up_a=pad_ff(w_up[:, :, :D_FF]).astype(BF16), w_up_g=pad_ff(w_up[:, :, D_FF:]).astype(BF16),
        cw_a=pad_ff(conv_ffn_w[:, :, :D_FF]), cw_g=pad_ff(conv_ffn_w[:, :, D_FF:]),
        cb_a=row(pad_ff(conv_ffn_b[:, :D_FF])), cb_g=row(pad_ff(conv_ffn_b[:, D_FF:])),
        w_down=jnp.pad(w_down, ((0, 0), (0, D_FF_PAD - D_FF), (0, 0))).astype(BF16),
    )


def kernel(x_prompt, x_sample, norm_mix_pre, norm_mix_post, norm_ffn_pre, norm_ffn_post, w_in, lam_q1, lam_k1, lam_q2, lam_k2, attn_subln, conv_ssd_w, conv_ssd_b, dt_bias, a_log, d_skip, ssd_norm, w_o_attn, w_o_ssd, w_out, w_up, conv_ffn_w, conv_ffn_b, w_down):
    weights = _prepare_weights(norm_mix_pre, norm_mix_post, norm_ffn_pre, norm_ffn_post, w_in, lam_q1,
                               lam_k1, lam_q2, lam_k2, attn_subln, conv_ssd_w, conv_ssd_b, dt_bias, a_log,
                               d_skip, ssd_norm, w_o_attn, w_o_ssd, w_out, w_up, conv_ffn_w, conv_ffn_b,
                               w_down)
    groups = []
    for xg in (x_prompt, x_sample):
        b, s, d = xg.shape
        groups.append((b, s, _tiles(s)))

    def step(carry, w):
        out = tuple(_layer(x, b, s, w, cfg) for x, (b, s, cfg) in zip(carry, groups))
        return out, None

    init = tuple(xg.reshape(-1, xg.shape[-1]) for xg in (x_prompt, x_sample))
    out, _ = lax.scan(step, init, weights)
    return tuple(o.reshape(xg.shape) for o, xg in zip(out, (x_prompt, x_sample)))
```

```python
import functools
import math

import jax
import jax.numpy as jnp
from jax import lax
from jax.experimental import pallas as pl
from jax.experimental.pallas import tpu as pltpu

F32 = jnp.float32
BF16 = jnp.bfloat16

D_MODEL = 1024
ATTN_HEADS = 8
HEAD_DIM = 64
V_DIM = 2 * HEAD_DIM
SSD_HEADS = 16
SSD_HEAD_DIM = 64
D_INNER = SSD_HEADS * SSD_HEAD_DIM
SSD_GROUPS = 2
GROUP_COLS = D_INNER // SSD_GROUPS
D_STATE = 128
CHUNK = 128
CONV_DIM = D_INNER + 2 * SSD_GROUPS * D_STATE
D_FF = 2752
EPS = 1e-6

LANES = 128
D_FF_PAD = 2816
DT_PAD = LANES
HALO = 16
PROJ_CHUNK = 512

UNDERFLOW = 136.0
BF16_EXACT_INT = 256
BOUND_REL_SLACK = 1.001
BOUND_ABS_SLACK = 1.0
REACH_CAP = 1e9
SUM_ROWS = 16
LOG2E = math.log2(math.e)
Q_PRESCALE = LOG2E / math.sqrt(HEAD_DIM)
VMEM_LIMIT = 56 * 1024 * 1024


def _params(sem):
    return pltpu.CompilerParams(dimension_semantics=sem, vmem_limit_bytes=VMEM_LIMIT)


def _rms(x, g):
    ms = jnp.mean(x * x, axis=-1, keepdims=True)
    return x * lax.rsqrt(ms + EPS) * g


def _dot(a, b):
    return jnp.dot(a, b, preferred_element_type=F32)


def _dot_nt(a, b):
    return lax.dot_general(a, b, (((1,), (1,)), ((), ())), preferred_element_type=F32)


def _bf16_terms(x):
    def top(v):
        bits = lax.bitcast_convert_type(v, jnp.uint32) & jnp.uint32(0xFFFF0000)
        return lax.bitcast_convert_type(bits, F32)

    hi = top(x)
    mid = top(x - hi)
    lo = x - hi - mid
    return hi, mid, lo


def _dot_f32_by_01(x, m01):
    return sum(_dot(term.astype(BF16), m01) for term in _bf16_terms(x))


def _dot_01_by_f32(m01, x):
    return sum(_dot(m01, term.astype(BF16)) for term in _bf16_terms(x))


def _softplus(x):
    return jnp.maximum(x, 0.0) + jnp.log1p(jnp.exp(-jnp.abs(x)))


def _silu(x):
    return x * jax.nn.sigmoid(x)


def _in_proj_kernel(x_ref, g_ref, wqk_ref, wvt_ref, wz_ref, wxbc_ref, wg_ref, wdt_ref, wdtt_ref,
                    qk_ref, vt_ref, z_ref, xbc_ref, gates_ref, dt_ref, dtt_ref):
    h = _rms(x_ref[...], g_ref[...]).astype(BF16)
    qk_cols = ATTN_HEADS * 2 * HEAD_DIM

    def project(w_ref, o_ref, scale_upto=0):
        n = w_ref.shape[1]
        for c0 in range(0, n, PROJ_CHUNK):
            c1 = min(c0 + PROJ_CHUNK, n)
            y = _dot(h, w_ref[:, c0:c1])
            if c1 <= scale_upto:
                y = y * Q_PRESCALE
            o_ref[:, c0:c1] = y.astype(o_ref.dtype)

    project(wqk_ref, qk_ref, scale_upto=qk_cols)
    for r0 in range(0, wvt_ref.shape[0], PROJ_CHUNK):
        vt_ref[r0:r0 + PROJ_CHUNK, :] = _dot_nt(wvt_ref[r0:r0 + PROJ_CHUNK, :], h).astype(vt_ref.dtype)
    project(wz_ref, z_ref)
    project(wxbc_ref, xbc_ref)
    project(wg_ref, gates_ref)
    project(wdt_ref, dt_ref)
    dtt_ref[...] = _dot_nt(wdtt_ref[...], h)


def in_proj(x, g, w, tm):
    t, d = x.shape
    weights = [w["w_qk"], w["w_vt"], w["w_z"], w["w_xbc"], w["w_gates"], w["w_dt"], w["w_dtt"]]
    rows = lambda n: pl.BlockSpec((tm, n), lambda i: (i, 0))
    cols = lambda n: pl.BlockSpec((n, tm), lambda i: (0, i))
    resident = lambda a: pl.BlockSpec(a.shape, lambda i: (0, 0), pipeline_mode=pl.Buffered(1))
    n_qk, n_v, n_z, n_xbc, n_g = (w["w_qk"].shape[1], w["w_vt"].shape[0], w["w_z"].shape[1],
                                  w["w_xbc"].shape[1], w["w_gates"].shape[1])
    n_dtt = w["w_dtt"].shape[0]
    return pl.pallas_call(
        _in_proj_kernel,
        grid=(t // tm,),
        in_specs=[rows(d), resident(g)] + [resident(a) for a in weights],
        out_specs=[rows(n_qk), cols(n_v), rows(n_z), rows(n_xbc), rows(n_g), rows(DT_PAD), cols(n_dtt)],
        out_shape=[
            jax.ShapeDtypeStruct((t, n_qk), BF16), jax.ShapeDtypeStruct((n_v, t), BF16),
            jax.ShapeDtypeStruct((t, n_z), F32), jax.ShapeDtypeStruct((t, n_xbc), F32),
            jax.ShapeDtypeStruct((t, n_g), F32), jax.ShapeDtypeStruct((t, DT_PAD), F32),
            jax.ShapeDtypeStruct((n_dtt, t), F32),
        ],
        compiler_params=_params(("parallel",)),
        name="in_proj",
    )(x, g, *weights)


def _attn_kernel(slopes_ref, lam_ref, q_ref, k_ref, vt_ref, lq1_ref, lk1_ref, lq2_ref, lk2_ref,
                 subln_ref, o_ref, kfeat_ref, qfeat_ref, knorm_ref, *s_refs, tq, tk, nk):
    h = pl.program_id(1)
    qi = pl.program_id(2)
    slope2 = slopes_ref[h] * LOG2E
    q0pos = qi * tq
    n_off = nk - 1
    kd = lax.shift_right_logical(qi, int(math.log2(tk // tq)))

    @pl.when(qi == 0)
    def _():
        ii = lax.broadcasted_iota(jnp.int32, (tq, LANES), 0).astype(F32)
        fq = lax.broadcasted_iota(jnp.int32, (tq, LANES), 1)
        row_terms = _bf16_terms(-slope2 * ii)
        slope_terms = _bf16_terms(jnp.full((tq, LANES), slope2, F32))
        q_feat = jnp.zeros((tq, LANES), F32)
        for n in range(3):
            q_feat = jnp.where(fq == n, row_terms[n], q_feat)
            q_feat = jnp.where((fq == 3 + n) | (fq == 6 + n), slope_terms[n], q_feat)
        qfeat_ref[...] = q_feat.astype(BF16)
        jj = lax.broadcasted_iota(jnp.int32, (tk, LANES), 0)
        fk = lax.broadcasted_iota(jnp.int32, (tk, LANES), 1)
        jj_lo = jnp.bitwise_and(jj, BF16_EXACT_INT - 1)
        jj_hi = (jj - jj_lo).astype(F32)
        jj_lo = jj_lo.astype(F32)
        k_feat = jnp.where(fk < 3, 1.0, jnp.where(fk < 6, jj_lo, jnp.where(fk < 9, jj_hi, 0.0)))
        kfeat_ref[0] = k_feat.astype(BF16)
        kfeat_ref[1] = (-k_feat).astype(BF16)

    q = q_ref[...]
    lane = lax.broadcasted_iota(jnp.int32, q.shape, 1)
    zero = jnp.zeros_like(q)
    q_feat = qfeat_ref[...]
    q_ops = jnp.concatenate([
        jnp.concatenate([jnp.where(lane < HEAD_DIM, q, zero), q_feat], axis=1),
        jnp.concatenate([jnp.where(lane >= HEAD_DIM, q, zero), q_feat], axis=1)], axis=0)
    ones_rows = jnp.ones((SUM_ROWS, tk), BF16)

    def raw_scores(kb, side):
        start = pl.multiple_of(kb * tk, tk)
        k_ops = jnp.concatenate([k_ref[pl.ds(start, tk), :], kfeat_ref[side]], axis=1)
        return _dot_nt(k_ops, q_ops)

    def offset(kb):
        return slope2 * jnp.abs(q0pos - kb * tk).astype(F32)

    def off_tile(t):
        side = (t >= kd).astype(jnp.int32)
        return t + side, side

    def produce(t, dst_ref):
        kb, side = off_tile(t)
        s = raw_scores(kb, side)
        dst_ref[...] = s
        return jnp.max(s, axis=0, keepdims=True)

    def update(s, mx, c, kb, state):
        m_old, acc_old = state
        m_new = jnp.maximum(m_old, mx - c)
        p = jnp.exp2(s - (m_new + c))
        alpha = jnp.exp2(m_old - m_new)
        vt = vt_ref[:, pl.ds(pl.multiple_of(kb * tk, tk), tk)]
        vt_ops = jnp.concatenate([vt, ones_rows], axis=0)
        return m_new, alpha * acc_old + _dot(vt_ops, p.astype(BF16))

    def consume(t, src_ref, mx, state):
        kb, _ = off_tile(t)
        return update(src_ref[...], mx, offset(kb), kb, state)

    DEPTH = len(s_refs)
    max_trips = (n_off - 1) // DEPTH

    @pl.when(qi == 0)
    def _():
        def tile_norm(j, best):
            kt = k_ref[pl.ds(pl.multiple_of(j * tk, tk), tk), :].astype(F32)
            return jnp.maximum(best, jnp.max(jnp.sum(kt * kt, axis=1, keepdims=True), axis=0, keepdims=True))
        knorm_ref[...] = jnp.broadcast_to(lax.fori_loop(0, nk, tile_norm, jnp.zeros((1, 1), F32)),
                                          knorm_ref.shape)

    c_d = offset(kd)
    s_d = jnp.minimum(raw_scores(kd, 0) - c_d, raw_scores(kd, 1) + c_d)
    state = (jnp.full((1, 2 * tq), -jnp.inf, F32), jnp.zeros((V_DIM + SUM_ROWS, 2 * tq), F32))
    state = update(s_d, jnp.max(s_d, axis=0, keepdims=True), 0.0, kd, state)

    qf = q.astype(F32)
    qn2 = jnp.max(jnp.sum(qf * qf, axis=1, keepdims=True), axis=0, keepdims=True)
    bound = jnp.sqrt(qn2 * knorm_ref[0:1, 0:1]) * BOUND_REL_SLACK + BOUND_ABS_SLACK
    m_min = jnp.min(state[0], axis=1, keepdims=True)
    reach = jnp.minimum((bound + UNDERFLOW - m_min) / slope2, REACH_CAP)
    reach = (jnp.ceil(reach).astype(jnp.int32) + 1)[0, 0]
    tk_shift = int(math.log2(tk))
    kb_lo = jnp.minimum(lax.shift_right_logical(jnp.maximum(q0pos + 1 - reach, 0), tk_shift), kd)
    kb_hi = jnp.maximum(jnp.minimum(lax.shift_right_logical(reach + q0pos + tq - 2, tk_shift), nk - 1), kd)
    wanted = jnp.maximum(kb_hi - kb_lo - 1, 0)
    if DEPTH == 2:
        whole = lax.shift_right_logical(wanted + 1, 1)
    else:
        whole = lax.shift_right_logical((wanted + 2) * 43, 7)
    trips = jnp.clip(whole, 1, max_trips)
    first = jnp.minimum(kb_lo, n_off - 1 - DEPTH * trips)

    mx = tuple(produce(first + i, s_refs[i]) for i in range(DEPTH))

    def trip(r, carry, n_produce):
        state, mx = carry[:2], list(carry[2])
        for i in range(DEPTH):
            t = first + DEPTH * r + i
            state = consume(t, s_refs[i], mx[i], state)
            if i < n_produce:
                mx[i] = produce(t + DEPTH, s_refs[i])
        return state + (tuple(mx),)

    full_trips = trips - 1
    triples = lax.shift_right_logical(full_trips * 43, 7)
    carry = lax.fori_loop(
        0, triples,
        lambda r3, c: trip(3 * r3 + 2, trip(3 * r3 + 1, trip(3 * r3, c, DEPTH), DEPTH), DEPTH),
        state + (mx,))
    carry = lax.fori_loop(3 * triples, full_trips, functools.partial(trip, n_produce=DEPTH), carry)
    carry = trip(trips - 1, carry, n_produce=1)
    _, acc_fin = consume(first + DEPTH * trips, s_refs[0], carry[2][0], carry[:2])

    lam_init = lam_ref[0]
    one_minus = lam_ref[1]
    lam = (jnp.exp(jnp.sum(lq1_ref[...] * lk1_ref[...], axis=-1, keepdims=True))
           - jnp.exp(jnp.sum(lq2_ref[...] * lk2_ref[...], axis=-1, keepdims=True)) + lam_init)
    o_both = acc_fin[:V_DIM, :] / acc_fin[V_DIM:V_DIM + 1, :]
    o = o_both[:, :tq] - lam * o_both[:, tq:]
    ms = jnp.mean(o * o, axis=0, keepdims=True)
    y = o * lax.rsqrt(ms + EPS) * subln_ref[...] * one_minus
    o_ref[...] = y.T.astype(o_ref.dtype)


def diff_attention(qk, vt, slopes, lam_consts, lq1, lk1, lq2, lk2, subln_col, b, s, tq, tk, depth):
    t = b * s
    nq = s // tq
    nk = s // tk
    assert depth in (2, 3) and depth + 2 <= nk < 126 and (nk - 2) % depth == 0, (s, tk)
    assert tk % tq == 0 and tk <= BF16_EXACT_INT ** 2
    hh = ATTN_HEADS
    smem = pl.BlockSpec(memory_space=pltpu.SMEM)
    vec = pl.BlockSpec((1, HEAD_DIM), lambda bi, h, qi: (0, 0))
    return pl.pallas_call(
        functools.partial(_attn_kernel, tq=tq, tk=tk, nk=nk),
        grid=(b, hh, nq),
        in_specs=[
            smem, smem,
            pl.BlockSpec((tq, V_DIM), lambda bi, h, qi: (bi * nq + qi, h)),
            pl.BlockSpec((s, V_DIM), lambda bi, h, qi: (bi, hh + h)),
            pl.BlockSpec((V_DIM, s), lambda bi, h, qi: (h, bi)),
            vec, vec, vec, vec,
            pl.BlockSpec((V_DIM, 1), lambda bi, h, qi: (0, 0)),
        ],
        out_specs=pl.BlockSpec((tq, V_DIM), lambda bi, h, qi: (bi * nq + qi, h)),
        out_shape=jax.ShapeDtypeStruct((t, hh * V_DIM), BF16),
        scratch_shapes=([pltpu.VMEM((2, tk, LANES), BF16), pltpu.VMEM((tq, LANES), BF16),
                         pltpu.VMEM((8, LANES), F32)]
                        + [pltpu.VMEM((tk, 2 * tq), F32)] * depth),
        compiler_params=_params(("parallel", "parallel", "arbitrary")),
        name="diff_attention",
    )(slopes, lam_consts, qk, qk, vt, lq1, lk1, lq2, lk2, subln_col)


def _ssd_kernel(*refs, rev, final, nc):
    if final:
        (xbc_ref, prev_ref, next_ref, dt_ref, dtt_ref, cw_ref, cb_ref, bias_p_ref, bias_c_ref,
         alog_p_ref, alog_c_ref, z_ref, yf_ref, dskip_ref, nw_ref, y_ref, st_ref) = refs
    else:
        (xbc_ref, prev_ref, next_ref, dt_ref, dtt_ref, cw_ref, cb_ref, bias_p_ref, bias_c_ref,
         alog_p_ref, alog_c_ref, y_ref, st_ref) = refs
    direction = 1 if rev else 0
    c = pl.program_id(1)
    cc = (nc - 1 - c) if rev else c
    ll = CHUNK

    @pl.when(c == 0)
    def _():
        st_ref[...] = jnp.zeros_like(st_ref)

    x = xbc_ref[...]
    before = jnp.where(cc == 0, 0.0, prev_ref[...])
    after = jnp.where(cc == nc - 1, 0.0, next_ref[...])
    x_ext = jnp.concatenate([x, after, before], axis=0)
    n_ext = x_ext.shape[0]
    xp = pltpu.roll(x_ext, 1, axis=0)[0:ll, :]
    xn = pltpu.roll(x_ext, n_ext - 1, axis=0)[0:ll, :]
    cw = cw_ref[...]
    u = _silu(cb_ref[...] + xp * cw[0:1, :] + x * cw[1:2, :] + xn * cw[2:3, :])
    xs = u[:, :D_INNER]
    bm = u[:, D_INNER:D_INNER + SSD_GROUPS * D_STATE]
    cm = u[:, D_INNER + SSD_GROUPS * D_STATE:]

    jj = lax.broadcasted_iota(jnp.int32, (DT_PAD, D_INNER), 0)
    col = lax.broadcasted_iota(jnp.int32, (DT_PAD, D_INNER), 1)
    head_of_col = lax.shift_right_logical(col, int(math.log2(SSD_HEAD_DIM)))
    expand = jnp.where(jj == direction * SSD_HEADS + head_of_col, 1.0, 0.0).astype(BF16)
    dt_cols = _softplus(dt_ref[...] + bias_p_ref[...])
    a_cols = dt_cols * (-jnp.exp(alog_p_ref[...]))
    dt_rows = _softplus(dtt_ref[direction * SSD_HEADS:(direction + 1) * SSD_HEADS, :] + bias_c_ref[...])
    a_rows = dt_rows * (-jnp.exp(alog_c_ref[...]))

    ri = lax.broadcasted_iota(jnp.int32, (ll, ll), 0)
    ci = lax.broadcasted_iota(jnp.int32, (ll, ll), 1)
    if rev:
        keep = ci >= ri
        edge = 0
    else:
        keep = ci <= ri
        edge = ll - 1
    tri = jnp.where(keep, 1.0, 0.0).astype(BF16)
    tri_t = jnp.where((ri >= ci) if rev else (ri <= ci), 1.0, 0.0).astype(BF16)
    cum_cols = _dot_01_by_f32(tri, a_cols)
    dt_full = _dot_f32_by_01(dt_cols, expand)
    cum_full = _dot_f32_by_01(cum_cols, expand)
    cum_rows = _dot_f32_by_01(a_rows, tri_t)

    xd = xs * dt_full
    xd_b = xd.astype(BF16)
    cum_edge = cum_full[edge:edge + 1, :]
    xdw = (xd * jnp.exp(cum_edge - cum_full)).astype(BF16)
    grow = jnp.exp(cum_full)
    lane = lax.broadcasted_iota(jnp.int32, (ll, LANES), 1)

    y_parts = []
    for g in range(SSD_GROUPS):
        bg = bm[:, g * D_STATE:(g + 1) * D_STATE]
        cg = cm[:, g * D_STATE:(g + 1) * D_STATE].astype(BF16)
        cb = _dot_nt(cg, bg.astype(BF16))
        gs = slice(g * GROUP_COLS, (g + 1) * GROUP_COLS)
        st_in = st_ref[:, gs]
        y_off = _dot(cg, st_in.astype(BF16)) * grow[:, gs]
        heads_per_group = SSD_HEADS // SSD_GROUPS
        for pair in range(heads_per_group // 2):
            lo = g * GROUP_COLS + pair * LANES
            xd_pair = xd_b[:, lo:lo + LANES]
            outs = []
            for sub in range(2):
                hd = g * heads_per_group + pair * 2 + sub
                seg = cum_full[:, hd * SSD_HEAD_DIM:hd * SSD_HEAD_DIM + 1] - cum_rows[hd:hd + 1, :]
                dec = jnp.exp(jnp.where(keep, seg, -jnp.inf))
                outs.append(_dot((cb * dec).astype(BF16), xd_pair))
            y_diag = jnp.where(lane < SSD_HEAD_DIM, outs[0], outs[1])
            y_parts.append(y_diag + y_off[:, pair * LANES:(pair + 1) * LANES])
        st_chunk = _dot(bg.T.astype(BF16), xdw[:, gs])
        st_ref[:, gs] = st_in * jnp.exp(cum_edge[:, gs]) + st_chunk
    y = jnp.concatenate(y_parts, axis=1)

    if final:
        y = yf_ref[...] + y + xs * dskip_ref[...]
        y = y * _silu(z_ref[...])
        nw = nw_ref[...]
        normed = []
        for g in range(SSD_GROUPS):
            gs = slice(g * GROUP_COLS, (g + 1) * GROUP_COLS)
            normed.append(_rms(y[:, gs], nw[:, gs]))
        y = jnp.concatenate(normed, axis=1)
    y_ref[...] = y.astype(y_ref.dtype)


def ssd_pass(xbc, dt, dtt, cw, cb, bias_p, bias_c, alog_p, alog_c, extras, b, s, rev):
    final = extras is not None
    t = b * s
    nc = s // CHUNK
    rows8 = CHUNK // 8
    last8 = t // 8 - 1

    def cidx(bi, c):
        return bi * nc + ((nc - 1 - c) if rev else c)

    def full(shape):
        return pl.BlockSpec(shape, lambda bi, c: (0, 0))

    chunk_rows = lambda w: pl.BlockSpec((CHUNK, w), lambda bi, c: (cidx(bi, c), 0))
    in_specs = [
        chunk_rows(CONV_DIM),
        pl.BlockSpec((8, CONV_DIM), lambda bi, c: (jnp.maximum(cidx(bi, c) * rows8 - 1, 0), 0)),
        pl.BlockSpec((8, CONV_DIM), lambda bi, c: (jnp.minimum((cidx(bi, c) + 1) * rows8, last8), 0)),
        chunk_rows(DT_PAD),
        pl.BlockSpec((2 * SSD_HEADS, CHUNK), lambda bi, c: (0, cidx(bi, c))),
        full((3, CONV_DIM)), full((1, CONV_DIM)),
        full((1, DT_PAD)), full((SSD_HEADS, 1)), full((1, DT_PAD)), full((SSD_HEADS, 1)),
    ]
    args = [xbc, xbc, xbc, dt, dtt, cw, cb, bias_p, bias_c, alog_p, alog_c]
    if final:
        z, yf, dskip_f, nw = extras
        in_specs += [chunk_rows(D_INNER), chunk_rows(D_INNER), full((1, D_INNER)), full((1, D_INNER))]
        args += [z, yf, dskip_f, nw]
    return pl.pallas_call(
        functools.partial(_ssd_kernel, rev=rev, final=final, nc=nc),
        grid=(b, nc),
        in_specs=in_specs,
        out_specs=chunk_rows(D_INNER),
        out_shape=jax.ShapeDtypeStruct((t, D_INNER), BF16 if final else F32),
        scratch_shapes=[pltpu.VMEM((D_STATE, D_INNER), F32)],
        compiler_params=_params(("parallel", "arbitrary")),
        name="ssd_bwd_final" if final else "ssd_fwd",
    )(*args)


def _merge_kernel(attn_ref, ssd_ref, gates_ref, x_ref, woa_ref, wos_ref, wout_ref, nw_ref, o_ref):
    a = _dot(attn_ref[...], woa_ref[...])
    s = _dot(ssd_ref[...], wos_ref[...])
    gates = gates_ref[...]
    merged = jax.nn.sigmoid(gates[:, :D_MODEL]) * a + jax.nn.sigmoid(gates[:, D_MODEL:]) * s
    mo = _dot(merged.astype(BF16), wout_ref[...])
    o_ref[...] = x_ref[...] + _rms(mo, nw_ref[...])


def merge_out(attn, ssd, gates, x, woa, wos, wout, nw, tm):
    t, d = x.shape
    rows = lambda w: pl.BlockSpec((tm, w), lambda i: (i, 0))
    full = lambda shape: pl.BlockSpec(shape, lambda i: (0, 0))
    return pl.pallas_call(
        _merge_kernel,
        grid=(t // tm,),
        in_specs=[rows(d), rows(d), rows(2 * d), rows(d), full((d, d)), full((d, d)), full((d, d)),
                  full((1, d))],
        out_specs=rows(d),
        out_shape=jax.ShapeDtypeStruct((t, d), F32),
        compiler_params=_params(("parallel",)),
        name="merge_out",
    )(attn, ssd, gates, x, woa, wos, wout, nw)


def _ffn_kernel(x_ref, xp_ref, xn_ref, gpre_ref, wa_ref, wg_ref, cwa_ref, cwg_ref, cba_ref, cbg_ref,
                wd_ref, gpost_ref, o_ref, h_ref, acc_ref, *, tm, tiles_per_seq, nf):
    i = pl.program_id(0)
    f = pl.program_id(1)

    @pl.when(f == 0)
    def _():
        gpre = gpre_ref[...]
        pos = i % tiles_per_seq
        h_ref[0:tm, :] = _rms(x_ref[...], gpre).astype(BF16)
        hp = jnp.where(pos == 0, 0.0, _rms(xp_ref[...], gpre))
        hn = jnp.where(pos == tiles_per_seq - 1, 0.0, _rms(xn_ref[...], gpre))
        h_ref[tm:tm + HALO, :] = hn.astype(BF16)
        h_ref[tm + HALO:tm + 2 * HALO, :] = hp.astype(BF16)
        acc_ref[...] = jnp.zeros_like(acc_ref)

    h = h_ref[...]
    n_ext = tm + 2 * HALO

    def conv_branch(w_ref, cw_ref, cb_ref):
        u = _dot(h, w_ref[...])
        um = u[0:tm, :]
        up = pltpu.roll(u, 1, axis=0)[0:tm, :]
        un = pltpu.roll(u, n_ext - 1, axis=0)[0:tm, :]
        cw = cw_ref[...]
        return cb_ref[...] + up * cw[0:1, :] + um * cw[1:2, :] + un * cw[2:3, :]

    a = conv_branch(wa_ref, cwa_ref, cba_ref)
    g = conv_branch(wg_ref, cwg_ref, cbg_ref)
    act = (_silu(g) * a).astype(BF16)
    acc_ref[...] += _dot(act, wd_ref[...])

    @pl.when(f == nf - 1)
    def _():
        o_ref[...] = x_ref[...] + _rms(acc_ref[...], gpost_ref[...])


def ffn(x, gpre, wa, wg, cwa, cwg, cba, cbg, wd, gpost, s, tm, tf):
    t, d = x.shape
    nf = D_FF_PAD // tf
    tiles_per_seq = s // tm
    blocks = tm // HALO
    last = t // HALO - 1
    full = lambda shape: pl.BlockSpec(shape, lambda i, f: (0, 0))
    colblk = lambda r: pl.BlockSpec((r, tf), lambda i, f: (0, f))
    return pl.pallas_call(
        functools.partial(_ffn_kernel, tm=tm, tiles_per_seq=tiles_per_seq, nf=nf),
        grid=(t // tm, nf),
        in_specs=[
            pl.BlockSpec((tm, d), lambda i, f: (i, 0)),
            pl.BlockSpec((HALO, d), lambda i, f: (jnp.maximum(i * blocks - 1, 0), 0)),
            pl.BlockSpec((HALO, d), lambda i, f: (jnp.minimum((i + 1) * blocks, last), 0)),
            full((1, d)),
            colblk(d), colblk(d), colblk(3), colblk(3), colblk(1), colblk(1),
            pl.BlockSpec((tf, d), lambda i, f: (f, 0)),
            full((1, d)),
        ],
        out_specs=pl.BlockSpec((tm, d), lambda i, f: (i, 0)),
        out_shape=jax.ShapeDtypeStruct((t, d), F32),
        scratch_shapes=[pltpu.VMEM((tm + 2 * HALO, d), BF16), pltpu.VMEM((tm, d), F32)],
        compiler_params=_params(("parallel", "arbitrary")),
        name="ffn",
    )(x, x, x, gpre, wa, wg, cwa, cwg, cba, cbg, wd, gpost)


def _tiles(s):
    return dict(
        tm_proj=min(256, s),
        tq=min(256, s), tk=min(512, s), depth=3,
        tm_merge=min(256, s),
        tm_ffn=min(1024, s), tf=256,
    )


def _layer(x, b, s, w, cfg):
    g_pre = w["norm_mix_pre"]
    qk, vt, z, xbc, gates, dt, dtt = in_proj(x, g_pre, w, cfg["tm_proj"])

    attn = diff_attention(qk, vt, w["slopes"], w["lam_consts"], w["lam_q1"], w["lam_k1"], w["lam_q2"],
                          w["lam_k2"], w["subln_col"], b, s, cfg["tq"], cfg["tk"], cfg["depth"])

    yf = ssd_pass(xbc, dt, dtt, w["conv_ssd_w"], w["conv_ssd_b"], w["bias_p"], w["bias_c"][0],
                  w["alog_p"], w["alog_c"][0], None, b, s, rev=False)
    ssd = ssd_pass(xbc, dt, dtt, w["conv_ssd_w"], w["conv_ssd_b"], w["bias_p"], w["bias_c"][1],
                   w["alog_p"], w["alog_c"][1], (z, yf, w["dskip_f"], w["ssd_norm"]), b, s, rev=True)

    x = merge_out(attn, ssd, gates, x, w["w_o_attn"], w["w_o_ssd"], w["w_out"], w["norm_mix_post"],
                  cfg["tm_merge"])
    x = ffn(x, w["norm_ffn_pre"], w["w_up_a"], w["w_up_g"], w["cw_a"], w["cw_g"], w["cb_a"], w["cb_g"],
            w["w_down"], w["norm_ffn_post"], s, cfg["tm_ffn"], cfg["tf"])
    return x


def _prepare_weights(norm_mix_pre, norm_mix_post, norm_ffn_pre, norm_ffn_post, w_in, lam_q1, lam_k1,
                     lam_q2, lam_k2, attn_subln, conv_ssd_w, conv_ssd_b, dt_bias, a_log, d_skip, ssd_norm,
                     w_o_attn, w_o_ssd, w_out, w_up, conv_ffn_w, conv_ffn_b, w_down):
    depth = w_in.shape[0]
    qk_cols = ATTN_HEADS * 2 * HEAD_DIM
    attn_w = ATTN_HEADS * V_DIM
    cuts = [0, 2 * qk_cols]
    for width in (attn_w, D_INNER, CONV_DIM, 2 * SSD_HEADS, 2 * D_MODEL):
        cuts.append(cuts[-1] + width)
    seg = lambda i: w_in[:, :, cuts[i]:cuts[i + 1]]
    w_dt = seg(4)
    row = lambda a: a[:, None, :]
    rep = lambda a: jnp.repeat(a, SSD_HEAD_DIM, axis=-1)
    pad_ff = lambda a: jnp.pad(a, [(0, 0)] * (a.ndim - 1) + [(0, D_FF_PAD - D_FF)])
    pad_dt = lambda a: jnp.pad(a.reshape(depth, 1, 2 * SSD_HEADS), ((0, 0), (0, 0), (0, DT_PAD - 2 * SSD_HEADS)))
    lam_init = [0.8 - 0.6 * math.exp(-0.3 * l) for l in range(depth)]
    return dict(
        norm_mix_pre=row(norm_mix_pre), norm_mix_post=row(norm_mix_post),
        norm_ffn_pre=row(norm_ffn_pre), norm_ffn_post=row(norm_ffn_post),
        w_qk=seg(0).astype(BF16),
        w_vt=jnp.swapaxes(seg(1), 1, 2).astype(BF16),
        w_z=seg(2).astype(BF16),
        w_xbc=seg(3).astype(BF16),
        w_dt=jnp.pad(w_dt, ((0, 0), (0, 0), (0, DT_PAD - 2 * SSD_HEADS))).astype(BF16),
        w_dtt=jnp.swapaxes(w_dt, 1, 2).astype(BF16),
        w_gates=seg(5).astype(BF16),
        slopes=jnp.tile(jnp.asarray([2.0 ** (-8.0 * (i + 1) / ATTN_HEADS) for i in range(ATTN_HEADS)],
                                    F32)[None], (depth, 1)),
        lam_consts=jnp.asarray([[li, 1.0 - li] for li in lam_init], F32),
        lam_q1=row(lam_q1), lam_k1=row(lam_k1), lam_q2=row(lam_q2), lam_k2=row(lam_k2),
        subln_col=attn_subln[:, :, None],
        conv_ssd_w=conv_ssd_w, conv_ssd_b=row(conv_ssd_b),
        bias_p=pad_dt(dt_bias), bias_c=dt_bias[..., None],
        alog_p=pad_dt(a_log), alog_c=a_log[..., None],
        dskip_f=row(rep(d_skip)), ssd_norm=row(ssd_norm),
        w_o_attn=w_o_attn.astype(BF16), w_o_ssd=w_o_ssd.astype(BF16), w_out=w_out.astype(BF16),
        w_up_a=pad_ff(w_up[:, :, :D_FF]).astype(BF16), w_up_g=pad_ff(w_up[:, :, D_FF:]).astype(BF16),
        cw_a=pad_ff(conv_ffn_w[:, :, :D_FF]), cw_g=pad_ff(conv_ffn_w[:, :, D_FF:]),
        cb_a=row(pad_ff(conv_ffn_b[:, :D_FF])), cb_g=row(pad_ff(conv_ffn_b[:, D_FF:])),
        w_down=jnp.pad(w_down, ((0, 0), (0, D_FF_PAD - D_FF), (0, 0))).astype(BF16),
    )


def kernel(x_prompt, x_sample, norm_mix_pre, norm_mix_post, norm_ffn_pre, norm_ffn_post, w_in, lam_q1, lam_k1, lam_q2, lam_k2, attn_subln, conv_ssd_w, conv_ssd_b, dt_bias, a_log, d_skip, ssd_norm, w_o_attn, w_o_ssd, w_out, w_up, conv_ffn_w, conv_ffn_b, w_down):
    weights = _prepare_weights(norm_mix_pre, norm_mix_post, norm_ffn_pre, norm_ffn_post, w_in, lam_q1,
                               lam_k1, lam_q2, lam_k2, attn_subln, conv_ssd_w, conv_ssd_b, dt_bias, a_log,
                               d_skip, ssd_norm, w_o_attn, w_o_ssd, w_out, w_up, conv_ffn_w, conv_ffn_b,
                               w_down)
    groups = []
    for xg in (x_prompt, x_sample):
        b, s, d = xg.shape
        groups.append((b, s, _tiles(s)))

    def step(carry, w):
        out = tuple(_layer(x, b, s, w, cfg) for x, (b, s, cfg) in zip(carry, groups))
        return out, None

    init = tuple(xg.reshape(-1, xg.shape[-1]) for xg in (x_prompt, x_sample))
    out, _ = lax.scan(step, init, weights)
    return tuple(o.reshape(xg.shape) for o, xg in zip(out, (x_prompt, x_sample)))
```

```python
import functools
import math

import jax
import jax.numpy as jnp
from jax import lax
from jax.experimental import pallas as pl
from jax.experimental.pallas import tpu as pltpu

F32 = jnp.float32
BF16 = jnp.bfloat16

D_MODEL = 1024
ATTN_HEADS = 8
HEAD_DIM = 64
V_DIM = 2 * HEAD_DIM
SSD_HEADS = 16
SSD_HEAD_DIM = 64
D_INNER = SSD_HEADS * SSD_HEAD_DIM
SSD_GROUPS = 2
GROUP_COLS = D_INNER // SSD_GROUPS
D_STATE = 128
CHUNK = 128
CONV_DIM = D_INNER + 2 * SSD_GROUPS * D_STATE
D_FF = 2752
EPS = 1e-6

LANES = 128
D_FF_PAD = 2816
DT_PAD = LANES
HALO = 16
PROJ_CHUNK = 512

UNDERFLOW = 136.0
BF16_EXACT_INT = 256
BOUND_REL_SLACK = 1.001
BOUND_ABS_SLACK = 1.0
REACH_CAP = 1e9
SUM_ROWS = 16
LOG2E = math.log2(math.e)
Q_PRESCALE = LOG2E / math.sqrt(HEAD_DIM)
VMEM_LIMIT = 56 * 1024 * 1024


def _params(sem):
    return pltpu.CompilerParams(dimension_semantics=sem, vmem_limit_bytes=VMEM_LIMIT)


def _rms(x, g):
    ms = jnp.mean(x * x, axis=-1, keepdims=True)
    return x * lax.rsqrt(ms + EPS) * g


def _dot(a, b):
    return jnp.dot(a, b, preferred_element_type=F32)


def _dot_nt(a, b):
    return lax.dot_general(a, b, (((1,), (1,)), ((), ())), preferred_element_type=F32)


def _bf16_terms(x):
    def top(v):
        bits = lax.bitcast_convert_type(v, jnp.uint32) & jnp.uint32(0xFFFF0000)
        return lax.bitcast_convert_type(bits, F32)

    hi = top(x)
    mid = top(x - hi)
    lo = x - hi - mid
    return hi, mid, lo


def _dot_f32_by_01(x, m01):
    return sum(_dot(term.astype(BF16), m01) for term in _bf16_terms(x))


def _dot_01_by_f32(m01, x):
    return sum(_dot(m01, term.astype(BF16)) for term in _bf16_terms(x))


def _softplus(x):
    return jnp.maximum(x, 0.0) + jnp.log1p(jnp.exp(-jnp.abs(x)))


def _silu(x):
    return x * jax.nn.sigmoid(x)


def _in_proj_kernel(x_ref, g_ref, wqk_ref, wvt_ref, wz_ref, wxbc_ref, wg_ref, wdt_ref, wdtt_ref,
                    qk_ref, vt_ref, z_ref, xbc_ref, gates_ref, dt_ref, dtt_ref):
    h = _rms(x_ref[...], g_ref[...]).astype(BF16)
    qk_cols = ATTN_HEADS * 2 * HEAD_DIM

    def project(w_ref, o_ref, scale_upto=0):
        n = w_ref.shape[1]
        for c0 in range(0, n, PROJ_CHUNK):
            c1 = min(c0 + PROJ_CHUNK, n)
            y = _dot(h, w_ref[:, c0:c1])
            if c1 <= scale_upto:
                y = y * Q_PRESCALE
            o_ref[:, c0:c1] = y.astype(o_ref.dtype)

    project(wqk_ref, qk_ref, scale_upto=qk_cols)
    for r0 in range(0, wvt_ref.shape[0], PROJ_CHUNK):
        vt_ref[r0:r0 + PROJ_CHUNK, :] = _dot_nt(wvt_ref[r0:r0 + PROJ_CHUNK, :], h).astype(vt_ref.dtype)
    project(wz_ref, z_ref)
    project(wxbc_ref, xbc_ref)
    project(wg_ref, gates_ref)
    project(wdt_ref, dt_ref)
    dtt_ref[...] = _dot_nt(wdtt_ref[...], h)


def in_proj(x, g, w, tm):
    t, d = x.shape
    weights = [w["w_qk"], w["w_vt"], w["w_z"], w["w_xbc"], w["w_gates"], w["w_dt"], w["w_dtt"]]
    rows = lambda n: pl.BlockSpec((tm, n), lambda i: (i, 0))
    cols = lambda n: pl.BlockSpec((n, tm), lambda i: (0, i))
    resident = lambda a: pl.BlockSpec(a.shape, lambda i: (0, 0), pipeline_mode=pl.Buffered(1))
    n_qk, n_v, n_z, n_xbc, n_g = (w["w_qk"].shape[1], w["w_vt"].shape[0], w["w_z"].shape[1],
                                  w["w_xbc"].shape[1], w["w_gates"].shape[1])
    n_dtt = w["w_dtt"].shape[0]
    return pl.pallas_call(
        _in_proj_kernel,
        grid=(t // tm,),
        in_specs=[rows(d), resident(g)] + [resident(a) for a in weights],
        out_specs=[rows(n_qk), cols(n_v), rows(n_z), rows(n_xbc), rows(n_g), rows(DT_PAD), cols(n_dtt)],
        out_shape=[
            jax.ShapeDtypeStruct((t, n_qk), BF16), jax.ShapeDtypeStruct((n_v, t), BF16),
            jax.ShapeDtypeStruct((t, n_z), F32), jax.ShapeDtypeStruct((t, n_xbc), F32),
            jax.ShapeDtypeStruct((t, n_g), F32), jax.ShapeDtypeStruct((t, DT_PAD), F32),
            jax.ShapeDtypeStruct((n_dtt, t), F32),
        ],
        compiler_params=_params(("parallel",)),
        name="in_proj",
    )(x, g, *weights)


def _attn_kernel(slopes_ref, lam_ref, q_ref, k_ref, vt_ref, lq1_ref, lk1_ref, lq2_ref, lk2_ref,
                 subln_ref, o_ref, kfeat_ref, qfeat_ref, knorm_ref, *s_refs, tq, tk, nk):
    h = pl.program_id(1)
    qi = pl.program_id(2)
    slope2 = slopes_ref[h] * LOG2E
    q0pos = qi * tq
    n_off = nk - 1
    kd = lax.shift_right_logical(qi, int(math.log2(tk // tq)))

    @pl.when(qi == 0)
    def _():
        ii = lax.broadcasted_iota(jnp.int32, (tq, LANES), 0).astype(F32)
        fq = lax.broadcasted_iota(jnp.int32, (tq, LANES), 1)
        row_terms = _bf16_terms(-slope2 * ii)
        slope_terms = _bf16_terms(jnp.full((tq, LANES), slope2, F32))
        q_feat = jnp.zeros((tq, LANES), F32)
        for n in range(3):
            q_feat = jnp.where(fq == n, row_terms[n], q_feat)
            q_feat = jnp.where((fq == 3 + n) | (fq == 6 + n), slope_terms[n], q_feat)
        qfeat_ref[...] = q_feat.astype(BF16)
        jj = lax.broadcasted_iota(jnp.int32, (tk, LANES), 0)
        fk = lax.broadcasted_iota(jnp.int32, (tk, LANES), 1)
        jj_lo = jnp.bitwise_and(jj, BF16_EXACT_INT - 1)
        jj_hi = (jj - jj_lo).astype(F32)
        jj_lo = jj_lo.astype(F32)
        k_feat = jnp.where(fk < 3, 1.0, jnp.where(fk < 6, jj_lo, jnp.where(fk < 9, jj_hi, 0.0)))
        kfeat_ref[0] = k_feat.astype(BF16)
        kfeat_ref[1] = (-k_feat).astype(BF16)

    q = q_ref[...]
    lane = lax.broadcasted_iota(jnp.int32, q.shape, 1)
    zero = jnp.zeros_like(q)
    q_feat = qfeat_ref[...]
    q_ops = jnp.concatenate([
        jnp.concatenate([jnp.where(lane < HEAD_DIM, q, zero), q_feat], axis=1),
        jnp.concatenate([jnp.where(lane >= HEAD_DIM, q, zero), q_feat], axis=1)], axis=0)
    ones_rows = jnp.ones((SUM_ROWS, tk), BF16)

    def raw_scores(kb, side):
        start = pl.multiple_of(kb * tk, tk)
        k_ops = jnp.concatenate([k_ref[pl.ds(start, tk), :], kfeat_ref[side]], axis=1)
        return _dot_nt(k_ops, q_ops)

    def offset(kb):
        return slope2 * jnp.abs(q0pos - kb * tk).astype(F32)

    def off_tile(t):
        side = (t >= kd).astype(jnp.int32)
        return t + side, side

    def produce(t, dst_ref):
        kb, side = off_tile(t)
        s = raw_scores(kb, side)
        dst_ref[...] = s
        return jnp.max(s, axis=0, keepdims=True)

    def update(s, mx, c, kb, state):
        m_old, acc_old = state
        m_new = jnp.maximum(m_old, mx - c)
        p = jnp.exp2(s - (m_new + c))
        alpha = jnp.exp2(m_old - m_new)
        vt = vt_ref[:, pl.ds(pl.multiple_of(kb * tk, tk), tk)]
        vt_ops = jnp.concatenate([vt, ones_rows], axis=0)
        return m_new, alpha * acc_old + _dot(vt_ops, p.astype(BF16))

    def consume(t, src_ref, mx, state):
        kb, _ = off_tile(t)
        return update(src_ref[...], mx, offset(kb), kb, state)

    DEPTH = len(s_refs)
    max_trips = (n_off - 1) // DEPTH

    @pl.when(qi == 0)
    def _():
        def tile_norm(j, best):
            kt = k_ref[pl.ds(pl.multiple_of(j * tk, tk), tk), :].astype(F32)
            return jnp.maximum(best, jnp.max(jnp.sum(kt * kt, axis=1, keepdims=True), axis=0, keepdims=True))
        knorm_ref[...] = jnp.broadcast_to(lax.fori_loop(0, nk, tile_norm, jnp.zeros((1, 1), F32)),
                                          knorm_ref.shape)

    c_d = offset(kd)
    s_d = jnp.minimum(raw_scores(kd, 0) - c_d, raw_scores(kd, 1) + c_d)
    state = (jnp.full((1, 2 * tq), -jnp.inf, F32), jnp.zeros((V_DIM + SUM_ROWS, 2 * tq), F32))
    state = update(s_d, jnp.max(s_d, axis=0, keepdims=True), 0.0, kd, state)

    qf = q.astype(F32)
    qn2 = jnp.max(jnp.sum(qf * qf, axis=1, keepdims=True), axis=0, keepdims=True)
    bound = jnp.sqrt(qn2 * knorm_ref[0:1, 0:1]) * BOUND_REL_SLACK + BOUND_ABS_SLACK
    m_min = jnp.min(state[0], axis=1, keepdims=True)
    reach = jnp.minimum((bound + UNDERFLOW - m_min) / slope2, REACH_CAP)
    reach = (jnp.ceil(reach).astype(jnp.int32) + 1)[0, 0]
    tk_shift = int(math.log2(tk))
    kb_lo = jnp.minimum(lax.shift_right_logical(jnp.maximum(q0pos + 1 - reach, 0), tk_shift), kd)
    kb_hi = jnp.maximum(jnp.minimum(lax.shift_right_logical(reach + q0pos + tq - 2, tk_shift), nk - 1), kd)
    wanted = jnp.maximum(kb_hi - kb_lo - 1, 0)
    if DEPTH == 2:
        whole = lax.shift_right_logical(wanted + 1, 1)
    else:
        whole = lax.shift_right_logical((wanted + 2) * 43, 7)
    trips = jnp.clip(whole, 1, max_trips)
    first = jnp.minimum(kb_lo, n_off - 1 - DEPTH * trips)

    mx = tuple(produce(first + i, s_refs[i]) for i in range(DEPTH))

    def trip(r, carry, n_produce):
        state, mx = carry[:2], list(carry[2])
        for i in range(DEPTH):
            t = first + DEPTH * r + i
            state = consume(t, s_refs[i], mx[i], state)
            if i < n_produce:
                mx[i] = produce(t + DEPTH, s_refs[i])
        return state + (tuple(mx),)

    full_trips = trips - 1
    triples = lax.shift_right_logical(full_trips * 43, 7)
    carry = lax.fori_loop(
        0, triples,
        lambda r3, c: trip(3 * r3 + 2, trip(3 * r3 + 1, trip(3 * r3, c, DEPTH), DEPTH), DEPTH),
        state + (mx,))
    carry = lax.fori_loop(3 * triples, full_trips, functools.partial(trip, n_produce=DEPTH), carry)
    carry = trip(trips - 1, carry, n_produce=1)
    _, acc_fin = consume(first + DEPTH * trips, s_refs[0], carry[2][0], carry[:2])

    lam_init = lam_ref[0]
    one_minus = lam_ref[1]
    lam = (jnp.exp(jnp.sum(lq1_ref[...] * lk1_ref[...], axis=-1, keepdims=True))
           - jnp.exp(jnp.sum(lq2_ref[...] * lk2_ref[...], axis=-1, keepdims=True)) + lam_init)
    o_both = acc_fin[:V_DIM, :] / acc_fin[V_DIM:V_DIM + 1, :]
    o = o_both[:, :tq] - lam * o_both[:, tq:]
    ms = jnp.mean(o * o, axis=0, keepdims=True)
    y = o * lax.rsqrt(ms + EPS) * subln_ref[...] * one_minus
    o_ref[...] = y.T.astype(o_ref.dtype)


def diff_attention(qk, vt, slopes, lam_consts, lq1, lk1, lq2, lk2, subln_col, b, s, tq, tk, depth):
    t = b * s
    nq = s // tq
    nk = s // tk
    assert depth in (2, 3) and depth + 2 <= nk < 126 and (nk - 2) % depth == 0, (s, tk)
    assert tk % tq == 0 and tk <= BF16_EXACT_INT ** 2
    hh = ATTN_HEADS
    smem = pl.BlockSpec(memory_space=pltpu.SMEM)
    vec = pl.BlockSpec((1, HEAD_DIM), lambda bi, h, qi: (0, 0))
    return pl.pallas_call(
        functools.partial(_attn_kernel, tq=tq, tk=tk, nk=nk),
        grid=(b, hh, nq),
        in_specs=[
            smem, smem,
            pl.BlockSpec((tq, V_DIM), lambda bi, h, qi: (bi * nq + qi, h)),
            pl.BlockSpec((s, V_DIM), lambda bi, h, qi: (bi, hh + h)),
            pl.BlockSpec((V_DIM, s), lambda bi, h, qi: (h, bi)),
            vec, vec, vec, vec,
            pl.BlockSpec((V_DIM, 1), lambda bi, h, qi: (0, 0)),
        ],
        out_specs=pl.BlockSpec((tq, V_DIM), lambda bi, h, qi: (bi * nq + qi, h)),
        out_shape=jax.ShapeDtypeStruct((t, hh * V_DIM), BF16),
        scratch_shapes=([pltpu.VMEM((2, tk, LANES), BF16), pltpu.VMEM((tq, LANES), BF16),
                         pltpu.VMEM((8, LANES), F32)]
                        + [pltpu.VMEM((tk, 2 * tq), F32)] * depth),
        compiler_params=_params(("parallel", "parallel", "arbitrary")),
        name="diff_attention",
    )(slopes, lam_consts, qk, qk, vt, lq1, lk1, lq2, lk2, subln_col)


def _ssd_kernel(*refs, rev, final, nc):
    if final:
        (u_ref, dt_ref, dtt_ref, bias_p_ref, bias_c_ref, alog_p_ref, alog_c_ref,
         z_ref, yf_ref, dskip_ref, nw_ref, y_ref, st_ref) = refs
    else:
        (xbc_ref, prev_ref, next_ref, cw_ref, cb_ref, dt_ref, dtt_ref, bias_p_ref, bias_c_ref,
         alog_p_ref, alog_c_ref, y_ref, u_ref, st_ref) = refs
    direction = 1 if rev else 0
    c = pl.program_id(1)
    cc = (nc - 1 - c) if rev else c
    ll = CHUNK

    @pl.when(c == 0)
    def _():
        st_ref[...] = jnp.zeros_like(st_ref)

    if final:
        u = u_ref[...]
    else:
        x = xbc_ref[...]
        before = jnp.where(cc == 0, 0.0, prev_ref[...])
        after = jnp.where(cc == nc - 1, 0.0, next_ref[...])
        x_ext = jnp.concatenate([x, after, before], axis=0)
        n_ext = x_ext.shape[0]
        xp = pltpu.roll(x_ext, 1, axis=0)[0:ll, :]
        xn = pltpu.roll(x_ext, n_ext - 1, axis=0)[0:ll, :]
        cw = cw_ref[...]
        u = _silu(cb_ref[...] + xp * cw[0:1, :] + x * cw[1:2, :] + xn * cw[2:3, :])
        u_ref[...] = u
    xs = u[:, :D_INNER]
    bm = u[:, D_INNER:D_INNER + SSD_GROUPS * D_STATE]
    cm = u[:, D_INNER + SSD_GROUPS * D_STATE:]

    jj = lax.broadcasted_iota(jnp.int32, (DT_PAD, D_INNER), 0)
    col = lax.broadcasted_iota(jnp.int32, (DT_PAD, D_INNER), 1)
    head_of_col = lax.shift_right_logical(col, int(math.log2(SSD_HEAD_DIM)))
    expand = jnp.where(jj == direction * SSD_HEADS + head_of_col, 1.0, 0.0).astype(BF16)
    dt_cols = _softplus(dt_ref[...] + bias_p_ref[...])
    a_cols = dt_cols * (-jnp.exp(alog_p_ref[...]))
    dt_rows = _softplus(dtt_ref[direction * SSD_HEADS:(direction + 1) * SSD_HEADS, :] + bias_c_ref[...])
    a_rows = dt_rows * (-jnp.exp(alog_c_ref[...]))

    ri = lax.broadcasted_iota(jnp.int32, (ll, ll), 0)
    ci = lax.broadcasted_iota(jnp.int32, (ll, ll), 1)
    if rev:
        keep = ci >= ri
        edge = 0
    else:
        keep = ci <= ri
        edge = ll - 1
    tri = jnp.where(keep, 1.0, 0.0).astype(BF16)
    tri_t = jnp.where((ri >= ci) if rev else (ri <= ci), 1.0, 0.0).astype(BF16)
    cum_cols = _dot_01_by_f32(tri, a_cols)
    dt_full = _dot_f32_by_01(dt_cols, expand)
    cum_full = _dot_f32_by_01(cum_cols, expand)
    cum_rows = _dot_f32_by_01(a_rows, tri_t)

    xd = xs * dt_full
    xd_b = xd.astype(BF16)
    cum_edge = cum_full[edge:edge + 1, :]
    xdw = (xd * jnp.exp(cum_edge - cum_full)).astype(BF16)
    grow = jnp.exp(cum_full)
    lane = lax.broadcasted_iota(jnp.int32, (ll, LANES), 1)

    y_parts = []
    for g in range(SSD_GROUPS):
        bg = bm[:, g * D_STATE:(g + 1) * D_STATE]
        cg = cm[:, g * D_STATE:(g + 1) * D_STATE].astype(BF16)
        cb = _dot_nt(cg, bg.astype(BF16))
        gs = slice(g * GROUP_COLS, (g + 1) * GROUP_COLS)
        st_in = st_ref[:, gs]
        y_off = _dot(cg, st_in.astype(BF16)) * grow[:, gs]
        heads_per_group = SSD_HEADS // SSD_GROUPS
        for pair in range(heads_per_group // 2):
            lo = g * GROUP_COLS + pair * LANES
            xd_pair = xd_b[:, lo:lo + LANES]
            outs = []
            for sub in range(2):
                hd = g * heads_per_group + pair * 2 + sub
                seg = cum_full[:, hd * SSD_HEAD_DIM:hd * SSD_HEAD_DIM + 1] - cum_rows[hd:hd + 1, :]
                dec = jnp.exp(jnp.where(keep, seg, -jnp.inf))
                outs.append(_dot((cb * dec).astype(BF16), xd_pair))
            y_diag = jnp.where(lane < SSD_HEAD_DIM, outs[0], outs[1])
            y_parts.append(y_diag + y_off[:, pair * LANES:(pair + 1) * LANES])
        st_chunk = _dot(bg.T.astype(BF16), xdw[:, gs])
        st_ref[:, gs] = st_in * jnp.exp(cum_edge[:, gs]) + st_chunk
    y = jnp.concatenate(y_parts, axis=1)

    if final:
        y = yf_ref[...] + y + xs * dskip_ref[...]
        y = y * _silu(z_ref[...])
        nw = nw_ref[...]
        normed = []
        for g in range(SSD_GROUPS):
            gs = slice(g * GROUP_COLS, (g + 1) * GROUP_COLS)
            normed.append(_rms(y[:, gs], nw[:, gs]))
        y = jnp.concatenate(normed, axis=1)
    y_ref[...] = y.astype(y_ref.dtype)


def ssd_pass(xbc, dt, dtt, cw, cb, bias_p, bias_c, alog_p, alog_c, extras, b, s, rev):
    final = extras is not None
    t = b * s
    nc = s // CHUNK
    rows8 = CHUNK // 8
    last8 = t // 8 - 1

    def cidx(bi, c):
        return bi * nc + ((nc - 1 - c) if rev else c)

    def full(shape):
        return pl.BlockSpec(shape, lambda bi, c: (0, 0))

    chunk_rows = lambda w: pl.BlockSpec((CHUNK, w), lambda bi, c: (cidx(bi, c), 0))
    dt_specs = [
        chunk_rows(DT_PAD),
        pl.BlockSpec((2 * SSD_HEADS, CHUNK), lambda bi, c: (0, cidx(bi, c))),
        full((1, DT_PAD)), full((SSD_HEADS, 1)), full((1, DT_PAD)), full((SSD_HEADS, 1)),
    ]
    dt_args = [dt, dtt, bias_p, bias_c, alog_p, alog_c]
    if final:
        z, yf, dskip_f, nw = extras
        in_specs = [chunk_rows(CONV_DIM)] + dt_specs + [
            chunk_rows(D_INNER), chunk_rows(D_INNER), full((1, D_INNER)), full((1, D_INNER))]
        args = [xbc] + dt_args + [z, yf, dskip_f, nw]
        out_specs = chunk_rows(D_INNER)
        out_shape = jax.ShapeDtypeStruct((t, D_INNER), BF16)
    else:
        in_specs = [
            chunk_rows(CONV_DIM),
            pl.BlockSpec((8, CONV_DIM), lambda bi, c: (jnp.maximum(cidx(bi, c) * rows8 - 1, 0), 0)),
            pl.BlockSpec((8, CONV_DIM), lambda bi, c: (jnp.minimum((cidx(bi, c) + 1) * rows8, last8), 0)),
            full((3, CONV_DIM)), full((1, CONV_DIM)),
        ] + dt_specs
        args = [xbc, xbc, xbc, cw, cb] + dt_args
        out_specs = [chunk_rows(D_INNER), chunk_rows(CONV_DIM)]
        out_shape = [jax.ShapeDtypeStruct((t, D_INNER), F32), jax.ShapeDtypeStruct((t, CONV_DIM), F32)]
    return pl.pallas_call(
        functools.partial(_ssd_kernel, rev=rev, final=final, nc=nc),
        grid=(b, nc),
        in_specs=in_specs,
        out_specs=out_specs,
        out_shape=out_shape,
        scratch_shapes=[pltpu.VMEM((D_STATE, D_INNER), F32)],
        compiler_params=_params(("parallel", "arbitrary")),
        name="ssd_bwd_final" if final else "ssd_fwd",
    )(*args)


def _merge_kernel(attn_ref, ssd_ref, gates_ref, x_ref, woa_ref, wos_ref, wout_ref, nw_ref, o_ref):
    a = _dot(attn_ref[...], woa_ref[...])
    s = _dot(ssd_ref[...], wos_ref[...])
    gates = gates_ref[...]
    merged = jax.nn.sigmoid(gates[:, :D_MODEL]) * a + jax.nn.sigmoid(gates[:, D_MODEL:]) * s
    mo = _dot(merged.astype(BF16), wout_ref[...])
    o_ref[...] = x_ref[...] + _rms(mo, nw_ref[...])


def merge_out(attn, ssd, gates, x, woa, wos, wout, nw, tm):
    t, d = x.shape
    rows = lambda w: pl.BlockSpec((tm, w), lambda i: (i, 0))
    full = lambda shape: pl.BlockSpec(shape, lambda i: (0, 0))
    return pl.pallas_call(
        _merge_kernel,
        grid=(t // tm,),
        in_specs=[rows(d), rows(d), rows(2 * d), rows(d), full((d, d)), full((d, d)), full((d, d)),
                  full((1, d))],
        out_specs=rows(d),
        out_shape=jax.ShapeDtypeStruct((t, d), F32),
        compiler_params=_params(("parallel",)),
        name="merge_out",
    )(attn, ssd, gates, x, woa, wos, wout, nw)


def _ffn_kernel(x_ref, xp_ref, xn_ref, gpre_ref, wa_ref, wg_ref, cwa_ref, cwg_ref, cba_ref, cbg_ref,
                wd_ref, gpost_ref, o_ref, h_ref, acc_ref, *, tm, tiles_per_seq, nf):
    i = pl.program_id(0)
    f = pl.program_id(1)

    @pl.when(f == 0)
    def _():
        gpre = gpre_ref[...]
        pos = i % tiles_per_seq
        h_ref[0:tm, :] = _rms(x_ref[...], gpre).astype(BF16)
        hp = jnp.where(pos == 0, 0.0, _rms(xp_ref[...], gpre))
        hn = jnp.where(pos == tiles_per_seq - 1, 0.0, _rms(xn_ref[...], gpre))
        h_ref[tm:tm + HALO, :] = hn.astype(BF16)
        h_ref[tm + HALO:tm + 2 * HALO, :] = hp.astype(BF16)
        acc_ref[...] = jnp.zeros_like(acc_ref)

    h = h_ref[...]
    n_ext = tm + 2 * HALO

    def conv_branch(w_ref, cw_ref, cb_ref):
        u = _dot(h, w_ref[...])
        um = u[0:tm, :]
        up = pltpu.roll(u, 1, axis=0)[0:tm, :]
        un = pltpu.roll(u, n_ext - 1, axis=0)[0:tm, :]
        cw = cw_ref[...]
        return cb_ref[...] + up * cw[0:1, :] + um * cw[1:2, :] + un * cw[2:3, :]

    a = conv_branch(wa_ref, cwa_ref, cba_ref)
    g = conv_branch(wg_ref, cwg_ref, cbg_ref)
    act = (_silu(g) * a).astype(BF16)
    acc_ref[...] += _dot(act, wd_ref[...])

    @pl.when(f == nf - 1)
    def _():
        o_ref[...] = x_ref[...] + _rms(acc_ref[...], gpost_ref[...])


def ffn(x, gpre, wa, wg, cwa, cwg, cba, cbg, wd, gpost, s, tm, tf):
    t, d = x.shape
    nf = D_FF_PAD // tf
    tiles_per_seq = s // tm
    blocks = tm // HALO
    last = t // HALO - 1
    full = lambda shape: pl.BlockSpec(shape, lambda i, f: (0, 0))
    colblk = lambda r: pl.BlockSpec((r, tf), lambda i, f: (0, f))
    return pl.pallas_call(
        functools.partial(_ffn_kernel, tm=tm, tiles_per_seq=tiles_per_seq, nf=nf),
        grid=(t // tm, nf),
        in_specs=[
            pl.BlockSpec((tm, d), lambda i, f: (i, 0)),
            pl.BlockSpec((HALO, d), lambda i, f: (jnp.maximum(i * blocks - 1, 0), 0)),
            pl.BlockSpec((HALO, d), lambda i, f: (jnp.minimum((i + 1) * blocks, last), 0)),
            full((1, d)),
            colblk(d), colblk(d), colblk(3), colblk(3), colblk(1), colblk(1),
            pl.BlockSpec((tf, d), lambda i, f: (f, 0)),
            full((1, d)),
        ],
        out_specs=pl.BlockSpec((tm, d), lambda i, f: (i, 0)),
        out_shape=jax.ShapeDtypeStruct((t, d), F32),
        scratch_shapes=[pltpu.VMEM((tm + 2 * HALO, d), BF16), pltpu.VMEM((tm, d), F32)],
        compiler_params=_params(("parallel", "arbitrary")),
        name="ffn",
    )(x, x, x, gpre, wa, wg, cwa, cwg, cba, cbg, wd, gpost)


def _tiles(s):
    return dict(
        tm_proj=min(256, s),
        tq=min(256, s), tk=min(512, s), depth=3,
        tm_merge=min(256, s),
        tm_ffn=min(1024, s), tf=256,
    )


def _layer(x, b, s, w, cfg):
    g_pre = w["norm_mix_pre"]
    qk, vt, z, xbc, gates, dt, dtt = in_proj(x, g_pre, w, cfg["tm_proj"])

    attn = diff_attention(qk, vt, w["slopes"], w["lam_consts"], w["lam_q1"], w["lam_k1"], w["lam_q2"],
                          w["lam_k2"], w["subln_col"], b, s, cfg["tq"], cfg["tk"], cfg["depth"])

    yf, u = ssd_pass(xbc, dt, dtt, w["conv_ssd_w"], w["conv_ssd_b"], w["bias_p"], w["bias_c"][0],
                     w["alog_p"], w["alog_c"][0], None, b, s, rev=False)
    ssd = ssd_pass(u, dt, dtt, w["conv_ssd_w"], w["conv_ssd_b"], w["bias_p"], w["bias_c"][1],
                   w["alog_p"], w["alog_c"][1], (z, yf, w["dskip_f"], w["ssd_norm"]), b, s, rev=True)

    x = merge_out(attn, ssd, gates, x, w["w_o_attn"], w["w_o_ssd"], w["w_out"], w["norm_mix_post"],
                  cfg["tm_merge"])
    x = ffn(x, w["norm_ffn_pre"], w["w_up_a"], w["w_up_g"], w["cw_a"], w["cw_g"], w["cb_a"], w["cb_g"],
            w["w_down"], w["norm_ffn_post"], s, cfg["tm_ffn"], cfg["tf"])
    return x


def _prepare_weights(norm_mix_pre, norm_mix_post, norm_ffn_pre, norm_ffn_post, w_in, lam_q1, lam_k1,
                     lam_q2, lam_k2, attn_subln, conv_ssd_w, conv_ssd_b, dt_bias, a_log, d_skip, ssd_norm,
                     w_o_attn, w_o_ssd, w_out, w_up, conv_ffn_w, conv_ffn_b, w_down):
    depth = w_in.shape[0]
    qk_cols = ATTN_HEADS * 2 * HEAD_DIM
    attn_w = ATTN_HEADS * V_DIM
    cuts = [0, 2 * qk_cols]
    for width in (attn_w, D_INNER, CONV_DIM, 2 * SSD_HEADS, 2 * D_MODEL):
        cuts.append(cuts[-1] + width)
    seg = lambda i: w_in[:, :, cuts[i]:cuts[i + 1]]
    w_dt = seg(4)
    row = lambda a: a[:, None, :]
    rep = lambda a: jnp.repeat(a, SSD_HEAD_DIM, axis=-1)
    pad_ff = lambda a: jnp.pad(a, [(0, 0)] * (a.ndim - 1) + [(0, D_FF_PAD - D_FF)])
    pad_dt = lambda a: jnp.pad(a.reshape(depth, 1, 2 * SSD_HEADS), ((0, 0), (0, 0), (0, DT_PAD - 2 * SSD_HEADS)))
    lam_init = [0.8 - 0.6 * math.exp(-0.3 * l) for l in range(depth)]
    return dict(
        norm_mix_pre=row(norm_mix_pre), norm_mix_post=row(norm_mix_post),
        norm_ffn_pre=row(norm_ffn_pre), norm_ffn_post=row(norm_ffn_post),
        w_qk=seg(0).astype(BF16),
        w_vt=jnp.swapaxes(seg(1), 1, 2).astype(BF16),
        w_z=seg(2).astype(BF16),
        w_xbc=seg(3).astype(BF16),
        w_dt=jnp.pad(w_dt, ((0, 0), (0, 0), (0, DT_PAD - 2 * SSD_HEADS))).astype(BF16),
        w_dtt=jnp.swapaxes(w_dt, 1, 2).astype(BF16),
        w_gates=seg(5).astype(BF16),
        slopes=jnp.tile(jnp.asarray([2.0 ** (-8.0 * (i + 1) / ATTN_HEADS) for i in range(ATTN_HEADS)],
                                    F32)[None], (depth, 1)),
        lam_consts=jnp.asarray([[li, 1.0 - li] for li in lam_init], F32),
        lam_q1=row(lam_q1), lam_k1=row(lam_k1), lam_q2=row(lam_q2), lam_k2=row(lam_k2),
        subln_col=attn_subln[:, :, None],
        conv_ssd_w=conv_ssd_w, conv_ssd_b=row(conv_ssd_b),
        bias_p=pad_dt(dt_bias), bias_c=dt_bias[..., None],
        alog_p=pad_dt(a_log), alog_c=a_log[..., None],
        dskip_f=row(rep(d_skip)), ssd_norm=row(ssd_norm),
        w_o_attn=w_o_attn.astype(BF16), w_o_ssd=w_o_ssd.astype(BF16), w_out=w_out.astype(BF16),
        w_up_a=pad_ff(w_up[:, :, :D_FF]).astype(BF16), w_up_g=pad_ff(w_up[:, :, D_FF:]).astype(BF16),
        cw_a=pad_ff(conv_ffn_w[:, :, :D_FF]), cw_g=pad_ff(conv_ffn_w[:, :, D_FF:]),
        cb_a=row(pad_ff(conv_ffn_b[:, :D_FF])), cb_g=row(pad_ff(conv_ffn_b[:, D_FF:])),
        w_down=jnp.pad(w_down, ((0, 0), (0, D_FF_PAD - D_FF), (0, 0))).astype(BF16),
    )


def kernel(x_prompt, x_sample, norm_mix_pre, norm_mix_post, norm_ffn_pre, norm_ffn_post, w_in, lam_q1, lam_k1, lam_q2, lam_k2, attn_subln, conv_ssd_w, conv_ssd_b, dt_bias, a_log, d_skip, ssd_norm, w_o_attn, w_o_ssd, w_out, w_up, conv_ffn_w, conv_ffn_b, w_down):
    weights = _prepare_weights(norm_mix_pre, norm_mix_post, norm_ffn_pre, norm_ffn_post, w_in, lam_q1,
                               lam_k1, lam_q2, lam_k2, attn_subln, conv_ssd_w, conv_ssd_b, dt_bias, a_log,
                               d_skip, ssd_norm, w_o_attn, w_o_ssd, w_out, w_up, conv_ffn_w, conv_ffn_b,
                               w_down)
    groups = []
    for xg in (x_prompt, x_sample):
        b, s, d = xg.shape
        groups.append((b, s, _tiles(s)))

    def step(carry, w):
        out = tuple(_layer(x, b, s, w, cfg) for x, (b, s, cfg) in zip(carry, groups))
        return out, None

    init = tuple(xg.reshape(-1, xg.shape[-1]) for xg in (x_prompt, x_sample))
    out, _ = lax.scan(step, init, weights)
    return tuple(o.reshape(xg.shape) for o, xg in zip(out, (x_prompt, x_sample)))
```

```python
import functools
import math

import jax
import jax.numpy as jnp
from jax import lax
from jax.experimental import pallas as pl
from jax.experimental.pallas import tpu as pltpu

F32 = jnp.float32
BF16 = jnp.bfloat16

D_MODEL = 1024
ATTN_HEADS = 8
HEAD_DIM = 64
V_DIM = 2 * HEAD_DIM
SSD_HEADS = 16
SSD_HEAD_DIM = 64
D_INNER = SSD_HEADS * SSD_HEAD_DIM
SSD_GROUPS = 2
GROUP_COLS = D_INNER // SSD_GROUPS
D_STATE = 128
CHUNK = 128
CONV_DIM = D_INNER + 2 * SSD_GROUPS * D_STATE
D_FF = 2752
EPS = 1e-6

LANES = 128
D_FF_PAD = 2816
DT_PAD = LANES
HALO = 16
PROJ_CHUNK = 512

UNDERFLOW = 136.0
BF16_EXACT_INT = 256
BOUND_REL_SLACK = 1.001
BOUND_ABS_SLACK = 1.0
REACH_CAP = 1e9
SUM_ROWS = 16
LOG2E = math.log2(math.e)
Q_PRESCALE = LOG2E / math.sqrt(HEAD_DIM)
VMEM_LIMIT = 56 * 1024 * 1024


def _params(sem):
    return pltpu.CompilerParams(dimension_semantics=sem, vmem_limit_bytes=VMEM_LIMIT)


def _rms(x, g):
    ms = jnp.mean(x * x, axis=-1, keepdims=True)
    return x * lax.rsqrt(ms + EPS) * g


def _dot(a, b):
    return jnp.dot(a, b, preferred_element_type=F32)


def _dot_nt(a, b):
    return lax.dot_general(a, b, (((1,), (1,)), ((), ())), preferred_element_type=F32)


def _bf16_terms(x):
    def top(v):
        bits = lax.bitcast_convert_type(v, jnp.uint32) & jnp.uint32(0xFFFF0000)
        return lax.bitcast_convert_type(bits, F32)

    hi = top(x)
    mid = top(x - hi)
    lo = x - hi - mid
    return hi, mid, lo


def _dot_f32_by_01(x, m01):
    return sum(_dot(term.astype(BF16), m01) for term in _bf16_terms(x))


def _dot_01_by_f32(m01, x):
    return sum(_dot(m01, term.astype(BF16)) for term in _bf16_terms(x))


def _softplus(x):
    return jnp.maximum(x, 0.0) + jnp.log1p(jnp.exp(-jnp.abs(x)))


def _silu(x):
    return x * jax.nn.sigmoid(x)


def _in_proj_kernel(x_ref, g_ref, wqk_ref, wvt_ref, wz_ref, wxbc_ref, wg_ref, wdt_ref, wdtt_ref,
                    qk_ref, vt_ref, z_ref, xbc_ref, gates_ref, dt_ref, dtt_ref):
    h = _rms(x_ref[...], g_ref[...]).astype(BF16)
    qk_cols = ATTN_HEADS * 2 * HEAD_DIM

    def project(w_ref, o_ref, scale_upto=0):
        n = w_ref.shape[1]
        for c0 in range(0, n, PROJ_CHUNK):
            c1 = min(c0 + PROJ_CHUNK, n)
            y = _dot(h, w_ref[:, c0:c1])
            if c1 <= scale_upto:
                y = y * Q_PRESCALE
            o_ref[:, c0:c1] = y.astype(o_ref.dtype)

    project(wqk_ref, qk_ref, scale_upto=qk_cols)
    for r0 in range(0, wvt_ref.shape[0], PROJ_CHUNK):
        vt_ref[r0:r0 + PROJ_CHUNK, :] = _dot_nt(wvt_ref[r0:r0 + PROJ_CHUNK, :], h).astype(vt_ref.dtype)
    project(wz_ref, z_ref)
    project(wxbc_ref, xbc_ref)
    project(wg_ref, gates_ref)
    project(wdt_ref, dt_ref)
    dtt_ref[...] = _dot_nt(wdtt_ref[...], h)


def in_proj(x, g, w, tm):
    t, d = x.shape
    weights = [w["w_qk"], w["w_vt"], w["w_z"], w["w_xbc"], w["w_gates"], w["w_dt"], w["w_dtt"]]
    rows = lambda n: pl.BlockSpec((tm, n), lambda i: (i, 0))
    cols = lambda n: pl.BlockSpec((n, tm), lambda i: (0, i))
    resident = lambda a: pl.BlockSpec(a.shape, lambda i: (0, 0), pipeline_mode=pl.Buffered(1))
    n_qk, n_v, n_z, n_xbc, n_g = (w["w_qk"].shape[1], w["w_vt"].shape[0], w["w_z"].shape[1],
                                  w["w_xbc"].shape[1], w["w_gates"].shape[1])
    n_dtt = w["w_dtt"].shape[0]
    return pl.pallas_call(
        _in_proj_kernel,
        grid=(t // tm,),
        in_specs=[rows(d), resident(g)] + [resident(a) for a in weights],
        out_specs=[rows(n_qk), cols(n_v), rows(n_z), rows(n_xbc), rows(n_g), rows(DT_PAD), cols(n_dtt)],
        out_shape=[
            jax.ShapeDtypeStruct((t, n_qk), BF16), jax.ShapeDtypeStruct((n_v, t), BF16),
            jax.ShapeDtypeStruct((t, n_z), F32), jax.ShapeDtypeStruct((t, n_xbc), F32),
            jax.ShapeDtypeStruct((t, n_g), F32), jax.ShapeDtypeStruct((t, DT_PAD), F32),
            jax.ShapeDtypeStruct((n_dtt, t), F32),
        ],
        compiler_params=_params(("parallel",)),
        name="in_proj",
    )(x, g, *weights)


def _attn_kernel(slopes_ref, lam_ref, q_ref, k_ref, vt_ref, lq1_ref, lk1_ref, lq2_ref, lk2_ref,
                 subln_ref, o_ref, kfeat_ref, qfeat_ref, knorm_ref, *s_refs, tq, tk, nk):
    h = pl.program_id(1)
    qi = pl.program_id(2)
    slope2 = slopes_ref[h] * LOG2E
    q0pos = qi * tq
    n_off = nk - 1
    kd = lax.shift_right_logical(qi, int(math.log2(tk // tq)))

    @pl.when(qi == 0)
    def _():
        ii = lax.broadcasted_iota(jnp.int32, (tq, LANES), 0).astype(F32)
        fq = lax.broadcasted_iota(jnp.int32, (tq, LANES), 1)
        row_terms = _bf16_terms(-slope2 * ii)
        slope_terms = _bf16_terms(jnp.full((tq, LANES), slope2, F32))
        q_feat = jnp.zeros((tq, LANES), F32)
        for n in range(3):
            q_feat = jnp.where(fq == n, row_terms[n], q_feat)
            q_feat = jnp.where((fq == 3 + n) | (fq == 6 + n), slope_terms[n], q_feat)
        qfeat_ref[...] = q_feat.astype(BF16)
        jj = lax.broadcasted_iota(jnp.int32, (tk, LANES), 0)
        fk = lax.broadcasted_iota(jnp.int32, (tk, LANES), 1)
        jj_lo = jnp.bitwise_and(jj, BF16_EXACT_INT - 1)
        jj_hi = (jj - jj_lo).astype(F32)
        jj_lo = jj_lo.astype(F32)
        k_feat = jnp.where(fk < 3, 1.0, jnp.where(fk < 6, jj_lo, jnp.where(fk < 9, jj_hi, 0.0)))
        kfeat_ref[0] = k_feat.astype(BF16)
        kfeat_ref[1] = (-k_feat).astype(BF16)

    q = q_ref[...]
    lane = lax.broadcasted_iota(jnp.int32, q.shape, 1)
    zero = jnp.zeros_like(q)
    q_feat = qfeat_ref[...]
    q_ops = jnp.concatenate([
        jnp.concatenate([jnp.where(lane < HEAD_DIM, q, zero), q_feat], axis=1),
        jnp.concatenate([jnp.where(lane >= HEAD_DIM, q, zero), q_feat], axis=1)], axis=0)
    ones_rows = jnp.ones((SUM_ROWS, tk), BF16)

    def raw_scores(kb, side):
        start = pl.multiple_of(kb * tk, tk)
        k_ops = jnp.concatenate([k_ref[pl.ds(start, tk), :], kfeat_ref[side]], axis=1)
        return _dot_nt(k_ops, q_ops)

    def offset(kb):
        return slope2 * jnp.abs(q0pos - kb * tk).astype(F32)

    def off_tile(t):
        side = (t >= kd).astype(jnp.int32)
        return t + side, side

    def produce(t, dst_ref):
        kb, side = off_tile(t)
        s = raw_scores(kb, side)
        dst_ref[...] = s
        return jnp.max(s, axis=0, keepdims=True)

    def update(s, mx, c, kb, state):
        m_old, acc_old = state
        m_new = jnp.maximum(m_old, mx - c)
        p = jnp.exp2(s - (m_new + c))
        alpha = jnp.exp2(m_old - m_new)
        vt = vt_ref[:, pl.ds(pl.multiple_of(kb * tk, tk), tk)]
        vt_ops = jnp.concatenate([vt, ones_rows], axis=0)
        return m_new, alpha * acc_old + _dot(vt_ops, p.astype(BF16))

    def consume(t, src_ref, mx, state):
        kb, _ = off_tile(t)
        return update(src_ref[...], mx, offset(kb), kb, state)

    DEPTH = len(s_refs)
    max_trips = (n_off - 1) // DEPTH

    @pl.when(qi == 0)
    def _():
        def tile_norm(j, best):
            kt = k_ref[pl.ds(pl.multiple_of(j * tk, tk), tk), :].astype(F32)
            return jnp.maximum(best, jnp.max(jnp.sum(kt * kt, axis=1, keepdims=True), axis=0, keepdims=True))
        knorm_ref[...] = jnp.broadcast_to(lax.fori_loop(0, nk, tile_norm, jnp.zeros((1, 1), F32)),
                                          knorm_ref.shape)

    c_d = offset(kd)
    s_d = jnp.minimum(raw_scores(kd, 0) - c_d, raw_scores(kd, 1) + c_d)
    state = (jnp.full((1, 2 * tq), -jnp.inf, F32), jnp.zeros((V_DIM + SUM_ROWS, 2 * tq), F32))
    state = update(s_d, jnp.max(s_d, axis=0, keepdims=True), 0.0, kd, state)

    qf = q.astype(F32)
    qn2 = jnp.max(jnp.sum(qf * qf, axis=1, keepdims=True), axis=0, keepdims=True)
    bound = jnp.sqrt(qn2 * knorm_ref[0:1, 0:1]) * BOUND_REL_SLACK + BOUND_ABS_SLACK
    m_min = jnp.min(state[0], axis=1, keepdims=True)
    reach = jnp.minimum((bound + UNDERFLOW - m_min) / slope2, REACH_CAP)
    reach = (jnp.ceil(reach).astype(jnp.int32) + 1)[0, 0]
    tk_shift = int(math.log2(tk))
    kb_lo = jnp.minimum(lax.shift_right_logical(jnp.maximum(q0pos + 1 - reach, 0), tk_shift), kd)
    kb_hi = jnp.maximum(jnp.minimum(lax.shift_right_logical(reach + q0pos + tq - 2, tk_shift), nk - 1), kd)
    wanted = jnp.maximum(kb_hi - kb_lo - 1, 0)
    if DEPTH == 2:
        whole = lax.shift_right_logical(wanted + 1, 1)
    else:
        whole = lax.shift_right_logical((wanted + 2) * 43, 7)
    trips = jnp.clip(whole, 1, max_trips)
    first = jnp.minimum(kb_lo, n_off - 1 - DEPTH * trips)

    mx = tuple(produce(first + i, s_refs[i]) for i in range(DEPTH))

    def trip(r, carry, n_produce):
        state, mx = carry[:2], list(carry[2])
        for i in range(DEPTH):
            t = first + DEPTH * r + i
            state = consume(t, s_refs[i], mx[i], state)
            if i < n_produce:
                mx[i] = produce(t + DEPTH, s_refs[i])
        return state + (tuple(mx),)

    full_trips = trips - 1
    triples = lax.shift_right_logical(full_trips * 43, 7)
    carry = lax.fori_loop(
        0, triples,
        lambda r3, c: trip(3 * r3 + 2, trip(3 * r3 + 1, trip(3 * r3, c, DEPTH), DEPTH), DEPTH),
        state + (mx,))
    carry = lax.fori_loop(3 * triples, full_trips, functools.partial(trip, n_produce=DEPTH), carry)
    carry = trip(trips - 1, carry, n_produce=1)
    _, acc_fin = consume(first + DEPTH * trips, s_refs[0], carry[2][0], carry[:2])

    lam_init = lam_ref[0]
    one_minus = lam_ref[1]
    lam = (jnp.exp(jnp.sum(lq1_ref[...] * lk1_ref[...], axis=-1, keepdims=True))
           - jnp.exp(jnp.sum(lq2_ref[...] * lk2_ref[...], axis=-1, keepdims=True)) + lam_init)
    o_both = acc_fin[:V_DIM, :] / acc_fin[V_DIM:V_DIM + 1, :]
    o = o_both[:, :tq] - lam * o_both[:, tq:]
    ms = jnp.mean(o * o, axis=0, keepdims=True)
    y = o * lax.rsqrt(ms + EPS) * subln_ref[...] * one_minus
    o_ref[...] = y.T.astype(o_ref.dtype)


def diff_attention(qk, vt, slopes, lam_consts, lq1, lk1, lq2, lk2, subln_col, b, s, tq, tk, depth):
    t = b * s
    nq = s // tq
    nk = s // tk
    assert depth in (2, 3) and depth + 2 <= nk < 126 and (nk - 2) % depth == 0, (s, tk)
    assert tk % tq == 0 and tk <= BF16_EXACT_INT ** 2
    hh = ATTN_HEADS
    smem = pl.BlockSpec(memory_space=pltpu.SMEM)
    vec = pl.BlockSpec((1, HEAD_DIM), lambda bi, h, qi: (0, 0))
    return pl.pallas_call(
        functools.partial(_attn_kernel, tq=tq, tk=tk, nk=nk),
        grid=(b, hh, nq),
        in_specs=[
            smem, smem,
            pl.BlockSpec((tq, V_DIM), lambda bi, h, qi: (bi * nq + qi, h)),
            pl.BlockSpec((s, V_DIM), lambda bi, h, qi: (bi, hh + h)),
            pl.BlockSpec((V_DIM, s), lambda bi, h, qi: (h, bi)),
            vec, vec, vec, vec,
            pl.BlockSpec((V_DIM, 1), lambda bi, h, qi: (0, 0)),
        ],
        out_specs=pl.BlockSpec((tq, V_DIM), lambda bi, h, qi: (bi * nq + qi, h)),
        out_shape=jax.ShapeDtypeStruct((t, hh * V_DIM), BF16),
        scratch_shapes=([pltpu.VMEM((2, tk, LANES), BF16), pltpu.VMEM((tq, LANES), BF16),
                         pltpu.VMEM((8, LANES), F32)]
                        + [pltpu.VMEM((tk, 2 * tq), F32)] * depth),
        compiler_params=_params(("parallel", "parallel", "arbitrary")),
        name="diff_attention",
    )(slopes, lam_consts, qk, qk, vt, lq1, lk1, lq2, lk2, subln_col)


def _ssd_kernel(*refs, rev, final, nc):
    if final:
        (u_ref, dt_ref, dtt_ref, bias_p_ref, bias_c_ref, alog_p_ref, alog_c_ref,
         z_ref, yf_ref, dskip_ref, nw_ref, y_ref, st_ref) = refs
    else:
        (xbc_ref, prev_ref, next_ref, cw_ref, cb_ref, dt_ref, dtt_ref, bias_p_ref, bias_c_ref,
         alog_p_ref, alog_c_ref, y_ref, u_ref, st_ref) = refs
    direction = 1 if rev else 0
    c = pl.program_id(1)
    cc = (nc - 1 - c) if rev else c
    ll = CHUNK

    @pl.when(c == 0)
    def _():
        st_ref[...] = jnp.zeros_like(st_ref)

    if final:
        u = u_ref[...]
    else:
        x = xbc_ref[...]
        before = jnp.where(cc == 0, 0.0, prev_ref[...])
        after = jnp.where(cc == nc - 1, 0.0, next_ref[...])
        x_ext = jnp.concatenate([x, after, before], axis=0)
        n_ext = x_ext.shape[0]
        xp = pltpu.roll(x_ext, 1, axis=0)[0:ll, :]
        xn = pltpu.roll(x_ext, n_ext - 1, axis=0)[0:ll, :]
        cw = cw_ref[...]
        u = _silu(cb_ref[...] + xp * cw[0:1, :] + x * cw[1:2, :] + xn * cw[2:3, :])
        u_ref[...] = u
    xs = u[:, :D_INNER]
    bm = u[:, D_INNER:D_INNER + SSD_GROUPS * D_STATE]
    cm = u[:, D_INNER + SSD_GROUPS * D_STATE:]

    jj = lax.broadcasted_iota(jnp.int32, (DT_PAD, D_INNER), 0)
    col = lax.broadcasted_iota(jnp.int32, (DT_PAD, D_INNER), 1)
    head_of_col = lax.shift_right_logical(col, int(math.log2(SSD_HEAD_DIM)))
    expand = jnp.where(jj == direction * SSD_HEADS + head_of_col, 1.0, 0.0).astype(BF16)
    dt_cols = _softplus(dt_ref[...] + bias_p_ref[...])
    a_cols = dt_cols * (-jnp.exp(alog_p_ref[...]))
    dt_rows = _softplus(dtt_ref[direction * SSD_HEADS:(direction + 1) * SSD_HEADS, :] + bias_c_ref[...])
    a_rows = dt_rows * (-jnp.exp(alog_c_ref[...]))

    ri = lax.broadcasted_iota(jnp.int32, (ll, ll), 0)
    ci = lax.broadcasted_iota(jnp.int32, (ll, ll), 1)
    if rev:
        keep = ci >= ri
        edge = 0
    else:
        keep = ci <= ri
        edge = ll - 1
    tri = jnp.where(keep, 1.0, 0.0).astype(BF16)
    tri_t = jnp.where((ri >= ci) if rev else (ri <= ci), 1.0, 0.0).astype(BF16)
    cum_cols = _dot_01_by_f32(tri, a_cols)
    dt_full = _dot_f32_by_01(dt_cols, expand)
    cum_full = _dot_f32_by_01(cum_cols, expand)
    cum_rows = _dot_f32_by_01(a_rows, tri_t)

    xd = xs * dt_full
    xd_b = xd.astype(BF16)
    cum_edge = cum_full[edge:edge + 1, :]
    xdw = (xd * jnp.exp(cum_edge - cum_full)).astype(BF16)
    grow = jnp.exp(cum_full)
    lane = lax.broadcasted_iota(jnp.int32, (ll, LANES), 1)

    y_parts = []
    for g in range(SSD_GROUPS):
        bg = bm[:, g * D_STATE:(g + 1) * D_STATE]
        cg = cm[:, g * D_STATE:(g + 1) * D_STATE].astype(BF16)
        cb = _dot_nt(cg, bg.astype(BF16))
        gs = slice(g * GROUP_COLS, (g + 1) * GROUP_COLS)
        st_in = st_ref[:, gs]
        y_off = _dot(cg, st_in.astype(BF16)) * grow[:, gs]
        heads_per_group = SSD_HEADS // SSD_GROUPS
        for pair in range(heads_per_group // 2):
            lo = g * GROUP_COLS + pair * LANES
            xd_pair = xd_b[:, lo:lo + LANES]
            outs = []
            for sub in range(2):
                hd = g * heads_per_group + pair * 2 + sub
                seg = cum_full[:, hd * SSD_HEAD_DIM:hd * SSD_HEAD_DIM + 1] - cum_rows[hd:hd + 1, :]
                dec = jnp.exp(jnp.where(keep, seg, -jnp.inf))
                outs.append(_dot((cb * dec).astype(BF16), xd_pair))
            y_diag = jnp.where(lane < SSD_HEAD_DIM, outs[0], outs[1])
            y_parts.append(y_diag + y_off[:, pair * LANES:(pair + 1) * LANES])
        st_chunk = _dot(bg.T.astype(BF16), xdw[:, gs])
        st_ref[:, gs] = st_in * jnp.exp(cum_edge[:, gs]) + st_chunk
    y = jnp.concatenate(y_parts, axis=1)

    if final:
        y = yf_ref[...] + y + xs * dskip_ref[...]
        y = y * _silu(z_ref[...])
        nw = nw_ref[...]
        normed = []
        for g in range(SSD_GROUPS):
            gs = slice(g * GROUP_COLS, (g + 1) * GROUP_COLS)
            normed.append(_rms(y[:, gs], nw[:, gs]))
        y = jnp.concatenate(normed, axis=1)
    y_ref[...] = y.astype(y_ref.dtype)


def ssd_pass(xbc, dt, dtt, cw, cb, bias_p, bias_c, alog_p, alog_c, extras, b, s, rev):
    final = extras is not None
    t = b * s
    nc = s // CHUNK
    rows8 = CHUNK // 8
    last8 = t // 8 - 1

    def cidx(bi, c):
        return bi * nc + ((nc - 1 - c) if rev else c)

    def full(shape):
        return pl.BlockSpec(shape, lambda bi, c: (0, 0))

    chunk_rows = lambda w: pl.BlockSpec((CHUNK, w), lambda bi, c: (cidx(bi, c), 0))
    dt_specs = [
        chunk_rows(DT_PAD),
        pl.BlockSpec((2 * SSD_HEADS, CHUNK), lambda bi, c: (0, cidx(bi, c))),
        full((1, DT_PAD)), full((SSD_HEADS, 1)), full((1, DT_PAD)), full((SSD_HEADS, 1)),
    ]
    dt_args = [dt, dtt, bias_p, bias_c, alog_p, alog_c]
    if final:
        z, yf, dskip_f, nw = extras
        in_specs = [chunk_rows(CONV_DIM)] + dt_specs + [
            chunk_rows(D_INNER), chunk_rows(D_INNER), full((1, D_INNER)), full((1, D_INNER))]
        args = [xbc] + dt_args + [z, yf, dskip_f, nw]
        out_specs = chunk_rows(D_INNER)
        out_shape = jax.ShapeDtypeStruct((t, D_INNER), BF16)
    else:
        in_specs = [
            chunk_rows(CONV_DIM),
            pl.BlockSpec((8, CONV_DIM), lambda bi, c: (jnp.maximum(cidx(bi, c) * rows8 - 1, 0), 0)),
            pl.BlockSpec((8, CONV_DIM), lambda bi, c: (jnp.minimum((cidx(bi, c) + 1) * rows8, last8), 0)),
            full((3, CONV_DIM)), full((1, CONV_DIM)),
        ] + dt_specs
        args = [xbc, xbc, xbc, cw, cb] + dt_args
        out_specs = [chunk_rows(D_INNER), chunk_rows(CONV_DIM)]
        out_shape = [jax.ShapeDtypeStruct((t, D_INNER), F32), jax.ShapeDtypeStruct((t, CONV_DIM), F32)]
    return pl.pallas_call(
        functools.partial(_ssd_kernel, rev=rev, final=final, nc=nc),
        grid=(b, nc),
        in_specs=in_specs,
        out_specs=out_specs,
        out_shape=out_shape,
        scratch_shapes=[pltpu.VMEM((D_STATE, D_INNER), F32)],
        compiler_params=_params(("parallel", "arbitrary")),
        name="ssd_bwd_final" if final else "ssd_fwd",
    )(*args)


def _merge_kernel(attn_ref, ssd_ref, gates_ref, x_ref, woa_ref, wos_ref, wout_ref, nw_ref, o_ref):
    a = _dot(attn_ref[...], woa_ref[...])
    s = _dot(ssd_ref[...], wos_ref[...])
    gates = gates_ref[...]
    merged = jax.nn.sigmoid(gates[:, :D_MODEL]) * a + jax.nn.sigmoid(gates[:, D_MODEL:]) * s
    mo = _dot(merged.astype(BF16), wout_ref[...])
    o_ref[...] = x_ref[...] + _rms(mo, nw_ref[...])


def merge_out(attn, ssd, gates, x, woa, wos, wout, nw, tm):
    t, d = x.shape
    rows = lambda w: pl.BlockSpec((tm, w), lambda i: (i, 0))
    full = lambda shape: pl.BlockSpec(shape, lambda i: (0, 0), pipeline_mode=pl.Buffered(1))
    return pl.pallas_call(
        _merge_kernel,
        grid=(t // tm,),
        in_specs=[rows(d), rows(d), rows(2 * d), rows(d), full((d, d)), full((d, d)), full((d, d)),
                  full((1, d))],
        out_specs=rows(d),
        out_shape=jax.ShapeDtypeStruct((t, d), F32),
        compiler_params=_params(("parallel",)),
        name="merge_out",
    )(attn, ssd, gates, x, woa, wos, wout, nw)


def _ffn_kernel(x_ref, xp_ref, xn_ref, gpre_ref, wa_ref, wg_ref, cwa_ref, cwg_ref, cba_ref, cbg_ref,
                wd_ref, gpost_ref, o_ref, h_ref, acc_ref, *, tm, tiles_per_seq, nf):
    i = pl.program_id(0)
    f = pl.program_id(1)

    @pl.when(f == 0)
    def _():
        gpre = gpre_ref[...]
        pos = i % tiles_per_seq
        h_ref[0:tm, :] = _rms(x_ref[...], gpre).astype(BF16)
        hp = jnp.where(pos == 0, 0.0, _rms(xp_ref[...], gpre))
        hn = jnp.where(pos == tiles_per_seq - 1, 0.0, _rms(xn_ref[...], gpre))
        h_ref[tm:tm + HALO, :] = hn.astype(BF16)
        h_ref[tm + HALO:tm + 2 * HALO, :] = hp.astype(BF16)
        acc_ref[...] = jnp.zeros_like(acc_ref)

    h = h_ref[...]
    n_ext = tm + 2 * HALO

    def conv_branch(w_ref, cw_ref, cb_ref):
        u = _dot(h, w_ref[...])
        um = u[0:tm, :]
        up = pltpu.roll(u, 1, axis=0)[0:tm, :]
        un = pltpu.roll(u, n_ext - 1, axis=0)[0:tm, :]
        cw = cw_ref[...]
        return cb_ref[...] + up * cw[0:1, :] + um * cw[1:2, :] + un * cw[2:3, :]

    a = conv_branch(wa_ref, cwa_ref, cba_ref)
    g = conv_branch(wg_ref, cwg_ref, cbg_ref)
    act = (_silu(g) * a).astype(BF16)
    acc_ref[...] += _dot(act, wd_ref[...])

    @pl.when(f == nf - 1)
    def _():
        o_ref[...] = x_ref[...] + _rms(acc_ref[...], gpost_ref[...])


def ffn(x, gpre, wa, wg, cwa, cwg, cba, cbg, wd, gpost, s, tm, tf):
    t, d = x.shape
    nf = D_FF_PAD // tf
    tiles_per_seq = s // tm
    blocks = tm // HALO
    last = t // HALO - 1
    full = lambda shape: pl.BlockSpec(shape, lambda i, f: (0, 0))
    colblk = lambda r: pl.BlockSpec((r, tf), lambda i, f: (0, f))
    return pl.pallas_call(
        functools.partial(_ffn_kernel, tm=tm, tiles_per_seq=tiles_per_seq, nf=nf),
        grid=(t // tm, nf),
        in_specs=[
            pl.BlockSpec((tm, d), lambda i, f: (i, 0)),
            pl.BlockSpec((HALO, d), lambda i, f: (jnp.maximum(i * blocks - 1, 0), 0)),
            pl.BlockSpec((HALO, d), lambda i, f: (jnp.minimum((i + 1) * blocks, last), 0)),
            full((1, d)),
            colblk(d), colblk(d), colblk(3), colblk(3), colblk(1), colblk(1),
            pl.BlockSpec((tf, d), lambda i, f: (f, 0)),
            full((1, d)),
        ],
        out_specs=pl.BlockSpec((tm, d), lambda i, f: (i, 0)),
        out_shape=jax.ShapeDtypeStruct((t, d), F32),
        scratch_shapes=[pltpu.VMEM((tm + 2 * HALO, d), BF16), pltpu.VMEM((tm, d), F32)],
        compiler_params=_params(("parallel", "arbitrary")),
        name="ffn",
    )(x, x, x, gpre, wa, wg, cwa, cwg, cba, cbg, wd, gpost)


def _tiles(s):
    return dict(
        tm_proj=min(256, s),
        tq=min(256, s), tk=min(512, s), depth=3,
        tm_merge=min(512, s),
        tm_ffn=min(1024, s), tf=256,
    )


def _layer(x, b, s, w, cfg):
    g_pre = w["norm_mix_pre"]
    qk, vt, z, xbc, gates, dt, dtt = in_proj(x, g_pre, w, cfg["tm_proj"])

    attn = diff_attention(qk, vt, w["slopes"], w["lam_consts"], w["lam_q1"], w["lam_k1"], w["lam_q2"],
                          w["lam_k2"], w["subln_col"], b, s, cfg["tq"], cfg["tk"], cfg["depth"])

    yf, u = ssd_pass(xbc, dt, dtt, w["conv_ssd_w"], w["conv_ssd_b"], w["bias_p"], w["bias_c"][0],
                     w["alog_p"], w["alog_c"][0], None, b, s, rev=False)
    ssd = ssd_pass(u, dt, dtt, w["conv_ssd_w"], w["conv_ssd_b"], w["bias_p"], w["bias_c"][1],
                   w["alog_p"], w["alog_c"][1], (z, yf, w["dskip_f"], w["ssd_norm"]), b, s, rev=True)

    x = merge_out(attn, ssd, gates, x, w["w_o_attn"], w["w_o_ssd"], w["w_out"], w["norm_mix_post"],
                  cfg["tm_merge"])
    x = ffn(x, w["norm_ffn_pre"], w["w_up_a"], w["w_up_g"], w["cw_a"], w["cw_g"], w["cb_a"], w["cb_g"],
            w["w_down"], w["norm_ffn_post"], s, cfg["tm_ffn"], cfg["tf"])
    return x


def _prepare_weights(norm_mix_pre, norm_mix_post, norm_ffn_pre, norm_ffn_post, w_in, lam_q1, lam_k1,
                     lam_q2, lam_k2, attn_subln, conv_ssd_w, conv_ssd_b, dt_bias, a_log, d_skip, ssd_norm,
                     w_o_attn, w_o_ssd, w_out, w_up, conv_ffn_w, conv_ffn_b, w_down):
    depth = w_in.shape[0]
    qk_cols = ATTN_HEADS * 2 * HEAD_DIM
    attn_w = ATTN_HEADS * V_DIM
    cuts = [0, 2 * qk_cols]
    for width in (attn_w, D_INNER, CONV_DIM, 2 * SSD_HEADS, 2 * D_MODEL):
        cuts.append(cuts[-1] + width)
    seg = lambda i: w_in[:, :, cuts[i]:cuts[i + 1]]
    w_dt = seg(4)
    row = lambda a: a[:, None, :]
    rep = lambda a: jnp.repeat(a, SSD_HEAD_DIM, axis=-1)
    pad_ff = lambda a: jnp.pad(a, [(0, 0)] * (a.ndim - 1) + [(0, D_FF_PAD - D_FF)])
    pad_dt = lambda a: jnp.pad(a.reshape(depth, 1, 2 * SSD_HEADS), ((0, 0), (0, 0), (0, DT_PAD - 2 * SSD_HEADS)))
    lam_init = [0.8 - 0.6 * math.exp(-0.3 * l) for l in range(depth)]
    return dict(
        norm_mix_pre=row(norm_mix_pre), norm_mix_post=row(norm_mix_post),
        norm_ffn_pre=row(norm_ffn_pre), norm_ffn_post=row(norm_ffn_post),
        w_qk=seg(0).astype(BF16),
        w_vt=jnp.swapaxes(seg(1), 1, 2).astype(BF16),
        w_z=seg(2).astype(BF16),
        w_xbc=seg(3).astype(BF16),
        w_dt=jnp.pad(w_dt, ((0, 0), (0, 0), (0, DT_PAD - 2 * SSD_HEADS))).astype(BF16),
        w_dtt=jnp.swapaxes(w_dt, 1, 2).astype(BF16),
        w_gates=seg(5).astype(BF16),
        slopes=jnp.tile(jnp.asarray([2.0 ** (-8.0 * (i + 1) / ATTN_HEADS) for i in range(ATTN_HEADS)],
                                    F32)[None], (depth, 1)),
        lam_consts=jnp.asarray([[li, 1.0 - li] for li in lam_init], F32),
        lam_q1=row(lam_q1), lam_k1=row(lam_k1), lam_q2=row(lam_q2), lam_k2=row(lam_k2),
        subln_col=attn_subln[:, :, None],
        conv_ssd_w=conv_ssd_w, conv_ssd_b=row(conv_ssd_b),
        bias_p=pad_dt(dt_bias), bias_c=dt_bias[..., None],
        alog_p=pad_dt(a_log), alog_c=a_log[..., None],
        dskip_f=row(rep(d_skip)), ssd_norm=row(ssd_norm),
        w_o_attn=w_o_attn.astype(BF16), w_o_ssd=w_o_ssd.astype(BF16), w_out=w_out.astype(BF16),
        w_up_a=pad_ff(w_up[:, :, :D_FF]).astype(BF16), w_up_g=pad_ff(w_up[:, :, D_FF:]).astype(BF16),
        cw_a=pad_ff(conv_ffn_w[:, :, :D_FF]), cw_g=pad_ff(conv_ffn_w[:, :, D_FF:]),
        cb_a=row(pad_ff(conv_ffn_b[:, :D_FF])), cb_g=row(pad_ff(conv_ffn_b[:, D_FF:])),
        w_down=jnp.pad(w_down, ((0, 0), (0, D_FF_PAD - D_FF), (0, 0))).astype(BF16),
    )


def kernel(x_prompt, x_sample, norm_mix_pre, norm_mix_post, norm_ffn_pre, norm_ffn_post, w_in, lam_q1, lam_k1, lam_q2, lam_k2, attn_subln, conv_ssd_w, conv_ssd_b, dt_bias, a_log, d_skip, ssd_norm, w_o_attn, w_o_ssd, w_out, w_up, conv_ffn_w, conv_ffn_b, w_down):
    weights = _prepare_weights(norm_mix_pre, norm_mix_post, norm_ffn_pre, norm_ffn_post, w_in, lam_q1,
                               lam_k1, lam_q2, lam_k2, attn_subln, conv_ssd_w, conv_ssd_b, dt_bias, a_log,
                               d_skip, ssd_norm, w_o_attn, w_o_ssd, w_out, w_up, conv_ffn_w, conv_ffn_b,
                               w_down)
    groups = []
    for xg in (x_prompt, x_sample):
        b, s, d = xg.shape
        groups.append((b, s, _tiles(s)))

    def step(carry, w):
        out = tuple(_layer(x, b, s, w, cfg) for x, (b, s, cfg) in zip(carry, groups))
        return out, None

    init = tuple(xg.reshape(-1, xg.shape[-1]) for xg in (x_prompt, x_sample))
    out, _ = lax.scan(step, init, weights)
    return tuple(o.reshape(xg.shape) for o, xg in zip(out, (x_prompt, x_sample)))
```
